```python
import jax, jax.numpy as jnp
from jax import lax
import numpy as np

D_MODEL = 4096
BATCH = 2
SEQ = 8192
DEPTH = 2

GRID_W = 64
CTX_LEN = 256
EPS = 1e-6
NEG_INF = -1e30
N_MOD = 6

A_HEADS = 16
A_DK = 128
A_DV = 128
A_QK = A_HEADS * A_DK
A_WIDTH = A_HEADS * A_DV
A_CHUNK = 32
B_HEADS = 8
B_DH = 128
B_WIDTH = B_HEADS * B_DH
NA_ROWS = 8
NA_COLS = 16
C_HEADS = 16
C_KV_HEADS = 2
C_DH = 64
C_WIDTH = C_HEADS * C_DH
C_WINDOW = 128
C_BLOCK = 128
ROPE_BASE = 10000.0

N_BRANCH = 3
MIX_WIDTH = A_WIDTH + B_WIDTH + C_WIDTH
D_FF = -(-8 * D_MODEL // (3 * 256)) * 256

IN_SIZES = (A_QK, A_WIDTH, A_QK, A_QK, A_WIDTH,
            B_WIDTH, B_WIDTH, B_WIDTH,
            C_WIDTH, C_KV_HEADS * C_DH, C_KV_HEADS * C_DH,
            N_BRANCH * D_MODEL)
IN_WIDTH = sum(IN_SIZES)

kernel_name = 'hybrid_hgrn2_natten_swa_dit_block'


def rms_norm(x, g):
    xf = x.astype(jnp.float32)
    y = xf * lax.rsqrt(jnp.mean(xf * xf, axis=-1, keepdims=True) + EPS)
    return (y * g.astype(jnp.float32)).astype(x.dtype)


def modulate(h, shift, scale):
    return h * (1 + scale) + shift


def heads(a, n):
    b, t, w = a.shape
    return a.reshape(b, t, n, w // n).transpose(0, 2, 1, 3)


def merge_heads(a):
    b, h, t, d = a.shape
    return a.transpose(0, 2, 1, 3).reshape(b, t, h * d)


def split_cols(p):
    idx = np.cumsum(IN_SIZES)[:-1].tolist()
    return jnp.split(p, idx, axis=-1)


def swiglu(h, w_gate, w_up, w_down):
    return (jax.nn.silu(h @ w_gate) * (h @ w_up)) @ w_down


def axial_rope(x):
    t, d = x.shape[2], x.shape[3]
    half = d // 2
    pos = jnp.arange(t)
    row = (pos // GRID_W).astype(jnp.float32)
    col = (pos % GRID_W).astype(jnp.float32)
    inv = ROPE_BASE ** (-jnp.arange(0, half, 2, dtype=jnp.float32) / half)
    xf = x.astype(jnp.float32)

    def rot(xa, p):
        ang = p[:, None] * inv[None, :]
        cos, sin = jnp.cos(ang), jnp.sin(ang)
        x1, x2 = xa[..., :half // 2], xa[..., half // 2:]
        return jnp.concatenate([x1 * cos - x2 * sin, x1 * sin + x2 * cos], axis=-1)

    out = jnp.concatenate([rot(xf[..., :half], row), rot(xf[..., half:], col)], axis=-1)
    return out.astype(x.dtype)


def chunk_scan(q, k, v, logf, s0):
    b, h, t, dk = q.shape
    dv = v.shape[-1]
    n = t // A_CHUNK
    q, k, v, logf = [a.reshape(b, h, n, A_CHUNK, a.shape[-1]) for a in (q, k, v, logf)]
    cum = jnp.cumsum(logf, axis=3)
    last = cum[:, :, :, -1:, :]
    q_dec = q * jnp.exp(cum)
    k_inv = k * jnp.exp(-cum)
    k_end = k * jnp.exp(last - cum)
    lower = jnp.tril(jnp.ones((A_CHUNK, A_CHUNK), dtype=bool))
    att = jnp.where(lower, jnp.einsum('bhncd,bhnsd->bhncs', q_dec, k_inv), 0.0)
    o_intra = jnp.einsum('bhncs,bhnsv->bhncv', att, v)
    chunk_decay = jnp.exp(last[:, :, :, 0, :])

    def step(s, inp):
        qd, ke, vv, dec = inp
        o = jnp.einsum('bhcd,bhdv->bhcv', qd, s)
        s = dec[..., None] * s + jnp.einsum('bhcd,bhcv->bhdv', ke, vv)
        return s, o

    xs = tuple(jnp.moveaxis(a, 2, 0) for a in (q_dec, k_end, v, chunk_decay))
    s_final, o_inter = lax.scan(step, s0, xs)
    o = o_intra + jnp.moveaxis(o_inter, 0, 2)
    return o.reshape(b, h, t, dv), s_final


def hgrn2_mixer(lat, cx, lb, norm_g):
    dt = lat[0].dtype
    b = lat[0].shape[0]

    def prep(parts):
        q, i = parts[0], parts[1]
        return (heads(jax.nn.silu(q.astype(jnp.float32)), A_HEADS),
                heads(i.astype(jnp.float32), A_HEADS))

    q_l, v_l = prep(lat)
    q_c, v_c = prep(cx)
    out_l, out_c = 0.0, 0.0
    for d_idx, rev in ((0, False), (1, True)):
        lb_d = lb[d_idx].reshape(1, A_HEADS, 1, A_DK)

        def gates(fpre):
            f = lb_d + (1 - lb_d) * jax.nn.sigmoid(heads(fpre.astype(jnp.float32), A_HEADS))
            return jnp.log(f), 1 - f

        logf_l, k_l = gates(lat[2 + d_idx])
        logf_c, k_c = gates(cx[2 + d_idx])
        seq_l = [q_l, k_l, v_l, logf_l]
        seq_c = [q_c, k_c, v_c, logf_c]
        if rev:
            seq_l = [jnp.flip(a, axis=2) for a in seq_l]
            seq_c = [jnp.flip(a, axis=2) for a in seq_c]
        s0 = jnp.zeros((b, A_HEADS, A_DK, A_DV), jnp.float32)
        o_c, s_c = chunk_scan(*seq_c, s0)
        o_l, _ = chunk_scan(*seq_l, s_c)
        if rev:
            o_l = jnp.flip(o_l, axis=2)
            o_c = jnp.flip(o_c, axis=2)
        out_l = out_l + o_l
        out_c = out_c + o_c

    def readout(o, g):
        o = o * lax.rsqrt(jnp.mean(o * o, axis=-1, keepdims=True) + EPS) * norm_g.astype(jnp.float32)
        return (merge_heads(o) * jax.nn.silu(g.astype(jnp.float32))).astype(dt)

    return readout(out_l, lat[4]), readout(out_c, cx[4])


def neighborhood_attention(q, k, v, kc, vc, rpb):
    b, h, t, d = q.shape
    rows = t // GRID_W
    kr = min(NA_ROWS, rows)
    r = jnp.arange(rows)
    col = jnp.arange(GRID_W)
    row_start = jnp.clip(r - kr // 2, 0, rows - kr)
    key_rows = row_start[:, None] + jnp.arange(kr)[None, :]
    col_start = jnp.clip(col - NA_COLS // 2, 0, GRID_W - NA_COLS)
    col_mask = (col[None, :] >= col_start[:, None]) & (col[None, :] < col_start[:, None] + NA_COLS)
    qg = q.reshape(b, h, rows, GRID_W, d) * (d ** -0.5)
    kg = k.reshape(b, h, rows, GRID_W, d)[:, :, key_rows]
    vg = v.reshape(b, h, rows, GRID_W, d)[:, :, key_rows]
    s_nb = jnp.einsum('bhrqd,bhrakd->bhrqak', qg, kg).astype(jnp.float32)
    row_off = key_rows - r[:, None] + NA_ROWS - 1
    col_off = jnp.clip(col[None, :] - col[:, None] + NA_COLS - 1, 0, 2 * NA_COLS - 2)
    bias = rpb.astype(jnp.float32)[:, row_off][..., col_off].transpose(0, 1, 3, 2, 4)
    s_nb = jnp.where(col_mask[None, None, None, :, None, :], s_nb + bias[None], NEG_INF)
    s_nb = s_nb.reshape(b, h, rows, GRID_W, kr * GRID_W)
    s_ctx = jnp.einsum('bhrqd,bhcd->bhrqc', qg, kc).astype(jnp.float32)
    p = jax.nn.softmax(jnp.concatenate([s_nb, s_ctx], axis=-1), axis=-1).astype(v.dtype)
    n_nb = kr * GRID_W
    o = (jnp.einsum('bhrqk,bhrkd->bhrqd', p[..., :n_nb], vg.reshape(b, h, rows, n_nb, d))
         + jnp.einsum('bhrqc,bhcd->bhrqd', p[..., n_nb:], vc))
    return o.reshape(b, h, t, d)


def window_gqa(q, k, v, kc, vc, sink):
    b, hq, t, d = q.shape
    hkv = k.shape[1]
    g = hq // hkv
    nb = t // C_BLOCK
    qb = q.reshape(b, hkv, g, nb, C_BLOCK, d) * (d ** -0.5)

    def band(a):
        ap = jnp.pad(a, ((0, 0), (0, 0), (C_BLOCK, C_BLOCK), (0, 0))).reshape(b, hkv, nb + 2, C_BLOCK, d)
        return jnp.concatenate([ap[:, :, :-2], ap[:, :, 1:-1], ap[:, :, 2:]], axis=3)

    kw, vw = band(k), band(v)
    blk = jnp.arange(nb)[:, None, None]
    qpos = blk * C_BLOCK + jnp.arange(C_BLOCK)[None, :, None]
    kpos = blk * C_BLOCK - C_BLOCK + jnp.arange(3 * C_BLOCK)[None, None, :]
    valid = (kpos >= 0) & (kpos < t) & (jnp.abs(qpos - kpos) <= C_WINDOW)
    s_w = jnp.einsum('bkgnqd,bknsd->bkgnqs', qb, kw).astype(jnp.float32)
    s_w = jnp.where(valid[None, None, None], s_w, NEG_INF)
    s_c = jnp.einsum('bkgnqd,bkcd->bkgnqc', qb, kc).astype(jnp.float32)
    sink_col = jnp.broadcast_to(sink.astype(jnp.float32).reshape(1, hkv, g, 1, 1, 1), s_w.shape[:-1] + (1,))
    p = jax.nn.softmax(jnp.concatenate([s_w, s_c, sink_col], axis=-1), axis=-1).astype(v.dtype)
    n_w = 3 * C_BLOCK
    n_c = kc.shape[2]
    o = (jnp.einsum('bkgnqs,bknsd->bkgnqd', p[..., :n_w], vw)
         + jnp.einsum('bkgnqc,bkcd->bkgnqd', p[..., n_w:n_w + n_c], vc))
    return o.reshape(b, hq, t, d)


def context_attention(q, k, v, sink):
    b, hq, l, d = q.shape
    hkv = k.shape[1]
    g = hq // hkv
    qg = q.reshape(b, hkv, g, l, d) * (d ** -0.5)
    s = jnp.einsum('bkgqd,bkcd->bkgqc', qg, k).astype(jnp.float32)
    if sink is not None:
        sink_col = jnp.broadcast_to(sink.astype(jnp.float32).reshape(1, hkv, g, 1, 1), s.shape[:-1] + (1,))
        s = jnp.concatenate([s, sink_col], axis=-1)
    p = jax.nn.softmax(s, axis=-1)[..., :l].astype(v.dtype)
    return jnp.einsum('bkgqc,bkcd->bkgqd', p, v).reshape(b, hq, l, d)


def merge_branches(o_a, o_b, o_c, gate_pre, w_branch, w_out):
    br_a = o_a @ w_branch[:A_WIDTH]
    br_b = o_b @ w_branch[A_WIDTH:A_WIDTH + B_WIDTH]
    br_c = o_c @ w_branch[A_WIDTH + B_WIDTH:]
    g_a, g_b, g_c = jnp.split(jax.nn.sigmoid(gate_pre), N_BRANCH, axis=-1)
    return (g_a * br_a + g_b * br_b + g_c * br_c) @ w_out


def token_mixers(h_lat, h_ctx, w_in, lb, a_norm_g, rpb, sink, w_branch, w_out, need_ctx):
    pl = split_cols(h_lat @ w_in)
    pc = split_cols(h_ctx @ w_in)
    a_lat, a_ctx = hgrn2_mixer(tuple(pl[0:5]), tuple(pc[0:5]), lb, a_norm_g)
    bq_l, bk_l, bv_l = [heads(a, B_HEADS) for a in pl[5:8]]
    bq_c, bk_c, bv_c = [heads(a, B_HEADS) for a in pc[5:8]]
    b_lat = merge_heads(neighborhood_attention(bq_l, bk_l, bv_l, bk_c, bv_c, rpb))
    cq_l = axial_rope(heads(pl[8], C_HEADS))
    ck_l = axial_rope(heads(pl[9], C_KV_HEADS))
    cv_l = heads(pl[10], C_KV_HEADS)
    cq_c = heads(pc[8], C_HEADS)
    ck_c = heads(pc[9], C_KV_HEADS)
    cv_c = heads(pc[10], C_KV_HEADS)
    c_lat = merge_heads(window_gqa(cq_l, ck_l, cv_l, ck_c, cv_c, sink))
    y_lat = merge_branches(a_lat, b_lat, c_lat, pl[11], w_branch, w_out)
    if not need_ctx:
        return y_lat, None
    b_ctx = merge_heads(context_attention(bq_c, bk_c, bv_c, None))
    c_ctx_o = merge_heads(context_attention(cq_c, ck_c, cv_c, sink))
    y_ctx = merge_branches(a_ctx, b_ctx, c_ctx_o, pc[11], w_branch, w_out)
    return y_lat, y_ctx


def setup_inputs(seed: int = 0) -> dict:
    key = jax.random.key(seed)
    ks = jax.random.split(key, 20)
    D = D_MODEL

    def nrm(k, shape, scale):
        return jax.random.normal(k, shape, jnp.float32) * scale

    return {
        'x': nrm(ks[0], (BATCH, SEQ, D), 1.0),
        'c': nrm(ks[1], (BATCH, D), 1.0),
        'ctx': nrm(ks[2], (BATCH, CTX_LEN, D), 1.0),
        'c_ctx': nrm(ks[3], (D,), 1.0),
        'norm1_g': 1.0 + nrm(ks[4], (DEPTH, D), 0.02),
        'norm2_g': 1.0 + nrm(ks[5], (DEPTH, D), 0.02),
        'w_mod': nrm(ks[6], (DEPTH, D, N_MOD * D), 0.5 * D ** -0.5),
        'b_mod': nrm(ks[7], (DEPTH, N_MOD * D), 0.02),
        'w_in': nrm(ks[8], (DEPTH, D, IN_WIDTH), D ** -0.5),
        'hgrn_lb': nrm(ks[9], (DEPTH, 2, A_QK), 0.1),
        'a_norm_g': 1.0 + nrm(ks[10], (DEPTH, A_DV), 0.02),
        'na_rpb': nrm(ks[11], (DEPTH, B_HEADS, 2 * NA_ROWS - 1, 2 * NA_COLS - 1), 0.1),
        'c_sink': nrm(ks[12], (DEPTH, C_HEADS), 1.0),
        'w_branch': nrm(ks[13], (DEPTH, MIX_WIDTH, D), B_WIDTH ** -0.5),
        'w_out': nrm(ks[14], (DEPTH, D, D), D ** -0.5),
        'w_ffn_gate': nrm(ks[15], (DEPTH, D, D_FF), D ** -0.5),
        'w_ffn_up': nrm(ks[16], (DEPTH, D, D_FF), D ** -0.5),
        'w_ffn_down': nrm(ks[17], (DEPTH, D_FF, D), D_FF ** -0.5),
        'final_norm_g': 1.0 + nrm(ks[18], (D,), 0.02),
    }


def reference(x, c, ctx, c_ctx, norm1_g, norm2_g, w_mod, b_mod, w_in, hgrn_lb, a_norm_g,
              na_rpb, c_sink, w_branch, w_out, w_ffn_gate, w_ffn_up, w_ffn_down, final_norm_g):
    lb_w = jax.nn.softmax(hgrn_lb.astype(jnp.float32), axis=0)
    lower_bounds = jnp.cumsum(lb_w, axis=0) - lb_w[:1]
    for l in range(DEPTH):
        need_ctx = l < DEPTH - 1
        mod_l = jnp.split((jax.nn.silu(c) @ w_mod[l] + b_mod[l])[:, None, :], N_MOD, axis=-1)
        mod_c = jnp.split((jax.nn.silu(c_ctx) @ w_mod[l] + b_mod[l])[None, None, :], N_MOD, axis=-1)
        h_lat = modulate(rms_norm(x, norm1_g[l]), mod_l[0], mod_l[1])
        h_ctx = modulate(rms_norm(ctx, norm1_g[l]), mod_c[0], mod_c[1])
        y_lat, y_ctx = token_mixers(h_lat, h_ctx, w_in[l], lower_bounds[l], a_norm_g[l], na_rpb[l],
                                    c_sink[l], w_branch[l], w_out[l], need_ctx)
        x = x + mod_l[2] * y_lat
        x = x + mod_l[5] * swiglu(modulate(rms_norm(x, norm2_g[l]), mod_l[3], mod_l[4]),
                                  w_ffn_gate[l], w_ffn_up[l], w_ffn_down[l])
        if need_ctx:
            ctx = ctx + mod_c[2] * y_ctx
            ctx = ctx + mod_c[5] * swiglu(modulate(rms_norm(ctx, norm2_g[l]), mod_c[3], mod_c[4]),
                                          w_ffn_gate[l], w_ffn_up[l], w_ffn_down[l])
    return rms_norm(x, final_norm_g)
```

```python
import functools

import jax
import jax.numpy as jnp
import numpy as np
from jax import lax
from jax.experimental import pallas as pl
from jax.experimental.pallas import tpu as pltpu

GRID_W = 64
EPS = 1e-6
NEG_INF = -1e30
N_MOD = 6
A_HEADS, A_DK, A_DV, A_CHUNK = 16, 128, 128, 32
A_QK = A_HEADS * A_DK
A_WIDTH = A_HEADS * A_DV
B_HEADS, B_DH = 8, 128
B_WIDTH = B_HEADS * B_DH
NA_ROWS, NA_COLS = 8, 16
C_HEADS, C_KV_HEADS, C_DH = 16, 2, 64
C_WIDTH = C_HEADS * C_DH
C_KV_WIDTH = C_KV_HEADS * C_DH
C_WINDOW = 128
C_BLOCK = 128
ROPE_BASE = 10000.0
N_BRANCH = 3

LANES = 128
VMEM_LIMIT_BYTES = 56 * 1024 * 1024

HGRN_BLOCK = 256
NA_QBLOCK = 256

_NT = (((1,), (1,)), ((), ()))
_TN = (((0,), (0,)), ((), ()))

bf16 = jnp.bfloat16
f32 = jnp.float32


def _params(*sem):
    return pltpu.CompilerParams(dimension_semantics=sem, vmem_limit_bytes=VMEM_LIMIT_BYTES)


def _row_tile(rows, want):
    t = min(rows, want)
    while rows % t:
        t //= 2
    return t


def _sigmoid(x):
    return 1.0 / (1.0 + jnp.exp(-x))


def _silu(x):
    return x * _sigmoid(x)


def _mod_kernel(c_ref, w_ref, b_ref, o_ref):
    a = _silu(c_ref[...]).astype(bf16)
    o_ref[...] = jnp.dot(a, w_ref[...].astype(bf16), preferred_element_type=f32) + b_ref[...]


def _modulation(c_rows, w_mod, b_mod):
    depth, d, n = w_mod.shape
    rows = c_rows.shape[0]
    tn = 512
    return pl.pallas_call(
        _mod_kernel,
        grid=(depth, n // tn),
        in_specs=[
            pl.BlockSpec((rows, d), lambda l, j: (0, 0)),
            pl.BlockSpec((None, d, tn), lambda l, j: (l, 0, j)),
            pl.BlockSpec((None, 1, tn), lambda l, j: (l, 0, j)),
        ],
        out_specs=pl.BlockSpec((None, rows, tn), lambda l, j: (l, 0, j)),
        out_shape=jax.ShapeDtypeStruct((depth, rows, n), f32),
        compiler_params=_params("arbitrary", "arbitrary"),
        name="modulation",
    )(c_rows, w_mod, b_mod.reshape(depth, 1, n))


def _norm_mod_kernel(x_ref, g_ref, shift_ref, scale_ref, o_ref):
    x = x_ref[...]
    y = x * lax.rsqrt(jnp.mean(x * x, axis=-1, keepdims=True) + EPS) * g_ref[...]
    o_ref[...] = (y * (1.0 + scale_ref[...]) + shift_ref[...]).astype(o_ref.dtype)


def _norm_kernel(x_ref, g_ref, o_ref):
    x = x_ref[...]
    y = x * lax.rsqrt(jnp.mean(x * x, axis=-1, keepdims=True) + EPS) * g_ref[...]
    o_ref[...] = y.astype(o_ref.dtype)


def _norm_modulate(x, g, mod, shift_idx, rows_per_group):
    rows, d = x.shape
    tm = _row_tile(rows_per_group, 256)
    per = rows_per_group // tm
    return pl.pallas_call(
        _norm_mod_kernel,
        grid=(rows // tm,),
        in_specs=[
            pl.BlockSpec((tm, d), lambda i: (i, 0)),
            pl.BlockSpec((1, d), lambda i: (0, 0)),
            pl.BlockSpec((None, 1, d), lambda i: (i // per, 0, shift_idx)),
            pl.BlockSpec((None, 1, d), lambda i: (i // per, 0, shift_idx + 1)),
        ],
        out_specs=pl.BlockSpec((tm, d), lambda i: (i, 0)),
        out_shape=jax.ShapeDtypeStruct((rows, d), bf16),
        compiler_params=_params("arbitrary"),
        name="norm_modulate",
    )(x, g.reshape(1, d), mod, mod)


def _final_norm(x, g):
    rows, d = x.shape
    tm = _row_tile(rows, 256)
    return pl.pallas_call(
        _norm_kernel,
        grid=(rows // tm,),
        in_specs=[pl.BlockSpec((tm, d), lambda i: (i, 0)), pl.BlockSpec((1, d), lambda i: (0, 0))],
        out_specs=pl.BlockSpec((tm, d), lambda i: (i, 0)),
        out_shape=jax.ShapeDtypeStruct((rows, d), f32),
        compiler_params=_params("arbitrary"),
        name="final_norm",
    )(x, g.reshape(1, d))


def _proj_kernel(h_ref, w_ref, o_ref):
    o_ref[...] = jnp.dot(h_ref[...], w_ref[...], preferred_element_type=f32).astype(o_ref.dtype)


def _proj_scale_kernel(h_ref, w_ref, s_ref, o_ref):
    acc = jnp.dot(h_ref[...], w_ref[...], preferred_element_type=f32)
    o_ref[...] = (acc * s_ref[...]).astype(o_ref.dtype)


def _proj_sigmoid_kernel(h_ref, w_ref, o_ref):
    acc = jnp.dot(h_ref[...], w_ref[...], preferred_element_type=f32)
    o_ref[...] = _sigmoid(acc).astype(o_ref.dtype)


def _rope_swap(x):
    n = x.shape[-1]
    lane = lax.broadcasted_iota(jnp.int32, x.shape, x.ndim - 1)
    up = pltpu.roll(x, n - 16, x.ndim - 1)
    down = pltpu.roll(x, 16, x.ndim - 1)
    return jnp.where((lane % 32) < 16, up, down)


def _proj_rope_kernel(h_ref, w_ref, s_ref, cos_ref, sin_ref, o_ref, *, rope_cols):
    acc = jnp.dot(h_ref[...], w_ref[...], preferred_element_type=f32)
    cos = cos_ref[...]
    sin = sin_ref[...]
    n = acc.shape[-1]
    for j in range(n // LANES):
        x = acc[:, j * LANES:(j + 1) * LANES]
        if j * LANES < rope_cols:
            x = x * cos + _rope_swap(x) * sin
        o_ref[:, j * LANES:(j + 1) * LANES] = (x * s_ref[:, j * LANES:(j + 1) * LANES]).astype(o_ref.dtype)


def _project(h, w, out_dtype, tn, *, col_scale=None, sigmoid=False, rope=None, name="proj"):
    rows, d = h.shape
    n = w.shape[1]
    tm = _row_tile(rows, 1024)
    grid = (rows // tm, n // tn)
    h_spec = pl.BlockSpec((tm, d), lambda i, j: (i, 0), pipeline_mode=pl.Buffered(1))
    w_spec = pl.BlockSpec((d, tn), lambda i, j: (0, j))
    o_spec = pl.BlockSpec((tm, tn), lambda i, j: (i, j))
    s_spec = pl.BlockSpec((1, tn), lambda i, j: (0, j))
    args, specs = [h, w], [h_spec, w_spec]
    if rope is not None:
        cos, sin, rope_cols = rope
        tok_tiles = cos.shape[0] // tm
        kern = functools.partial(_proj_rope_kernel, rope_cols=rope_cols)
        t_spec = pl.BlockSpec((tm, LANES), lambda i, j: (i % tok_tiles, 0))
        args += [col_scale, cos, sin]
        specs += [s_spec, t_spec, t_spec]
    elif sigmoid:
        kern = _proj_sigmoid_kernel
    elif col_scale is not None:
        kern = _proj_scale_kernel
        args.append(col_scale)
        specs.append(s_spec)
    else:
        kern = _proj_kernel
    return pl.pallas_call(
        kern,
        grid=grid,
        in_specs=specs,
        out_specs=o_spec,
        out_shape=jax.ShapeDtypeStruct((rows, n), out_dtype),
        compiler_params=_params("arbitrary", "arbitrary"),
        name=name,
    )(*args)


def _hgrn_kernel(*refs, rev, has_s0, emit_state, readout):
    it = iter(refs)
    q_ref, v_ref, f_ref, lb_ref = next(it), next(it), next(it), next(it)
    s0_ref = next(it) if has_s0 else None
    if readout:
        g_ref, oprev_ref, ng_ref = next(it), next(it), next(it)
    o_ref = next(it)
    sout_ref = next(it) if emit_state else None
    st_ref = next(it)

    blk = pl.program_id(2)
    nblk = pl.num_programs(2)
    tb = q_ref.shape[0]
    nchunk = tb // A_CHUNK

    @pl.when(blk == 0)
    def _():
        if has_s0:
            st_ref[...] = s0_ref[...]
        else:
            st_ref[...] = jnp.zeros_like(st_ref)

    lb = lb_ref[...]
    f = lb + (1.0 - lb) * _sigmoid(f_ref[...])
    logf = jnp.log(f)
    k = 1.0 - f

    row = lax.broadcasted_iota(jnp.int32, (tb, tb), 0)
    col = lax.broadcasted_iota(jnp.int32, (tb, tb), 1)
    same = (row // A_CHUNK) == (col // A_CHUNK)
    causal = (col >= row) if rev else (col <= row)
    mask = same & causal
    tri = jnp.where(mask, 1.0, 0.0).astype(bf16)

    hi = logf.astype(bf16)
    r1 = logf - hi.astype(f32)
    mid = r1.astype(bf16)
    lo = (r1 - mid.astype(f32)).astype(bf16)
    cum = (jnp.dot(tri, hi, preferred_element_type=f32)
           + jnp.dot(tri, mid, preferred_element_type=f32)
           + jnp.dot(tri, lo, preferred_element_type=f32))

    q_dec = (_silu(q_ref[...]) * jnp.exp(cum)).astype(bf16)
    k_inv = (k * jnp.exp(-cum)).astype(bf16)
    v = v_ref[...].astype(bf16)
    att = lax.dot_general(q_dec, k_inv, _NT, preferred_element_type=f32)
    att = jnp.where(mask, att, 0.0).astype(bf16)
    o_intra = jnp.dot(att, v, preferred_element_type=f32)

    order = range(nchunk - 1, -1, -1) if rev else range(nchunk)
    upd, dec = {}, {}
    for j in order:
        sl = slice(j * A_CHUNK, (j + 1) * A_CHUNK)
        last = j * A_CHUNK if rev else (j + 1) * A_CHUNK - 1
        tot = cum[last:last + 1, :]
        k_end = (k[sl] * jnp.exp(tot - cum[sl])).astype(bf16)
        upd[j] = lax.dot_general(v[sl], k_end, _TN, preferred_element_type=f32)
        dec[j] = jnp.exp(tot)

    s = st_ref[...]
    before = {}
    for j in order:
        before[j] = s.astype(bf16)
        s = s * dec[j] + upd[j]
    st_ref[...] = s

    o_inter = [lax.dot_general(q_dec[j * A_CHUNK:(j + 1) * A_CHUNK], before[j], _NT,
                               preferred_element_type=f32) for j in range(nchunk)]
    o = o_intra + jnp.concatenate(o_inter, axis=0)

    if readout:
        o = o + oprev_ref[...]
        o = o * lax.rsqrt(jnp.mean(o * o, axis=-1, keepdims=True) + EPS) * ng_ref[...]
        o_ref[...] = (o * _silu(g_ref[...])).astype(o_ref.dtype)
    else:
        o_ref[...] = o

    if emit_state:
        @pl.when(blk == nblk - 1)
        def _():
            sout_ref[...] = s


def _hgrn_scan(proj, lb_dir, seq, direction, *, s0=None, emit_state=False, readout=None):
    rows = proj.shape[0]
    batch = rows // seq
    tb = min(HGRN_BLOCK, seq)
    nblk = seq // tb
    rev = direction == 1

    def tok(b, h, i):
        return b * nblk + (nblk - 1 - i if rev else i)

    def col_spec(group):
        return pl.BlockSpec((tb, A_DK), lambda b, h, i: (tok(b, h, i), group * A_HEADS + h))

    args = [proj, proj, proj, lb_dir.reshape(1, A_QK)]
    specs = [col_spec(0), col_spec(1), col_spec(2 + direction),
             pl.BlockSpec((1, A_DK), lambda b, h, i: (0, h))]
    if s0 is not None:
        args.append(s0)
        specs.append(pl.BlockSpec((None, None, A_DV, A_DK), lambda b, h, i: (b, h, 0, 0)))
    if readout is not None:
        o_prev, norm_g = readout
        args += [proj, o_prev, norm_g.reshape(1, A_DV)]
        specs += [col_spec(4),
                  pl.BlockSpec((tb, A_DV), lambda b, h, i: (tok(b, h, i), h)),
                  pl.BlockSpec((1, A_DV), lambda b, h, i: (0, 0))]
    out_shape = [jax.ShapeDtypeStruct((rows, A_WIDTH), bf16 if readout is not None else f32)]
    out_specs = [pl.BlockSpec((tb, A_DV), lambda b, h, i: (tok(b, h, i), h))]
    if emit_state:
        out_shape.append(jax.ShapeDtypeStruct((batch, A_HEADS, A_DV, A_DK), f32))
        out_specs.append(pl.BlockSpec((None, None, A_DV, A_DK), lambda b, h, i: (b, h, 0, 0)))
    kern = functools.partial(_hgrn_kernel, rev=rev, has_s0=s0 is not None,
                             emit_state=emit_state, readout=readout is not None)
    res = pl.pallas_call(
        kern,
        grid=(batch, A_HEADS, nblk),
        in_specs=specs,
        out_specs=out_specs,
        out_shape=out_shape,
        scratch_shapes=[pltpu.VMEM((A_DV, A_DK), f32)],
        compiler_params=_params("arbitrary", "arbitrary", "arbitrary"),
        name="hgrn_scan",
    )(*args)
    return res if emit_state else res[0]


def _hgrn_mixer(proj_lat, proj_ctx, lb, norm_g, seq, ctx_len, need_ctx):
    o_c_f, s_f = _hgrn_scan(proj_ctx, lb[0], ctx_len, 0, emit_state=True)
    o_l_f = _hgrn_scan(proj_lat, lb[0], seq, 0, s0=s_f)
    if need_ctx:
        a_ctx, s_b = _hgrn_scan(proj_ctx, lb[1], ctx_len, 1, emit_state=True, readout=(o_c_f, norm_g))
    else:
        _, s_b = _hgrn_scan(proj_ctx, lb[1], ctx_len, 1, emit_state=True)
        a_ctx = None
    a_lat = _hgrn_scan(proj_lat, lb[1], seq, 1, s0=s_b, readout=(o_l_f, norm_g))
    return a_lat, a_ctx


def _attend(parts, sink=None):
    m = parts[0][0].max(axis=-1, keepdims=True)
    for s, _ in parts[1:]:
        m = jnp.maximum(m, s.max(axis=-1, keepdims=True))
    if sink is not None:
        m = jnp.maximum(m, sink)
    den = None
    acc = None
    for s, v in parts:
        e = jnp.exp(s - m)
        d = e.sum(axis=-1, keepdims=True)
        o = jnp.dot(e.astype(bf16), v, preferred_element_type=f32)
        den = d if den is None else den + d
        acc = o if acc is None else acc + o
    if sink is not None:
        den = den + jnp.exp(sink - m)
    return acc / den


def _na_kernel(q_ref, k_ref, v_ref, kc_ref, vc_ref, bias_ref, mask_ref, o_ref, *, grid_rows):
    step = pl.program_id(2)
    rows_per_step = q_ref.shape[0] // GRID_W
    span = NA_ROWS * GRID_W
    kc = kc_ref[...]
    vc = vc_ref[...]
    valid = mask_ref[...] > 0.0
    for rl in range(rows_per_step):
        r = step * rows_per_step + rl
        row_start = jnp.clip(r - NA_ROWS // 2, 0, grid_rows - NA_ROWS)
        off = row_start - r + NA_ROWS - 1
        start = pl.multiple_of(row_start * GRID_W, GRID_W)
        q = q_ref[rl * GRID_W:(rl + 1) * GRID_W, :]
        kn = k_ref[pl.ds(start, span), :]
        vn = v_ref[pl.ds(start, span), :]
        s_nb = lax.dot_general(q, kn, _NT, preferred_element_type=f32)
        s_nb = jnp.where(valid, s_nb + bias_ref[off], NEG_INF)
        s_cx = lax.dot_general(q, kc, _NT, preferred_element_type=f32)
        o = _attend([(s_nb, vn), (s_cx, vc)])
        o_ref[rl * GRID_W:(rl + 1) * GRID_W, :] = o.astype(o_ref.dtype)


def _neighborhood_attention(qkv_lat, qkv_ctx, bias_win, col_mask, seq, ctx_len):
    rows = qkv_lat.shape[0]
    batch = rows // seq
    tq = min(NA_QBLOCK, seq)
    nq = seq // tq
    grid_rows = seq // GRID_W
    span = NA_ROWS * GRID_W
    kern = functools.partial(_na_kernel, grid_rows=grid_rows)
    return pl.pallas_call(
        kern,
        grid=(batch, B_HEADS, nq),
        in_specs=[
            pl.BlockSpec((tq, B_DH), lambda b, h, i: (b * nq + i, h)),
            pl.BlockSpec((seq, B_DH), lambda b, h, i: (b, B_HEADS + h)),
            pl.BlockSpec((seq, B_DH), lambda b, h, i: (b, 2 * B_HEADS + h)),
            pl.BlockSpec((ctx_len, B_DH), lambda b, h, i: (b, B_HEADS + h)),
            pl.BlockSpec((ctx_len, B_DH), lambda b, h, i: (b, 2 * B_HEADS + h)),
            pl.BlockSpec((None, NA_ROWS, GRID_W, span), lambda b, h, i: (h, 0, 0, 0)),
            pl.BlockSpec((GRID_W, span), lambda b, h, i: (0, 0)),
        ],
        out_specs=pl.BlockSpec((tq, B_DH), lambda b, h, i: (b * nq + i, h)),
        out_shape=jax.ShapeDtypeStruct((rows, B_WIDTH), bf16),
        compiler_params=_params("arbitrary", "arbitrary", "arbitrary"),
        name="neighborhood_attention",
    )(qkv_lat, qkv_lat, qkv_lat, qkv_ctx, qkv_ctx, bias_win, col_mask)


def _na_tables(rpb):
    col = np.arange(GRID_W)
    col_off = np.clip(col[None, :] - col[:, None] + NA_COLS - 1, 0, 2 * NA_COLS - 2)
    toe = rpb.astype(f32)[:, :, col_off]
    win = jnp.stack([jnp.concatenate([toe[:, o + a] for a in range(NA_ROWS)], axis=-1)
                     for o in range(NA_ROWS)], axis=1)
    col_start = np.clip(col - NA_COLS // 2, 0, GRID_W - NA_COLS)
    cm = (col[None, :] >= col_start[:, None]) & (col[None, :] < col_start[:, None] + NA_COLS)
    mask = jnp.asarray(np.tile(cm.astype(np.float32), (1, NA_ROWS)))
    return win, mask


def _window_kernel(sink_ref, q_ref, k_ref, v_ref, kc_ref, vc_ref, o_ref, *, seq):
    n = pl.program_id(1)
    span = 3 * C_BLOCK
    start = pl.multiple_of(jnp.clip((n - 1) * C_BLOCK, 0, seq - span), C_BLOCK)
    qpos = n * C_BLOCK + lax.broadcasted_iota(jnp.int32, (C_BLOCK, span), 0)
    kpos = start + lax.broadcasted_iota(jnp.int32, (C_BLOCK, span), 1)
    valid = jnp.abs(qpos - kpos) <= C_WINDOW
    kw_all = k_ref[pl.ds(start, span), :]
    vw_all = v_ref[pl.ds(start, span), :]
    kc_all = kc_ref[...]
    vc_all = vc_ref[...]
    group = C_HEADS // C_KV_HEADS
    for kh in range(C_KV_HEADS):
        ks = slice(kh * C_DH, (kh + 1) * C_DH)
        kw, vw, kc, vc = kw_all[:, ks], vw_all[:, ks], kc_all[:, ks], vc_all[:, ks]
        for g in range(group):
            hq = kh * group + g
            q = q_ref[:, hq * C_DH:(hq + 1) * C_DH]
            s_w = lax.dot_general(q, kw, _NT, preferred_element_type=f32)
            s_w = jnp.where(valid, s_w, NEG_INF)
            s_c = lax.dot_general(q, kc, _NT, preferred_element_type=f32)
            o = _attend([(s_w, vw), (s_c, vc)], sink=sink_ref[hq])
            o_ref[:, hq * C_DH:(hq + 1) * C_DH] = o.astype(o_ref.dtype)


def _window_attention(qkv_lat, qkv_ctx, sink, seq, ctx_len):
    rows = qkv_lat.shape[0]
    batch = rows // seq
    nq = seq // C_BLOCK
    kcol = C_WIDTH // C_KV_WIDTH
    kern = functools.partial(_window_kernel, seq=seq)
    return pl.pallas_call(
        kern,
        grid=(batch, nq),
        in_specs=[
            pl.BlockSpec(memory_space=pltpu.SMEM),
            pl.BlockSpec((C_BLOCK, C_WIDTH), lambda b, i: (b * nq + i, 0)),
            pl.BlockSpec((seq, C_KV_WIDTH), lambda b, i: (b, kcol)),
            pl.BlockSpec((seq, C_KV_WIDTH), lambda b, i: (b, kcol + 1)),
            pl.BlockSpec((ctx_len, C_KV_WIDTH), lambda b, i: (b, kcol)),
            pl.BlockSpec((ctx_len, C_KV_WIDTH), lambda b, i: (b, kcol + 1)),
        ],
        out_specs=pl.BlockSpec((C_BLOCK, C_WIDTH), lambda b, i: (b * nq + i, 0)),
        out_shape=jax.ShapeDtypeStruct((rows, C_WIDTH), bf16),
        compiler_params=_params("arbitrary", "arbitrary"),
        name="window_attention",
    )(sink.astype(f32), qkv_lat, qkv_lat, qkv_lat, qkv_ctx, qkv_ctx)


def _ctx_attn_kernel(*refs, heads, kv_heads, dh, has_sink):
    if has_sink:
        sink_ref, qkv_ref, o_ref = refs
    else:
        qkv_ref, o_ref = refs
    group = heads // kv_heads
    k_off = heads * dh
    v_off = k_off + kv_heads * dh
    for kh in range(kv_heads):
        k = qkv_ref[:, k_off + kh * dh:k_off + (kh + 1) * dh]
        v = qkv_ref[:, v_off + kh * dh:v_off + (kh + 1) * dh]
        for g in range(group):
            hq = kh * group + g
            q = qkv_ref[:, hq * dh:(hq + 1) * dh]
            s = lax.dot_general(q, k, _NT, preferred_element_type=f32)
            o = _attend([(s, v)], sink=sink_ref[hq] if has_sink else None)
            o_ref[:, hq * dh:(hq + 1) * dh] = o.astype(o_ref.dtype)


def _context_attention(qkv_ctx, ctx_len, heads, kv_heads, dh, sink=None):
    rows, width = qkv_ctx.shape
    kern = functools.partial(_ctx_attn_kernel, heads=heads, kv_heads=kv_heads, dh=dh,
                             has_sink=sink is not None)
    args, specs = [qkv_ctx], [pl.BlockSpec((ctx_len, width), lambda b: (b, 0))]
    if sink is not None:
        args.insert(0, sink.astype(f32))
        specs.insert(0, pl.BlockSpec(memory_space=pltpu.SMEM))
    return pl.pallas_call(
        kern,
        grid=(rows // ctx_len,),
        in_specs=specs,
        out_specs=pl.BlockSpec((ctx_len, heads * dh), lambda b: (b, 0)),
        out_shape=jax.ShapeDtypeStruct((rows, heads * dh), bf16),
        compiler_params=_params("arbitrary"),
        name="context_attention",
    )(*args)


def _merge_kernel(oa_ref, ob_ref, oc_ref, wa_ref, wb_ref, wc_ref, ga_ref, gb_ref, gc_ref, o_ref):
    br_a = jnp.dot(oa_ref[...], wa_ref[...], preferred_element_type=f32)
    br_b = jnp.dot(ob_ref[...], wb_ref[...], preferred_element_type=f32)
    br_c = jnp.dot(oc_ref[...], wc_ref[...], preferred_element_type=f32)
    m = ga_ref[...] * br_a + gb_ref[...] * br_b + gc_ref[...] * br_c
    o_ref[...] = m.astype(o_ref.dtype)


def _merge(o_a, o_b, o_c, gates, w_a, w_b, w_c):
    rows = o_a.shape[0]
    d = w_a.shape[1]
    tm = _row_tile(rows, 1024)
    tn = 512
    nj = d // tn
    return pl.pallas_call(
        _merge_kernel,
        grid=(rows // tm, nj),
        in_specs=[
            pl.BlockSpec((tm, A_WIDTH), lambda i, j: (i, 0)),
            pl.BlockSpec((tm, B_WIDTH), lambda i, j: (i, 0)),
            pl.BlockSpec((tm, C_WIDTH), lambda i, j: (i, 0)),
            pl.BlockSpec((A_WIDTH, tn), lambda i, j: (0, j)),
            pl.BlockSpec((B_WIDTH, tn), lambda i, j: (0, j)),
            pl.BlockSpec((C_WIDTH, tn), lambda i, j: (0, j)),
            pl.BlockSpec((tm, tn), lambda i, j: (i, j)),
            pl.BlockSpec((tm, tn), lambda i, j: (i, nj + j)),
            pl.BlockSpec((tm, tn), lambda i, j: (i, 2 * nj + j)),
        ],
        out_specs=pl.BlockSpec((tm, tn), lambda i, j: (i, j)),
        out_shape=jax.ShapeDtypeStruct((rows, d), bf16),
        compiler_params=_params("arbitrary", "arbitrary"),
        name="merge_branches",
    )(o_a, o_b, o_c, w_a, w_b, w_c, gates, gates, gates)


def _residual_kernel(a_ref, w_ref, x_ref, gate_ref, o_ref):
    y = jnp.dot(a_ref[...], w_ref[...], preferred_element_type=f32)
    o_ref[...] = x_ref[...] + gate_ref[...] * y


def _gated_residual_matmul(a, w, x, mod, gate_idx, rows_per_group, tn, name):
    rows, kdim = a.shape
    d = w.shape[1]
    tm = _row_tile(rows_per_group, 1024)
    per = rows_per_group // tm
    nj = d // tn
    return pl.pallas_call(
        _residual_kernel,
        grid=(rows // tm, nj),
        in_specs=[
            pl.BlockSpec((tm, kdim), lambda i, j: (i, 0), pipeline_mode=pl.Buffered(1)),
            pl.BlockSpec((kdim, tn), lambda i, j: (0, j)),
            pl.BlockSpec((tm, tn), lambda i, j: (i, j)),
            pl.BlockSpec((None, 1, tn), lambda i, j: (i // per, 0, gate_idx * nj + j)),
        ],
        out_specs=pl.BlockSpec((tm, tn), lambda i, j: (i, j)),
        out_shape=jax.ShapeDtypeStruct((rows, d), f32),
        compiler_params=_params("arbitrary", "arbitrary"),
        name=name,
    )(a, w, x, mod)


def _swiglu_kernel(h_ref, wg_ref, wu_ref, o_ref):
    h = h_ref[...]
    g = jnp.dot(h, wg_ref[...], preferred_element_type=f32)
    u = jnp.dot(h, wu_ref[...], preferred_element_type=f32)
    o_ref[...] = (_silu(g) * u).astype(o_ref.dtype)


def _swiglu_up(h, w_gate, w_up):
    rows, d = h.shape
    n = w_gate.shape[1]
    tm = _row_tile(rows, 1024)
    tn = 256
    return pl.pallas_call(
        _swiglu_kernel,
        grid=(rows // tm, n // tn),
        in_specs=[
            pl.BlockSpec((tm, d), lambda i, j: (i, 0)),
            pl.BlockSpec((d, tn), lambda i, j: (0, j)),
            pl.BlockSpec((d, tn), lambda i, j: (0, j)),
        ],
        out_specs=pl.BlockSpec((tm, tn), lambda i, j: (i, j)),
        out_shape=jax.ShapeDtypeStruct((rows, n), bf16),
        compiler_params=_params("arbitrary", "arbitrary"),
        name="swiglu_up",
    )(h, w_gate, w_up)


def _rope_tables(seq):
    half = C_DH // 2
    pos = jnp.arange(seq)
    inv = ROPE_BASE ** (-jnp.arange(0, half, 2, dtype=f32) / half)
    ang_row = (pos // GRID_W).astype(f32)[:, None] * inv[None, :]
    ang_col = (pos % GRID_W).astype(f32)[:, None] * inv[None, :]

    def one(ang):
        return (jnp.concatenate([jnp.cos(ang), jnp.cos(ang)], axis=-1),
                jnp.concatenate([-jnp.sin(ang), jnp.sin(ang)], axis=-1))

    cr, sr = one(ang_row)
    cc, sc = one(ang_col)
    cos = jnp.concatenate([cr, cc], axis=-1)
    sin = jnp.concatenate([sr, sc], axis=-1)
    reps = LANES // C_DH
    return jnp.tile(cos, (1, reps)), jnp.tile(sin, (1, reps))


def kernel(x, c, ctx, c_ctx, norm1_g, norm2_g, w_mod, b_mod, w_in, hgrn_lb, a_norm_g, na_rpb,
           c_sink, w_branch, w_out, w_ffn_gate, w_ffn_up, w_ffn_down, final_norm_g):
    batch, seq, d = x.shape
    ctx_len = ctx.shape[1]
    depth = w_in.shape[0]
    n_lat, n_ctx = batch * seq, batch * ctx_len

    lb_w = jax.nn.softmax(hgrn_lb.astype(f32), axis=0)
    lower_bounds = jnp.cumsum(lb_w, axis=0) - lb_w[:1]

    mod_rows = 8 * (-(-(batch + 1) // 8))
    c_rows = jnp.zeros((mod_rows, d), f32).at[:batch].set(c).at[batch].set(c_ctx)
    mod_all = _modulation(c_rows, w_mod, b_mod)

    cos_t, sin_t = _rope_tables(seq)
    ones_t = jnp.ones((n_ctx, LANES), f32)
    zeros_t = jnp.zeros((n_ctx, LANES), f32)

    a_hi = 3 * A_QK + 2 * A_WIDTH
    b_hi = a_hi + 3 * B_WIDTH
    c_hi = b_hi + C_WIDTH + 2 * C_KV_WIDTH
    b_scale = jnp.concatenate([jnp.full((B_WIDTH,), B_DH ** -0.5, f32),
                               jnp.ones((2 * B_WIDTH,), f32)]).reshape(1, -1)
    c_scale = jnp.concatenate([jnp.full((C_WIDTH,), C_DH ** -0.5, f32),
                               jnp.ones((2 * C_KV_WIDTH,), f32)]).reshape(1, -1)
    rope_cols = C_WIDTH + C_KV_WIDTH

    x_lat = x.reshape(n_lat, d)
    x_ctx = ctx.reshape(n_ctx, d)

    for l in range(depth):
        need_ctx = l < depth - 1
        mod_l = mod_all[l, :batch].reshape(batch, 1, N_MOD * d)
        mod_c = mod_all[l, batch:batch + 1].reshape(1, 1, N_MOD * d)
        w_l = w_in[l]
        w_a = w_l[:, :a_hi].astype(bf16)
        w_b = w_l[:, a_hi:b_hi].astype(bf16)
        w_c = w_l[:, b_hi:c_hi].astype(bf16)
        w_g = w_l[:, c_hi:].astype(bf16)
        wbr = w_branch[l].astype(bf16)
        wo = w_out[l].astype(bf16)
        wfg = w_ffn_gate[l].astype(bf16)
        wfu = w_ffn_up[l].astype(bf16)
        wfd = w_ffn_down[l].astype(bf16)
        bias_win, col_mask = _na_tables(na_rpb[l])

        h_lat = _norm_modulate(x_lat, norm1_g[l], mod_l, 0, seq)
        h_ctx = _norm_modulate(x_ctx, norm1_g[l], mod_c, 0, n_ctx)

        pa_lat = _project(h_lat, w_a, f32, 1024, name="proj_hgrn")
        pa_ctx = _project(h_ctx, w_a, f32, 1024, name="proj_hgrn")
        pb_lat = _project(h_lat, w_b, bf16, 1024, col_scale=b_scale, name="proj_na")
        pb_ctx = _project(h_ctx, w_b, bf16, 1024, col_scale=b_scale, name="proj_na")
        pc_lat = _project(h_lat, w_c, bf16, c_hi - b_hi, col_scale=c_scale,
                          rope=(cos_t, sin_t, rope_cols), name="proj_window")
        pc_ctx = _project(h_ctx, w_c, bf16, c_hi - b_hi, col_scale=c_scale,
                          rope=(ones_t, zeros_t, rope_cols), name="proj_window")
        g_lat = _project(h_lat, w_g, bf16, 1024, sigmoid=True, name="proj_gates")

        a_lat, a_ctx = _hgrn_mixer(pa_lat, pa_ctx, lower_bounds[l], a_norm_g[l], seq, ctx_len, need_ctx)
        b_lat = _neighborhood_attention(pb_lat, pb_ctx, bias_win, col_mask, seq, ctx_len)
        c_lat = _window_attention(pc_lat, pc_ctx, c_sink[l], seq, ctx_len)

        wbr_a, wbr_b, wbr_c = wbr[:A_WIDTH], wbr[A_WIDTH:A_WIDTH + B_WIDTH], wbr[A_WIDTH + B_WIDTH:]
        m_lat = _merge(a_lat, b_lat, c_lat, g_lat, wbr_a, wbr_b, wbr_c)
        x_lat = _gated_residual_matmul(m_lat, wo, x_lat, mod_l, 2, seq, 512, "out_proj")
        h2 = _norm_modulate(x_lat, norm2_g[l], mod_l, 3, seq)
        u = _swiglu_up(h2, wfg, wfu)
        x_lat = _gated_residual_matmul(u, wfd, x_lat, mod_l, 5, seq, 256, "ffn_down")

        if need_ctx:
            g_ctx = _project(h_ctx, w_g, bf16, 1024, sigmoid=True, name="proj_gates")
            b_ctx = _context_attention(pb_ctx, ctx_len, B_HEADS, B_HEADS, B_DH)
            c_ctx_o = _context_attention(pc_ctx, ctx_len, C_HEADS, C_KV_HEADS, C_DH, sink=c_sink[l])
            m_ctx = _merge(a_ctx, b_ctx, c_ctx_o, g_ctx, wbr_a, wbr_b, wbr_c)
            x_ctx = _gated_residual_matmul(m_ctx, wo, x_ctx, mod_c, 2, n_ctx, 512, "out_proj")
            h2c = _norm_modulate(x_ctx, norm2_g[l], mod_c, 3, n_ctx)
            uc = _swiglu_up(h2c, wfg, wfu)
            x_ctx = _gated_residual_matmul(uc, wfd, x_ctx, mod_c, 5, n_ctx, 256, "ffn_down")

    return _final_norm(x_lat, final_norm_g).reshape(batch, seq, d)
```

```python
import functools

import jax
import jax.numpy as jnp
import numpy as np
from jax import lax
from jax.experimental import pallas as pl
from jax.experimental.pallas import tpu as pltpu

GRID_W = 64
EPS = 1e-6
NEG_INF = -1e30
N_MOD = 6
A_HEADS, A_DK, A_DV, A_CHUNK = 16, 128, 128, 32
A_QK = A_HEADS * A_DK
A_WIDTH = A_HEADS * A_DV
B_HEADS, B_DH = 8, 128
B_WIDTH = B_HEADS * B_DH
NA_ROWS, NA_COLS = 8, 16
C_HEADS, C_KV_HEADS, C_DH = 16, 2, 64
C_WIDTH = C_HEADS * C_DH
C_KV_WIDTH = C_KV_HEADS * C_DH
C_WINDOW = 128
C_BLOCK = 128
ROPE_BASE = 10000.0
N_BRANCH = 3

LANES = 128
VMEM_LIMIT_BYTES = 56 * 1024 * 1024

HGRN_BLOCK = 256
HGRN_HEADS_PER_STEP = 4
NA_QROWS = 4
NA_UNION = NA_ROWS + NA_QROWS
NA_HEADS_PER_STEP = 2

C_PAIRS = C_WIDTH // LANES
C_EXP = 2 * C_KV_HEADS * LANES
C_OUT_WIDTH = C_WIDTH + 2 * C_EXP

_NT = (((1,), (1,)), ((), ()))
_TN = (((0,), (0,)), ((), ()))

bf16 = jnp.bfloat16
f32 = jnp.float32


def _params(*sem):
    return pltpu.CompilerParams(dimension_semantics=sem, vmem_limit_bytes=VMEM_LIMIT_BYTES)


def _row_tile(rows, want):
    t = min(rows, want)
    while rows % t:
        t //= 2
    return t


def _sigmoid(x):
    return 1.0 / (1.0 + jnp.exp(-x))


def _silu(x):
    return x * _sigmoid(x)


def _mod_kernel(c_ref, w_ref, b_ref, o_ref):
    a = _silu(c_ref[...]).astype(bf16)
    o_ref[...] = jnp.dot(a, w_ref[...].astype(bf16), preferred_element_type=f32) + b_ref[...]


def _modulation(c_rows, w_mod, b_mod):
    depth, d, n = w_mod.shape
    rows = c_rows.shape[0]
    tn = 512
    return pl.pallas_call(
        _mod_kernel,
        grid=(depth, n // tn),
        in_specs=[
            pl.BlockSpec((rows, d), lambda l, j: (0, 0)),
            pl.BlockSpec((None, d, tn), lambda l, j: (l, 0, j)),
            pl.BlockSpec((None, 1, tn), lambda l, j: (l, 0, j)),
        ],
        out_specs=pl.BlockSpec((None, rows, tn), lambda l, j: (l, 0, j)),
        out_shape=jax.ShapeDtypeStruct((depth, rows, n), f32),
        compiler_params=_params("arbitrary", "arbitrary"),
        name="modulation",
    )(c_rows, w_mod, b_mod.reshape(depth, 1, n))


def _norm_mod_kernel(x_ref, g_ref, shift_ref, scale_ref, o_ref):
    x = x_ref[...]
    y = x * lax.rsqrt(jnp.mean(x * x, axis=-1, keepdims=True) + EPS) * g_ref[...]
    o_ref[...] = (y * (1.0 + scale_ref[...]) + shift_ref[...]).astype(o_ref.dtype)


def _norm_kernel(x_ref, g_ref, o_ref):
    x = x_ref[...]
    y = x * lax.rsqrt(jnp.mean(x * x, axis=-1, keepdims=True) + EPS) * g_ref[...]
    o_ref[...] = y.astype(o_ref.dtype)


def _norm_modulate(x, g, mod, shift_idx, rows_per_group):
    rows, d = x.shape
    tm = _row_tile(rows_per_group, 256)
    per = rows_per_group // tm
    return pl.pallas_call(
        _norm_mod_kernel,
        grid=(rows // tm,),
        in_specs=[
            pl.BlockSpec((tm, d), lambda i: (i, 0)),
            pl.BlockSpec((1, d), lambda i: (0, 0)),
            pl.BlockSpec((None, 1, d), lambda i: (i // per, 0, shift_idx)),
            pl.BlockSpec((None, 1, d), lambda i: (i // per, 0, shift_idx + 1)),
        ],
        out_specs=pl.BlockSpec((tm, d), lambda i: (i, 0)),
        out_shape=jax.ShapeDtypeStruct((rows, d), bf16),
        compiler_params=_params("arbitrary"),
        name="norm_modulate",
    )(x, g.reshape(1, d), mod, mod)


def _final_norm(x, g):
    rows, d = x.shape
    tm = _row_tile(rows, 256)
    return pl.pallas_call(
        _norm_kernel,
        grid=(rows // tm,),
        in_specs=[pl.BlockSpec((tm, d), lambda i: (i, 0)), pl.BlockSpec((1, d), lambda i: (0, 0))],
        out_specs=pl.BlockSpec((tm, d), lambda i: (i, 0)),
        out_shape=jax.ShapeDtypeStruct((rows, d), f32),
        compiler_params=_params("arbitrary"),
        name="final_norm",
    )(x, g.reshape(1, d))


def _proj_kernel(h_ref, w_ref, o_ref):
    o_ref[...] = jnp.dot(h_ref[...], w_ref[...], preferred_element_type=f32).astype(o_ref.dtype)


def _proj_scale_kernel(h_ref, w_ref, s_ref, o_ref):
    acc = jnp.dot(h_ref[...], w_ref[...], preferred_element_type=f32)
    o_ref[...] = (acc * s_ref[...]).astype(o_ref.dtype)


def _proj_sigmoid_kernel(h_ref, w_ref, o_ref):
    acc = jnp.dot(h_ref[...], w_ref[...], preferred_element_type=f32)
    o_ref[...] = _sigmoid(acc).astype(o_ref.dtype)


def _project(h, w, out_dtype, tn, *, col_scale=None, sigmoid=False, name="proj"):
    rows, d = h.shape
    n = w.shape[1]
    tm = _row_tile(rows, 1024)
    h_spec = pl.BlockSpec((tm, d), lambda i, j: (i, 0), pipeline_mode=pl.Buffered(1))
    w_spec = pl.BlockSpec((d, tn), lambda i, j: (0, j))
    args, specs = [h, w], [h_spec, w_spec]
    if sigmoid:
        kern = _proj_sigmoid_kernel
    elif col_scale is not None:
        kern = _proj_scale_kernel
        args.append(col_scale)
        specs.append(pl.BlockSpec((1, tn), lambda i, j: (0, j)))
    else:
        kern = _proj_kernel
    return pl.pallas_call(
        kern,
        grid=(rows // tm, n // tn),
        in_specs=specs,
        out_specs=pl.BlockSpec((tm, tn), lambda i, j: (i, j)),
        out_shape=jax.ShapeDtypeStruct((rows, n), out_dtype),
        compiler_params=_params("arbitrary", "arbitrary"),
        name=name,
    )(*args)


def _rope_rotate(x, cos, sin):
    n = x.shape[-1]
    lane = lax.broadcasted_iota(jnp.int32, x.shape, x.ndim - 1)
    up = pltpu.roll(x, n - 16, x.ndim - 1)
    down = pltpu.roll(x, 16, x.ndim - 1)
    return x * cos + jnp.where((lane % 32) < 16, up, down) * sin


def _proj_window_kernel(h_ref, w_ref, cos_ref, sin_ref, o_ref):
    acc = jnp.dot(h_ref[...], w_ref[...], preferred_element_type=f32)
    cos = cos_ref[...]
    sin = sin_ref[...]
    for j in range(C_PAIRS):
        x = _rope_rotate(acc[:, j * LANES:(j + 1) * LANES], cos, sin) * (C_DH ** -0.5)
        o_ref[:, j * LANES:(j + 1) * LANES] = x.astype(o_ref.dtype)
    k = _rope_rotate(acc[:, C_WIDTH:C_WIDTH + LANES], cos, sin)
    v = acc[:, C_WIDTH + LANES:C_WIDTH + 2 * LANES]
    low = lax.broadcasted_iota(jnp.int32, k.shape, 1) < C_DH
    for idx, x in enumerate((k, v)):
        swapped = pltpu.roll(x, C_DH, 1)
        groups = (jnp.where(low, x, 0.0), jnp.where(low, 0.0, swapped),
                  jnp.where(low, swapped, 0.0), jnp.where(low, 0.0, x))
        base = C_WIDTH + idx * C_EXP
        for c, val in enumerate(groups):
            o_ref[:, base + c * LANES:base + (c + 1) * LANES] = val.astype(o_ref.dtype)


def _project_window(h, w, cos, sin):
    rows, d = h.shape
    n = w.shape[1]
    tm = _row_tile(rows, 1024)
    tok_tiles = cos.shape[0] // tm
    t_spec = pl.BlockSpec((tm, LANES), lambda i: (i % tok_tiles, 0))
    return pl.pallas_call(
        _proj_window_kernel,
        grid=(rows // tm,),
        in_specs=[pl.BlockSpec((tm, d), lambda i: (i, 0)),
                  pl.BlockSpec((d, n), lambda i: (0, 0), pipeline_mode=pl.Buffered(1)),
                  t_spec, t_spec],
        out_specs=pl.BlockSpec((tm, C_OUT_WIDTH), lambda i: (i, 0)),
        out_shape=jax.ShapeDtypeStruct((rows, C_OUT_WIDTH), bf16),
        compiler_params=_params("arbitrary"),
        name="proj_window",
    )(h, w, cos, sin)


def _hgrn_kernel(*refs, rev, has_s0, emit_state, readout, hp):
    it = iter(refs)
    q_ref, v_ref, f_ref, lb_ref = next(it), next(it), next(it), next(it)
    s0_ref = next(it) if has_s0 else None
    if readout:
        g_ref, oprev_ref, ng_ref = next(it), next(it), next(it)
    o_ref = next(it)
    sout_ref = next(it) if emit_state else None
    st_ref = next(it)

    blk = pl.program_id(2)
    nblk = pl.num_programs(2)
    tb = q_ref.shape[0]
    nchunk = tb // A_CHUNK
    width = hp * A_DK
    heads = range(hp)

    def head(a, h):
        return a[:, h * A_DK:(h + 1) * A_DK]

    @pl.when(blk == 0)
    def _():
        if has_s0:
            st_ref[...] = s0_ref[...]
        else:
            st_ref[...] = jnp.zeros_like(st_ref)

    lb = lb_ref[...]
    f = lb + (1.0 - lb) * _sigmoid(f_ref[...])
    logf = jnp.log(f)
    k = 1.0 - f

    row = lax.broadcasted_iota(jnp.int32, (tb, tb), 0)
    col = lax.broadcasted_iota(jnp.int32, (tb, tb), 1)
    same = (row // A_CHUNK) == (col // A_CHUNK)
    causal = (col >= row) if rev else (col <= row)
    mask = same & causal
    tri = jnp.where(mask, 1.0, 0.0).astype(bf16)

    hi = logf.astype(bf16)
    lo = (logf - hi.astype(f32)).astype(bf16)
    cum2 = jnp.dot(tri, jnp.concatenate([hi, lo], axis=1), preferred_element_type=f32)
    cum = cum2[:, :width] + cum2[:, width:]

    q_dec = (_silu(q_ref[...]) * jnp.exp(cum)).astype(bf16)
    k_inv = (k * jnp.exp(-cum)).astype(bf16)
    v = v_ref[...].astype(bf16)
    att = [lax.dot_general(head(q_dec, h), head(k_inv, h), _NT, preferred_element_type=f32)
           for h in heads]
    att = [jnp.where(mask, a, 0.0).astype(bf16) for a in att]
    o_intra = [jnp.dot(att[h], head(v, h), preferred_element_type=f32) for h in heads]

    order = range(nchunk - 1, -1, -1) if rev else range(nchunk)
    k_end, dec = {}, {}
    for j in order:
        sl = slice(j * A_CHUNK, (j + 1) * A_CHUNK)
        last = j * A_CHUNK if rev else (j + 1) * A_CHUNK - 1
        tot = cum[last:last + 1, :]
        k_end[j] = (k[sl] * jnp.exp(tot - cum[sl])).astype(bf16)
        dec[j] = jnp.exp(tot)
    upd = {(h, j): lax.dot_general(head(v[j * A_CHUNK:(j + 1) * A_CHUNK], h), head(k_end[j], h), _TN,
                                   preferred_element_type=f32)
           for j in order for h in heads}

    before = {}
    final = []
    for h in heads:
        s = st_ref[h]
        for j in order:
            before[h, j] = s.astype(bf16)
            s = s * head(dec[j], h) + upd[h, j]
        st_ref[h] = s
        final.append(s)

    outs = []
    for h in heads:
        o_inter = [lax.dot_general(head(q_dec[j * A_CHUNK:(j + 1) * A_CHUNK], h), before[h, j], _NT,
                                   preferred_element_type=f32) for j in range(nchunk)]
        outs.append(o_intra[h] + jnp.concatenate(o_inter, axis=0))

    if readout:
        ng = ng_ref[...]
        normed = []
        for h in heads:
            o = outs[h] + head(oprev_ref[...], h)
            normed.append(o * lax.rsqrt(jnp.mean(o * o, axis=-1, keepdims=True) + EPS) * ng)
        o_ref[...] = (jnp.concatenate(normed, axis=1) * _silu(g_ref[...])).astype(o_ref.dtype)
    else:
        o_ref[...] = jnp.concatenate(outs, axis=1)

    if emit_state:
        @pl.when(blk == nblk - 1)
        def _():
            for h in heads:
                sout_ref[h] = final[h]


def _hgrn_scan(proj, lb_dir, seq, direction, *, s0=None, emit_state=False, readout=None):
    rows = proj.shape[0]
    batch = rows // seq
    tb = min(HGRN_BLOCK, seq)
    nblk = seq // tb
    rev = direction == 1
    hp = HGRN_HEADS_PER_STEP
    hblocks = A_HEADS // hp
    width = hp * A_DK

    def tok(b, h, i):
        return b * nblk + (nblk - 1 - i if rev else i)

    def col_spec(group):
        return pl.BlockSpec((tb, width), lambda b, h, i: (tok(b, h, i), group * hblocks + h))

    state_spec = pl.BlockSpec((None, hp, A_DV, A_DK), lambda b, h, i: (b, h, 0, 0))
    args = [proj, proj, proj, lb_dir.reshape(1, A_QK)]
    specs = [col_spec(0), col_spec(1), col_spec(2 + direction),
             pl.BlockSpec((1, width), lambda b, h, i: (0, h))]
    if s0 is not None:
        args.append(s0)
        specs.append(state_spec)
    if readout is not None:
        o_prev, norm_g = readout
        args += [proj, o_prev, norm_g.reshape(1, A_DV)]
        specs += [col_spec(4),
                  pl.BlockSpec((tb, width), lambda b, h, i: (tok(b, h, i), h)),
                  pl.BlockSpec((1, A_DV), lambda b, h, i: (0, 0))]
    out_shape = [jax.ShapeDtypeStruct((rows, A_WIDTH), bf16 if readout is not None else f32)]
    out_specs = [pl.BlockSpec((tb, width), lambda b, h, i: (tok(b, h, i), h))]
    if emit_state:
        out_shape.append(jax.ShapeDtypeStruct((batch, A_HEADS, A_DV, A_DK), f32))
        out_specs.append(state_spec)
    kern = functools.partial(_hgrn_kernel, rev=rev, has_s0=s0 is not None,
                             emit_state=emit_state, readout=readout is not None, hp=hp)
    res = pl.pallas_call(
        kern,
        grid=(batch, hblocks, nblk),
        in_specs=specs,
        out_specs=out_specs,
        out_shape=out_shape,
        scratch_shapes=[pltpu.VMEM((hp, A_DV, A_DK), f32)],
        compiler_params=_params("arbitrary", "arbitrary", "arbitrary"),
        name="hgrn_scan",
    )(*args)
    return res if emit_state else res[0]


def _hgrn_mixer(proj_lat, proj_ctx, lb, norm_g, seq, ctx_len, need_ctx):
    o_c_f, s_f = _hgrn_scan(proj_ctx, lb[0], ctx_len, 0, emit_state=True)
    o_l_f = _hgrn_scan(proj_lat, lb[0], seq, 0, s0=s_f)
    if need_ctx:
        a_ctx, s_b = _hgrn_scan(proj_ctx, lb[1], ctx_len, 1, emit_state=True, readout=(o_c_f, norm_g))
    else:
        _, s_b = _hgrn_scan(proj_ctx, lb[1], ctx_len, 1, emit_state=True)
        a_ctx = None
    a_lat = _hgrn_scan(proj_lat, lb[1], seq, 1, s0=s_b, readout=(o_l_f, norm_g))
    return a_lat, a_ctx


def _attend(parts, sink=None):
    m = parts[0][0].max(axis=-1, keepdims=True)
    for s, _ in parts[1:]:
        m = jnp.maximum(m, s.max(axis=-1, keepdims=True))
    if sink is not None:
        m = jnp.maximum(m, sink)
    den = None
    acc = None
    for s, v in parts:
        e = jnp.exp(s - m)
        d = e.sum(axis=-1, keepdims=True)
        o = jnp.dot(e.astype(bf16), v, preferred_element_type=f32)
        den = d if den is None else den + d
        acc = o if acc is None else acc + o
    if sink is not None:
        den = den + jnp.exp(sink - m)
    return acc / den


def _na_kernel(types_ref, q_ref, k_ref, v_ref, kc_ref, vc_ref, tbl_ref, o_ref, *, grid_rows, hp):
    del types_ref
    step = pl.program_id(2)
    span = NA_UNION * GRID_W
    first = jnp.clip(step * NA_QROWS - NA_ROWS // 2, 0, grid_rows - NA_UNION)
    start = pl.multiple_of(first * GRID_W, GRID_W)
    for h in range(hp):
        hs = slice(h * B_DH, (h + 1) * B_DH)
        q = q_ref[:, hs]
        kn = k_ref[pl.ds(start, span), hs]
        vn = v_ref[pl.ds(start, span), hs]
        s_nb = lax.dot_general(q, kn, _NT, preferred_element_type=f32) + tbl_ref[h]
        s_cx = lax.dot_general(q, kc_ref[:, hs], _NT, preferred_element_type=f32)
        o = _attend([(s_nb, vn), (s_cx, vc_ref[:, hs])])
        o_ref[:, hs] = o.astype(o_ref.dtype)


def _na_tables(rpb, grid_rows):
    assert grid_rows >= NA_UNION and grid_rows % NA_QROWS == 0
    col = np.arange(GRID_W)
    col_off = np.clip(col[None, :] - col[:, None] + NA_COLS - 1, 0, 2 * NA_COLS - 2)
    col_start = np.clip(col - NA_COLS // 2, 0, GRID_W - NA_COLS)
    col_ok = (col[None, :] >= col_start[:, None]) & (col[None, :] < col_start[:, None] + NA_COLS)
    seen, types = {}, []
    for i in range(grid_rows // NA_QROWS):
        first = int(np.clip(i * NA_QROWS - NA_ROWS // 2, 0, grid_rows - NA_UNION))
        key_row = first + np.arange(NA_UNION)[None, :]
        r = i * NA_QROWS + np.arange(NA_QROWS)[:, None]
        row_start = np.clip(r - NA_ROWS // 2, 0, grid_rows - NA_ROWS)
        ok = (key_row >= row_start) & (key_row < row_start + NA_ROWS)
        assert (ok.sum(axis=1) == NA_ROWS).all()
        off = np.where(ok, key_row - r + NA_ROWS - 1, 0)
        sig = (ok.tobytes(), off.tobytes())
        if sig not in seen:
            seen[sig] = (len(seen), ok, off)
        types.append(seen[sig][0])
    tables = []
    for _, ok, off in sorted(seen.values(), key=lambda t: t[0]):
        b = rpb.astype(f32)[:, off][..., col_off]
        valid = ok[None, :, :, None, None] & col_ok[None, None, None]
        b = jnp.where(valid, b, NEG_INF).transpose(0, 1, 3, 2, 4)
        tables.append(b.reshape(rpb.shape[0], NA_QROWS * GRID_W, NA_UNION * GRID_W))
    return jnp.stack(tables, axis=1), jnp.asarray(np.array(types, np.int32))


def _neighborhood_attention(qkv_lat, qkv_ctx, tables, types, seq, ctx_len):
    rows = qkv_lat.shape[0]
    batch = rows // seq
    tq = NA_QROWS * GRID_W
    nq = seq // tq
    hp = NA_HEADS_PER_STEP
    hb = B_HEADS // hp
    width = hp * B_DH
    kern = functools.partial(_na_kernel, grid_rows=seq // GRID_W, hp=hp)
    grid_spec = pltpu.PrefetchScalarGridSpec(
        num_scalar_prefetch=1,
        grid=(batch, hb, nq),
        in_specs=[
            pl.BlockSpec((tq, width), lambda b, h, i, t: (b * nq + i, h)),
            pl.BlockSpec((seq, width), lambda b, h, i, t: (b, hb + h)),
            pl.BlockSpec((seq, width), lambda b, h, i, t: (b, 2 * hb + h)),
            pl.BlockSpec((ctx_len, width), lambda b, h, i, t: (b, hb + h)),
            pl.BlockSpec((ctx_len, width), lambda b, h, i, t: (b, 2 * hb + h)),
            pl.BlockSpec((hp, None, tq, NA_UNION * GRID_W), lambda b, h, i, t: (h, t[i], 0, 0)),
        ],
        out_specs=pl.BlockSpec((tq, width), lambda b, h, i, t: (b * nq + i, h)),
    )
    return pl.pallas_call(
        kern,
        grid_spec=grid_spec,
        out_shape=jax.ShapeDtypeStruct((rows, B_WIDTH), bf16),
        compiler_params=_params("arbitrary", "arbitrary", "arbitrary"),
        name="neighborhood_attention",
    )(types, qkv_lat, qkv_lat, qkv_lat, qkv_ctx, qkv_ctx, tables)


def _window_kernel(sink_ref, q_ref, k_ref, v_ref, kc_ref, vc_ref, o_ref, *, seq):
    n = pl.program_id(1)
    span = 3 * C_BLOCK
    start = pl.multiple_of(jnp.clip((n - 1) * C_BLOCK, 0, seq - span), C_BLOCK)
    pairs = C_PAIRS // C_KV_HEADS
    rows = pairs * C_BLOCK
    qpos = n * C_BLOCK + lax.broadcasted_iota(jnp.int32, (rows, span), 0) % C_BLOCK
    kpos = start + lax.broadcasted_iota(jnp.int32, (rows, span), 1)
    valid = jnp.abs(qpos - kpos) <= C_WINDOW
    pair_id = lax.broadcasted_iota(jnp.int32, (rows, 1), 0) // C_BLOCK

    chains = [(g, e) for g in range(C_KV_HEADS) for e in range(2)]
    q = {g: jnp.concatenate([q_ref[:, (g * pairs + p) * LANES:(g * pairs + p + 1) * LANES]
                             for p in range(pairs)], axis=0) for g in range(C_KV_HEADS)}
    s_w, s_c, sink = {}, {}, {}
    for g, e in chains:
        cs = slice((2 * g + e) * LANES, (2 * g + e + 1) * LANES)
        s = lax.dot_general(q[g], k_ref[pl.ds(start, span), cs], _NT, preferred_element_type=f32)
        s_w[g, e] = jnp.where(valid, s, NEG_INF)
        s_c[g, e] = lax.dot_general(q[g], kc_ref[:, cs], _NT, preferred_element_type=f32)
        col = jnp.zeros((rows, 1), f32)
        for p in range(pairs):
            col = jnp.where(pair_id == p, sink_ref[(g * pairs + p) * 2 + e], col)
        sink[g, e] = col
    out = {}
    for g, e in chains:
        cs = slice((2 * g + e) * LANES, (2 * g + e + 1) * LANES)
        out[g, e] = _attend([(s_w[g, e], v_ref[pl.ds(start, span), cs]), (s_c[g, e], vc_ref[:, cs])],
                            sink=sink[g, e])
    for g in range(C_KV_HEADS):
        o = out[g, 0] + out[g, 1]
        for p in range(pairs):
            o_ref[:, (g * pairs + p) * LANES:(g * pairs + p + 1) * LANES] = (
                o[p * C_BLOCK:(p + 1) * C_BLOCK].astype(o_ref.dtype))


def _window_attention(qkv_lat, qkv_ctx, sink, seq, ctx_len):
    rows = qkv_lat.shape[0]
    batch = rows // seq
    nq = seq // C_BLOCK
    kblk = C_WIDTH // C_EXP
    kern = functools.partial(_window_kernel, seq=seq)
    resident = dict(pipeline_mode=pl.Buffered(1))
    return pl.pallas_call(
        kern,
        grid=(batch, nq),
        in_specs=[
            pl.BlockSpec(memory_space=pltpu.SMEM),
            pl.BlockSpec((C_BLOCK, C_WIDTH), lambda b, i: (b * nq + i, 0)),
            pl.BlockSpec((seq, C_EXP), lambda b, i: (b, kblk), **resident),
            pl.BlockSpec((seq, C_EXP), lambda b, i: (b, kblk + 1), **resident),
            pl.BlockSpec((ctx_len, C_EXP), lambda b, i: (b, kblk)),
            pl.BlockSpec((ctx_len, C_EXP), lambda b, i: (b, kblk + 1)),
        ],
        out_specs=pl.BlockSpec((C_BLOCK, C_WIDTH), lambda b, i: (b * nq + i, 0)),
        out_shape=jax.ShapeDtypeStruct((rows, C_WIDTH), bf16),
        compiler_params=_params("arbitrary", "arbitrary"),
        name="window_attention",
    )(sink.astype(f32), qkv_lat, qkv_lat, qkv_lat, qkv_ctx, qkv_ctx)


def _ctx_attn_kernel(*refs, heads, dh, k_cols, v_cols, has_sink):
    if has_sink:
        sink_ref, qkv_ref, o_ref = refs
    else:
        qkv_ref, o_ref = refs
    group = heads // len(k_cols)
    for kh, (kc, vc) in enumerate(zip(k_cols, v_cols)):
        k = qkv_ref[:, kc:kc + dh]
        v = qkv_ref[:, vc:vc + dh]
        for g in range(group):
            hq = kh * group + g
            q = qkv_ref[:, hq * dh:(hq + 1) * dh]
            s = lax.dot_general(q, k, _NT, preferred_element_type=f32)
            o = _attend([(s, v)], sink=sink_ref[hq] if has_sink else None)
            o_ref[:, hq * dh:(hq + 1) * dh] = o.astype(o_ref.dtype)


def _context_attention(qkv_ctx, ctx_len, heads, dh, k_cols, v_cols, sink=None):
    rows, width = qkv_ctx.shape
    kern = functools.partial(_ctx_attn_kernel, heads=heads, dh=dh, k_cols=k_cols, v_cols=v_cols,
                             has_sink=sink is not None)
    args, specs = [qkv_ctx], [pl.BlockSpec((ctx_len, width), lambda b: (b, 0))]
    if sink is not None:
        args.insert(0, sink.astype(f32))
        specs.insert(0, pl.BlockSpec(memory_space=pltpu.SMEM))
    return pl.pallas_call(
        kern,
        grid=(rows // ctx_len,),
        in_specs=specs,
        out_specs=pl.BlockSpec((ctx_len, heads * dh), lambda b: (b, 0)),
        out_shape=jax.ShapeDtypeStruct((rows, heads * dh), bf16),
        compiler_params=_params("arbitrary"),
        name="context_attention",
    )(*args)


def _merge_kernel(oa_ref, ob_ref, oc_ref, wa_ref, wb_ref, wc_ref, ga_ref, gb_ref, gc_ref, o_ref):
    br_a = jnp.dot(oa_ref[...], wa_ref[...], preferred_element_type=f32)
    br_b = jnp.dot(ob_ref[...], wb_ref[...], preferred_element_type=f32)
    br_c = jnp.dot(oc_ref[...], wc_ref[...], preferred_element_type=f32)
    m = ga_ref[...] * br_a + gb_ref[...] * br_b + gc_ref[...] * br_c
    o_ref[...] = m.astype(o_ref.dtype)


def _merge(o_a, o_b, o_c, gates, w_a, w_b, w_c):
    rows = o_a.shape[0]
    d = w_a.shape[1]
    tm = _row_tile(rows, 1024)
    tn = 512
    nj = d // tn
    return pl.pallas_call(
        _merge_kernel,
        grid=(rows // tm, nj),
        in_specs=[
            pl.BlockSpec((tm, A_WIDTH), lambda i, j: (i, 0)),
            pl.BlockSpec((tm, B_WIDTH), lambda i, j: (i, 0)),
            pl.BlockSpec((tm, C_WIDTH), lambda i, j: (i, 0)),
            pl.BlockSpec((A_WIDTH, tn), lambda i, j: (0, j)),
            pl.BlockSpec((B_WIDTH, tn), lambda i, j: (0, j)),
            pl.BlockSpec((C_WIDTH, tn), lambda i, j: (0, j)),
            pl.BlockSpec((tm, tn), lambda i, j: (i, j)),
            pl.BlockSpec((tm, tn), lambda i, j: (i, nj + j)),
            pl.BlockSpec((tm, tn), lambda i, j: (i, 2 * nj + j)),
        ],
        out_specs=pl.BlockSpec((tm, tn), lambda i, j: (i, j)),
        out_shape=jax.ShapeDtypeStruct((rows, d), bf16),
        compiler_params=_params("arbitrary", "arbitrary"),
        name="merge_branches",
    )(o_a, o_b, o_c, w_a, w_b, w_c, gates, gates, gates)


def _residual_kernel(a_ref, w_ref, x_ref, gate_ref, o_ref):
    y = jnp.dot(a_ref[...], w_ref[...], preferred_element_type=f32)
    o_ref[...] = x_ref[...] + gate_ref[...] * y


def _gated_residual_matmul(a, w, x, mod, gate_idx, rows_per_group, tn, name):
    rows, kdim = a.shape
    d = w.shape[1]
    tm = _row_tile(rows_per_group, 1024)
    per = rows_per_group // tm
    nj = d // tn
    return pl.pallas_call(
        _residual_kernel,
        grid=(rows // tm, nj),
        in_specs=[
            pl.BlockSpec((tm, kdim), lambda i, j: (i, 0), pipeline_mode=pl.Buffered(1)),
            pl.BlockSpec((kdim, tn), lambda i, j: (0, j)),
            pl.BlockSpec((tm, tn), lambda i, j: (i, j)),
            pl.BlockSpec((None, 1, tn), lambda i, j: (i // per, 0, gate_idx * nj + j)),
        ],
        out_specs=pl.BlockSpec((tm, tn), lambda i, j: (i, j)),
        out_shape=jax.ShapeDtypeStruct((rows, d), f32),
        compiler_params=_params("arbitrary", "arbitrary"),
        name=name,
    )(a, w, x, mod)


def _swiglu_kernel(h_ref, wg_ref, wu_ref, o_ref):
    h = h_ref[...]
    g = jnp.dot(h, wg_ref[...], preferred_element_type=f32)
    u = jnp.dot(h, wu_ref[...], preferred_element_type=f32)
    o_ref[...] = (_silu(g) * u).astype(o_ref.dtype)


def _swiglu_up(h, w_gate, w_up):
    rows, d = h.shape
    n = w_gate.shape[1]
    tm = _row_tile(rows, 1024)
    tn = 256
    return pl.pallas_call(
        _swiglu_kernel,
        grid=(rows // tm, n // tn),
        in_specs=[
            pl.BlockSpec((tm, d), lambda i, j: (i, 0)),
            pl.BlockSpec((d, tn), lambda i, j: (0, j)),
            pl.BlockSpec((d, tn), lambda i, j: (0, j)),
        ],
        out_specs=pl.BlockSpec((tm, tn), lambda i, j: (i, j)),
        out_shape=jax.ShapeDtypeStruct((rows, n), bf16),
        compiler_params=_params("arbitrary", "arbitrary"),
        name="swiglu_up",
    )(h, w_gate, w_up)


def _rope_tables(seq):
    half = C_DH // 2
    pos = jnp.arange(seq)
    inv = ROPE_BASE ** (-jnp.arange(0, half, 2, dtype=f32) / half)
    ang_row = (pos // GRID_W).astype(f32)[:, None] * inv[None, :]
    ang_col = (pos % GRID_W).astype(f32)[:, None] * inv[None, :]

    def one(ang):
        return (jnp.concatenate([jnp.cos(ang), jnp.cos(ang)], axis=-1),
                jnp.concatenate([-jnp.sin(ang), jnp.sin(ang)], axis=-1))

    cr, sr = one(ang_row)
    cc, sc = one(ang_col)
    cos = jnp.concatenate([cr, cc], axis=-1)
    sin = jnp.concatenate([sr, sc], axis=-1)
    reps = LANES // C_DH
    return jnp.tile(cos, (1, reps)), jnp.tile(sin, (1, reps))


def kernel(x, c, ctx, c_ctx, norm1_g, norm2_g, w_mod, b_mod, w_in, hgrn_lb, a_norm_g, na_rpb,
           c_sink, w_branch, w_out, w_ffn_gate, w_ffn_up, w_ffn_down, final_norm_g):
    batch, seq, d = x.shape
    ctx_len = ctx.shape[1]
    depth = w_in.shape[0]
    n_lat, n_ctx = batch * seq, batch * ctx_len

    lb_w = jax.nn.softmax(hgrn_lb.astype(f32), axis=0)
    lower_bounds = jnp.cumsum(lb_w, axis=0) - lb_w[:1]

    mod_rows = 8 * (-(-(batch + 1) // 8))
    c_rows = jnp.zeros((mod_rows, d), f32).at[:batch].set(c).at[batch].set(c_ctx)
    mod_all = _modulation(c_rows, w_mod, b_mod)

    cos_t, sin_t = _rope_tables(seq)
    ones_t = jnp.ones((n_ctx, LANES), f32)
    zeros_t = jnp.zeros((n_ctx, LANES), f32)

    a_hi = 3 * A_QK + 2 * A_WIDTH
    b_hi = a_hi + 3 * B_WIDTH
    c_hi = b_hi + C_WIDTH + 2 * C_KV_WIDTH
    b_scale = jnp.concatenate([jnp.full((B_WIDTH,), B_DH ** -0.5, f32),
                               jnp.ones((2 * B_WIDTH,), f32)]).reshape(1, -1)
    ck_cols = tuple(C_WIDTH + 2 * g * LANES for g in range(C_KV_HEADS))
    cv_cols = tuple(C_WIDTH + C_EXP + 2 * g * LANES for g in range(C_KV_HEADS))
    bk_cols = tuple(B_WIDTH + h * B_DH for h in range(B_HEADS))
    bv_cols = tuple(2 * B_WIDTH + h * B_DH for h in range(B_HEADS))

    x_lat = x.reshape(n_lat, d)
    x_ctx = ctx.reshape(n_ctx, d)

    for l in range(depth):
        need_ctx = l < depth - 1
        mod_l = mod_all[l, :batch].reshape(batch, 1, N_MOD * d)
        mod_c = mod_all[l, batch:batch + 1].reshape(1, 1, N_MOD * d)
        w_l = w_in[l]
        w_a = w_l[:, :a_hi].astype(bf16)
        w_b = w_l[:, a_hi:b_hi].astype(bf16)
        w_c = w_l[:, b_hi:c_hi].astype(bf16)
        w_g = w_l[:, c_hi:].astype(bf16)
        wbr = w_branch[l].astype(bf16)
        wo = w_out[l].astype(bf16)
        wfg = w_ffn_gate[l].astype(bf16)
        wfu = w_ffn_up[l].astype(bf16)
        wfd = w_ffn_down[l].astype(bf16)
        na_tables, na_types = _na_tables(na_rpb[l], seq // GRID_W)

        h_lat = _norm_modulate(x_lat, norm1_g[l], mod_l, 0, seq)
        h_ctx = _norm_modulate(x_ctx, norm1_g[l], mod_c, 0, n_ctx)

        pa_lat = _project(h_lat, w_a, f32, 1024, name="proj_hgrn")
        pa_ctx = _project(h_ctx, w_a, f32, 1024, name="proj_hgrn")
        pb_lat = _project(h_lat, w_b, bf16, 1024, col_scale=b_scale, name="proj_na")
        pb_ctx = _project(h_ctx, w_b, bf16, 1024, col_scale=b_scale, name="proj_na")
        pc_lat = _project_window(h_lat, w_c, cos_t, sin_t)
        pc_ctx = _project_window(h_ctx, w_c, ones_t, zeros_t)
        g_lat = _project(h_lat, w_g, bf16, 1024, sigmoid=True, name="proj_gates")

        a_lat, a_ctx = _hgrn_mixer(pa_lat, pa_ctx, lower_bounds[l], a_norm_g[l], seq, ctx_len, need_ctx)
        b_lat = _neighborhood_attention(pb_lat, pb_ctx, na_tables, na_types, seq, ctx_len)
        c_lat = _window_attention(pc_lat, pc_ctx, c_sink[l], seq, ctx_len)

        wbr_a, wbr_b, wbr_c = wbr[:A_WIDTH], wbr[A_WIDTH:A_WIDTH + B_WIDTH], wbr[A_WIDTH + B_WIDTH:]
        m_lat = _merge(a_lat, b_lat, c_lat, g_lat, wbr_a, wbr_b, wbr_c)
        x_lat = _gated_residual_matmul(m_lat, wo, x_lat, mod_l, 2, seq, 1024, "out_proj")
        h2 = _norm_modulate(x_lat, norm2_g[l], mod_l, 3, seq)
        u = _swiglu_up(h2, wfg, wfu)
        x_lat = _gated_residual_matmul(u, wfd, x_lat, mod_l, 5, seq, 256, "ffn_down")

        if need_ctx:
            g_ctx = _project(h_ctx, w_g, bf16, 1024, sigmoid=True, name="proj_gates")
            b_ctx = _context_attention(pb_ctx, ctx_len, B_HEADS, B_DH, bk_cols, bv_cols)
            c_ctx_o = _context_attention(pc_ctx, ctx_len, C_HEADS, C_DH, ck_cols, cv_cols, sink=c_sink[l])
            m_ctx = _merge(a_ctx, b_ctx, c_ctx_o, g_ctx, wbr_a, wbr_b, wbr_c)
            x_ctx = _gated_residual_matmul(m_ctx, wo, x_ctx, mod_c, 2, n_ctx, 1024, "out_proj")
            h2c = _norm_modulate(x_ctx, norm2_g[l], mod_c, 3, n_ctx)
            uc = _swiglu_up(h2c, wfg, wfu)
            x_ctx = _gated_residual_matmul(uc, wfd, x_ctx, mod_c, 5, n_ctx, 256, "ffn_down")

    return _final_norm(x_lat, final_norm_g).reshape(batch, seq, d)
```

```python
import functools

import jax
import jax.numpy as jnp
import numpy as np
from jax import lax
from jax.experimental import pallas as pl
from jax.experimental.pallas import tpu as pltpu

GRID_W = 64
EPS = 1e-6
NEG_INF = -1e30
N_MOD = 6
A_HEADS, A_DK, A_DV, A_CHUNK = 16, 128, 128, 32
A_QK = A_HEADS * A_DK
A_WIDTH = A_HEADS * A_DV
B_HEADS, B_DH = 8, 128
B_WIDTH = B_HEADS * B_DH
NA_ROWS, NA_COLS = 8, 16
C_HEADS, C_KV_HEADS, C_DH = 16, 2, 64
C_WIDTH = C_HEADS * C_DH
C_KV_WIDTH = C_KV_HEADS * C_DH
C_WINDOW = 128
C_BLOCK = 128
ROPE_BASE = 10000.0
N_BRANCH = 3

LANES = 128
VMEM_LIMIT_BYTES = 56 * 1024 * 1024

HGRN_BLOCK = 256
HGRN_HEADS_PER_STEP = 4
NA_QROWS = 4
NA_UNION = NA_ROWS + NA_QROWS
NA_HEADS_PER_STEP = 2
FFN_TN = 256
CAST_BLOCK_BYTES = 4 * 1024 * 1024

C_PAIRS = C_WIDTH // LANES
C_EXP = 2 * C_KV_HEADS * LANES
C_OUT_WIDTH = C_WIDTH + 2 * C_EXP

_NT = (((1,), (1,)), ((), ()))
_TN = (((0,), (0,)), ((), ()))

bf16 = jnp.bfloat16
f32 = jnp.float32


def _params(*sem):
    return pltpu.CompilerParams(dimension_semantics=sem, vmem_limit_bytes=VMEM_LIMIT_BYTES)


def _row_tile(rows, want):
    t = min(rows, want)
    while rows % t:
        t //= 2
    return t


def _sigmoid(x):
    return 1.0 / (1.0 + jnp.exp(-x))


def _silu(x):
    return x * _sigmoid(x)


def _mod_kernel(c_ref, w_ref, b_ref, o_ref):
    a = _silu(c_ref[...]).astype(bf16)
    o_ref[...] = jnp.dot(a, w_ref[...].astype(bf16), preferred_element_type=f32) + b_ref[...]


def _modulation(c_rows, w_mod, b_mod):
    depth, d, n = w_mod.shape
    rows = c_rows.shape[0]
    tn = 512
    return pl.pallas_call(
        _mod_kernel,
        grid=(depth, n // tn),
        in_specs=[
            pl.BlockSpec((rows, d), lambda l, j: (0, 0)),
            pl.BlockSpec((None, d, tn), lambda l, j: (l, 0, j)),
            pl.BlockSpec((None, 1, tn), lambda l, j: (l, 0, j)),
        ],
        out_specs=pl.BlockSpec((None, rows, tn), lambda l, j: (l, 0, j)),
        out_shape=jax.ShapeDtypeStruct((depth, rows, n), f32),
        compiler_params=_params("arbitrary", "arbitrary"),
        name="modulation",
    )(c_rows, w_mod, b_mod.reshape(depth, 1, n))


def _norm_mod_kernel(x_ref, g_ref, shift_ref, scale_ref, o_ref):
    x = x_ref[...]
    y = x * lax.rsqrt(jnp.mean(x * x, axis=-1, keepdims=True) + EPS) * g_ref[...]
    o_ref[...] = (y * (1.0 + scale_ref[...]) + shift_ref[...]).astype(o_ref.dtype)


def _norm_kernel(x_ref, g_ref, o_ref):
    x = x_ref[...]
    y = x * lax.rsqrt(jnp.mean(x * x, axis=-1, keepdims=True) + EPS) * g_ref[...]
    o_ref[...] = y.astype(o_ref.dtype)


def _norm_modulate(x, g, mod, shift_idx, rows_per_group):
    rows, d = x.shape
    tm = _row_tile(rows_per_group, 256)
    per = rows_per_group // tm
    return pl.pallas_call(
        _norm_mod_kernel,
        grid=(rows // tm,),
        in_specs=[
            pl.BlockSpec((tm, d), lambda i: (i, 0)),
            pl.BlockSpec((1, d), lambda i: (0, 0)),
            pl.BlockSpec((None, 1, d), lambda i: (i // per, 0, shift_idx)),
            pl.BlockSpec((None, 1, d), lambda i: (i // per, 0, shift_idx + 1)),
        ],
        out_specs=pl.BlockSpec((tm, d), lambda i: (i, 0)),
        out_shape=jax.ShapeDtypeStruct((rows, d), bf16),
        compiler_params=_params("arbitrary"),
        name="norm_modulate",
    )(x, g.reshape(1, d), mod, mod)


def _final_norm(x, g):
    rows, d = x.shape
    tm = _row_tile(rows, 256)
    return pl.pallas_call(
        _norm_kernel,
        grid=(rows // tm,),
        in_specs=[pl.BlockSpec((tm, d), lambda i: (i, 0)), pl.BlockSpec((1, d), lambda i: (0, 0))],
        out_specs=pl.BlockSpec((tm, d), lambda i: (i, 0)),
        out_shape=jax.ShapeDtypeStruct((rows, d), f32),
        compiler_params=_params("arbitrary"),
        name="final_norm",
    )(x, g.reshape(1, d))


def _proj_kernel(h_ref, w_ref, o_ref):
    o_ref[...] = jnp.dot(h_ref[...], w_ref[...], preferred_element_type=f32).astype(o_ref.dtype)


def _proj_scale_kernel(h_ref, w_ref, s_ref, o_ref):
    acc = jnp.dot(h_ref[...], w_ref[...], preferred_element_type=f32)
    o_ref[...] = (acc * s_ref[...]).astype(o_ref.dtype)


def _proj_sigmoid_kernel(h_ref, w_ref, o_ref):
    acc = jnp.dot(h_ref[...], w_ref[...], preferred_element_type=f32)
    o_ref[...] = _sigmoid(acc).astype(o_ref.dtype)


def _cast_kernel(w_ref, *o_refs, bounds):
    for o_ref, (lo, hi) in zip(o_refs, bounds):
        o_ref[...] = w_ref[:, lo:hi].astype(o_ref.dtype)


def _cast_weight(w, layer, splits=None):
    _, k, n = w.shape
    bounds = tuple(splits) if splits else ((0, n),)
    tk = 1 << ((CAST_BLOCK_BYTES // (4 * n)).bit_length() - 1)
    while k % tk:
        tk //= 2
    outs = pl.pallas_call(
        functools.partial(_cast_kernel, bounds=bounds),
        grid=(k // tk,),
        in_specs=[pl.BlockSpec((None, tk, n), lambda i: (layer, i, 0))],
        out_specs=[pl.BlockSpec((tk, hi - lo), lambda i: (i, 0)) for lo, hi in bounds],
        out_shape=[jax.ShapeDtypeStruct((k, hi - lo), bf16) for lo, hi in bounds],
        compiler_params=_params("arbitrary"),
        name="cast_weight",
    )(w)
    return outs if splits else outs[0]


def _project(h, w, out_dtype, tn, *, tm_want=1024, col_scale=None, sigmoid=False, name="proj"):
    rows, d = h.shape
    n = w.shape[1]
    nj = n // tn
    tm = _row_tile(rows, tm_want)
    h_spec = pl.BlockSpec((tm, d), lambda i, j: (i, 0), pipeline_mode=pl.Buffered(1))
    w_spec = pl.BlockSpec((d, tn), lambda i, j: (0, j))
    args, specs = [h, w], [h_spec, w_spec]
    if sigmoid:
        kern = _proj_sigmoid_kernel
    elif col_scale is not None:
        kern = _proj_scale_kernel
        args.append(col_scale)
        specs.append(pl.BlockSpec((1, tn), lambda i, j: (0, j)))
    else:
        kern = _proj_kernel
    return pl.pallas_call(
        kern,
        grid=(rows // tm, nj),
        in_specs=specs,
        out_specs=pl.BlockSpec((tm, tn), lambda i, j: (i, j)),
        out_shape=jax.ShapeDtypeStruct((rows, n), out_dtype),
        compiler_params=_params("arbitrary", "arbitrary"),
        name=name,
    )(*args)


def _rope_rotate(x, cos, sin):
    n = x.shape[-1]
    lane = lax.broadcasted_iota(jnp.int32, x.shape, x.ndim - 1)
    up = pltpu.roll(x, n - 16, x.ndim - 1)
    down = pltpu.roll(x, 16, x.ndim - 1)
    return x * cos + jnp.where((lane % 32) < 16, up, down) * sin


def _proj_window_kernel(h_ref, w_ref, cos_ref, sin_ref, o_ref):
    acc = jnp.dot(h_ref[...], w_ref[...], preferred_element_type=f32)
    cos = cos_ref[...]
    sin = sin_ref[...]
    for j in range(C_PAIRS):
        x = _rope_rotate(acc[:, j * LANES:(j + 1) * LANES], cos, sin) * (C_DH ** -0.5)
        o_ref[:, j * LANES:(j + 1) * LANES] = x.astype(o_ref.dtype)
    k = _rope_rotate(acc[:, C_WIDTH:C_WIDTH + LANES], cos, sin)
    v = acc[:, C_WIDTH + LANES:C_WIDTH + 2 * LANES]
    low = lax.broadcasted_iota(jnp.int32, k.shape, 1) < C_DH
    for idx, x in enumerate((k, v)):
        swapped = pltpu.roll(x, C_DH, 1)
        groups = (jnp.where(low, x, 0.0), jnp.where(low, 0.0, swapped),
                  jnp.where(low, swapped, 0.0), jnp.where(low, 0.0, x))
        base = C_WIDTH + idx * C_EXP
        for c, val in enumerate(groups):
            o_ref[:, base + c * LANES:base + (c + 1) * LANES] = val.astype(o_ref.dtype)


def _project_window(h, w, cos, sin):
    rows, d = h.shape
    n = w.shape[1]
    tm = _row_tile(rows, 1024)
    tok_tiles = cos.shape[0] // tm
    t_spec = pl.BlockSpec((tm, LANES), lambda i: (i % tok_tiles, 0))
    return pl.pallas_call(
        _proj_window_kernel,
        grid=(rows // tm,),
        in_specs=[pl.BlockSpec((tm, d), lambda i: (i, 0)),
                  pl.BlockSpec((d, n), lambda i: (0, 0), pipeline_mode=pl.Buffered(1)),
                  t_spec, t_spec],
        out_specs=pl.BlockSpec((tm, C_OUT_WIDTH), lambda i: (i, 0)),
        out_shape=jax.ShapeDtypeStruct((rows, C_OUT_WIDTH), bf16),
        compiler_params=_params("arbitrary"),
        name="proj_window",
    )(h, w, cos, sin)


def _hgrn_kernel(*refs, rev, has_s0, emit_state, readout, hp):
    it = iter(refs)
    q_ref, v_ref, f_ref, lb_ref = next(it), next(it), next(it), next(it)
    s0_ref = next(it) if has_s0 else None
    if readout:
        g_ref, oprev_ref, ng_ref = next(it), next(it), next(it)
    o_ref = next(it)
    sout_ref = next(it) if emit_state else None
    st_ref = next(it)

    blk = pl.program_id(2)
    nblk = pl.num_programs(2)
    tb = q_ref.shape[0]
    nchunk = tb // A_CHUNK
    width = hp * A_DK
    heads = range(hp)

    def head(a, h):
        return a[:, h * A_DK:(h + 1) * A_DK]

    @pl.when(blk == 0)
    def _():
        if has_s0:
            st_ref[...] = s0_ref[...]
        else:
            st_ref[...] = jnp.zeros_like(st_ref)

    lb = lb_ref[...]
    f = lb + (1.0 - lb) * _sigmoid(f_ref[...])
    logf = jnp.log(f)
    k = 1.0 - f

    row = lax.broadcasted_iota(jnp.int32, (tb, tb), 0)
    col = lax.broadcasted_iota(jnp.int32, (tb, tb), 1)
    same = (row // A_CHUNK) == (col // A_CHUNK)
    causal = (col >= row) if rev else (col <= row)
    mask = same & causal
    tri = jnp.where(mask, 1.0, 0.0).astype(bf16)

    hi = logf.astype(bf16)
    lo = (logf - hi.astype(f32)).astype(bf16)
    cum2 = jnp.dot(tri, jnp.concatenate([hi, lo], axis=1), preferred_element_type=f32)
    cum = cum2[:, :width] + cum2[:, width:]

    q_dec = (_silu(q_ref[...]) * jnp.exp(cum)).astype(bf16)
    k_inv = (k * jnp.exp(-cum)).astype(bf16)
    v = v_ref[...].astype(bf16)
    att = [lax.dot_general(head(q_dec, h), head(k_inv, h), _NT, preferred_element_type=f32)
           for h in heads]
    att = [jnp.where(mask, a, 0.0).astype(bf16) for a in att]
    o_intra = [jnp.dot(att[h], head(v, h), preferred_element_type=f32) for h in heads]

    order = range(nchunk - 1, -1, -1) if rev else range(nchunk)
    k_end, dec = {}, {}
    for j in order:
        sl = slice(j * A_CHUNK, (j + 1) * A_CHUNK)
        last = j * A_CHUNK if rev else (j + 1) * A_CHUNK - 1
        tot = cum[last:last + 1, :]
        k_end[j] = (k[sl] * jnp.exp(tot - cum[sl])).astype(bf16)
        dec[j] = jnp.exp(tot)
    upd = {(h, j): lax.dot_general(head(v[j * A_CHUNK:(j + 1) * A_CHUNK], h), head(k_end[j], h), _TN,
                                   preferred_element_type=f32)
           for j in order for h in heads}

    before = {}
    final = []
    for h in heads:
        s = st_ref[h]
        for j in order:
            before[h, j] = s.astype(bf16)
            s = s * head(dec[j], h) + upd[h, j]
        st_ref[h] = s
        final.append(s)

    outs = []
    for h in heads:
        o_inter = [lax.dot_general(head(q_dec[j * A_CHUNK:(j + 1) * A_CHUNK], h), before[h, j], _NT,
                                   preferred_element_type=f32) for j in range(nchunk)]
        outs.append(o_intra[h] + jnp.concatenate(o_inter, axis=0))

    if readout:
        ng = ng_ref[...]
        normed = []
        for h in heads:
            o = outs[h] + head(oprev_ref[...], h)
            normed.append(o * lax.rsqrt(jnp.mean(o * o, axis=-1, keepdims=True) + EPS) * ng)
        o_ref[...] = (jnp.concatenate(normed, axis=1) * _silu(g_ref[...])).astype(o_ref.dtype)
    else:
        o_ref[...] = jnp.concatenate(outs, axis=1)

    if emit_state:
        @pl.when(blk == nblk - 1)
        def _():
            for h in heads:
                sout_ref[h] = final[h]


def _hgrn_scan(proj, lb_dir, seq, direction, *, s0=None, emit_state=False, readout=None):
    rows = proj.shape[0]
    batch = rows // seq
    tb = min(HGRN_BLOCK, seq)
    nblk = seq // tb
    rev = direction == 1
    hp = HGRN_HEADS_PER_STEP
    hblocks = A_HEADS // hp
    width = hp * A_DK

    def tok(b, h, i):
        return b * nblk + (nblk - 1 - i if rev else i)

    def col_spec(group):
        return pl.BlockSpec((tb, width), lambda b, h, i: (tok(b, h, i), group * hblocks + h))

    state_spec = pl.BlockSpec((None, hp, A_DV, A_DK), lambda b, h, i: (b, h, 0, 0))
    args = [proj, proj, proj, lb_dir.reshape(1, A_QK)]
    specs = [col_spec(0), col_spec(1), col_spec(2 + direction),
             pl.BlockSpec((1, width), lambda b, h, i: (0, h))]
    if s0 is not None:
        args.append(s0)
        specs.append(state_spec)
    if readout is not None:
        o_prev, norm_g = readout
        args += [proj, o_prev, norm_g.reshape(1, A_DV)]
        specs += [col_spec(4),
                  pl.BlockSpec((tb, width), lambda b, h, i: (tok(b, h, i), h)),
                  pl.BlockSpec((1, A_DV), lambda b, h, i: (0, 0))]
    out_shape = [jax.ShapeDtypeStruct((rows, A_WIDTH), bf16 if readout is not None else f32)]
    out_specs = [pl.BlockSpec((tb, width), lambda b, h, i: (tok(b, h, i), h))]
    if emit_state:
        out_shape.append(jax.ShapeDtypeStruct((batch, A_HEADS, A_DV, A_DK), f32))
        out_specs.append(state_spec)
    kern = functools.partial(_hgrn_kernel, rev=rev, has_s0=s0 is not None,
                             emit_state=emit_state, readout=readout is not None, hp=hp)
    res = pl.pallas_call(
        kern,
        grid=(batch, hblocks, nblk),
        in_specs=specs,
        out_specs=out_specs,
        out_shape=out_shape,
        scratch_shapes=[pltpu.VMEM((hp, A_DV, A_DK), f32)],
        compiler_params=_params("arbitrary", "arbitrary", "arbitrary"),
        name="hgrn_scan",
    )(*args)
    return res if emit_state else res[0]


def _hgrn_mixer(proj_lat, proj_ctx, lb, norm_g, seq, ctx_len, need_ctx):
    o_c_f, s_f = _hgrn_scan(proj_ctx, lb[0], ctx_len, 0, emit_state=True)
    o_l_f = _hgrn_scan(proj_lat, lb[0], seq, 0, s0=s_f)
    if need_ctx:
        a_ctx, s_b = _hgrn_scan(proj_ctx, lb[1], ctx_len, 1, emit_state=True, readout=(o_c_f, norm_g))
    else:
        _, s_b = _hgrn_scan(proj_ctx, lb[1], ctx_len, 1, emit_state=True)
        a_ctx = None
    a_lat = _hgrn_scan(proj_lat, lb[1], seq, 1, s0=s_b, readout=(o_l_f, norm_g))
    return a_lat, a_ctx


def _attend(parts, sink=None):
    m = parts[0][0].max(axis=-1, keepdims=True)
    for s, _ in parts[1:]:
        m = jnp.maximum(m, s.max(axis=-1, keepdims=True))
    if sink is not None:
        m = jnp.maximum(m, sink)
    den = None
    acc = None
    for s, v in parts:
        e = jnp.exp(s - m)
        d = e.sum(axis=-1, keepdims=True)
        o = jnp.dot(e.astype(bf16), v, preferred_element_type=f32)
        den = d if den is None else den + d
        acc = o if acc is None else acc + o
    if sink is not None:
        den = den + jnp.exp(sink - m)
    return acc / den


def _na_kernel(types_ref, q_ref, k_ref, v_ref, kc_ref, vc_ref, tbl_ref, o_ref, *, grid_rows, hp):
    del types_ref
    step = pl.program_id(2)
    span = NA_UNION * GRID_W
    first = jnp.clip(step * NA_QROWS - NA_ROWS // 2, 0, grid_rows - NA_UNION)
    start = pl.multiple_of(first * GRID_W, GRID_W)
    for h in range(hp):
        hs = slice(h * B_DH, (h + 1) * B_DH)
        q = q_ref[:, hs]
        kn = k_ref[pl.ds(start, span), hs]
        vn = v_ref[pl.ds(start, span), hs]
        s_nb = lax.dot_general(q, kn, _NT, preferred_element_type=f32) + tbl_ref[h]
        s_cx = lax.dot_general(q, kc_ref[:, hs], _NT, preferred_element_type=f32)
        o = _attend([(s_nb, vn), (s_cx, vc_ref[:, hs])])
        o_ref[:, hs] = o.astype(o_ref.dtype)


def _na_tables(rpb, grid_rows):
    assert grid_rows >= NA_UNION and grid_rows % NA_QROWS == 0
    col = np.arange(GRID_W)
    col_off = np.clip(col[None, :] - col[:, None] + NA_COLS - 1, 0, 2 * NA_COLS - 2)
    col_start = np.clip(col - NA_COLS // 2, 0, GRID_W - NA_COLS)
    col_ok = (col[None, :] >= col_start[:, None]) & (col[None, :] < col_start[:, None] + NA_COLS)
    seen, types = {}, []
    for i in range(grid_rows // NA_QROWS):
        first = int(np.clip(i * NA_QROWS - NA_ROWS // 2, 0, grid_rows - NA_UNION))
        key_row = first + np.arange(NA_UNION)[None, :]
        r = i * NA_QROWS + np.arange(NA_QROWS)[:, None]
        row_start = np.clip(r - NA_ROWS // 2, 0, grid_rows - NA_ROWS)
        ok = (key_row >= row_start) & (key_row < row_start + NA_ROWS)
        assert (ok.sum(axis=1) == NA_ROWS).all()
        off = np.where(ok, key_row - r + NA_ROWS - 1, 0)
        sig = (ok.tobytes(), off.tobytes())
        if sig not in seen:
            seen[sig] = (len(seen), ok, off)
        types.append(seen[sig][0])
    toe = jnp.where(col_ok[None, None], rpb.astype(f32)[:, :, col_off], NEG_INF)
    neg = jnp.full((rpb.shape[0], GRID_W, GRID_W), NEG_INF, f32)
    tables = []
    for _, ok, off in sorted(seen.values(), key=lambda t: t[0]):
        rows = [jnp.concatenate([toe[:, off[rl, a]] if ok[rl, a] else neg for a in range(NA_UNION)],
                                axis=-1) for rl in range(NA_QROWS)]
        tables.append(jnp.concatenate(rows, axis=-2))
    return jnp.stack(tables, axis=1), jnp.asarray(np.array(types, np.int32))


def _neighborhood_attention(qkv_lat, qkv_ctx, tables, types, seq, ctx_len):
    rows = qkv_lat.shape[0]
    batch = rows // seq
    tq = NA_QROWS * GRID_W
    nq = seq // tq
    hp = NA_HEADS_PER_STEP
    hb = B_HEADS // hp
    width = hp * B_DH
    kern = functools.partial(_na_kernel, grid_rows=seq // GRID_W, hp=hp)
    grid_spec = pltpu.PrefetchScalarGridSpec(
        num_scalar_prefetch=1,
        grid=(batch, hb, nq),
        in_specs=[
            pl.BlockSpec((tq, width), lambda b, h, i, t: (b * nq + i, h)),
            pl.BlockSpec((seq, width), lambda b, h, i, t: (b, hb + h)),
            pl.BlockSpec((seq, width), lambda b, h, i, t: (b, 2 * hb + h)),
            pl.BlockSpec((ctx_len, width), lambda b, h, i, t: (b, hb + h)),
            pl.BlockSpec((ctx_len, width), lambda b, h, i, t: (b, 2 * hb + h)),
            pl.BlockSpec((hp, None, tq, NA_UNION * GRID_W), lambda b, h, i, t: (h, t[i], 0, 0)),
        ],
        out_specs=pl.BlockSpec((tq, width), lambda b, h, i, t: (b * nq + i, h)),
    )
    return pl.pallas_call(
        kern,
        grid_spec=grid_spec,
        out_shape=jax.ShapeDtypeStruct((rows, B_WIDTH), bf16),
        compiler_params=_params("arbitrary", "arbitrary", "arbitrary"),
        name="neighborhood_attention",
    )(types, qkv_lat, qkv_lat, qkv_lat, qkv_ctx, qkv_ctx, tables)


def _window_kernel(sink_ref, q_ref, k_ref, v_ref, kc_ref, vc_ref, o_ref, *, seq):
    n = pl.program_id(1)
    span = 3 * C_BLOCK
    start = pl.multiple_of(jnp.clip((n - 1) * C_BLOCK, 0, seq - span), C_BLOCK)
    pairs = C_PAIRS // C_KV_HEADS
    rows = pairs * C_BLOCK
    qpos = n * C_BLOCK + lax.broadcasted_iota(jnp.int32, (rows, span), 0) % C_BLOCK
    kpos = start + lax.broadcasted_iota(jnp.int32, (rows, span), 1)
    valid = jnp.abs(qpos - kpos) <= C_WINDOW
    pair_id = lax.broadcasted_iota(jnp.int32, (rows, 1), 0) // C_BLOCK

    chains = [(g, e) for g in range(C_KV_HEADS) for e in range(2)]
    q = {g: jnp.concatenate([q_ref[:, (g * pairs + p) * LANES:(g * pairs + p + 1) * LANES]
                             for p in range(pairs)], axis=0) for g in range(C_KV_HEADS)}
    s_w, s_c, sink = {}, {}, {}
    for g, e in chains:
        cs = slice((2 * g + e) * LANES, (2 * g + e + 1) * LANES)
        s = lax.dot_general(q[g], k_ref[pl.ds(start, span), cs], _NT, preferred_element_type=f32)
        s_w[g, e] = jnp.where(valid, s, NEG_INF)
        s_c[g, e] = lax.dot_general(q[g], kc_ref[:, cs], _NT, preferred_element_type=f32)
        col = jnp.zeros((rows, 1), f32)
        for p in range(pairs):
            col = jnp.where(pair_id == p, sink_ref[(g * pairs + p) * 2 + e], col)
        sink[g, e] = col
    out = {}
    for g, e in chains:
        cs = slice((2 * g + e) * LANES, (2 * g + e + 1) * LANES)
        out[g, e] = _attend([(s_w[g, e], v_ref[pl.ds(start, span), cs]), (s_c[g, e], vc_ref[:, cs])],
                            sink=sink[g, e])
    for g in range(C_KV_HEADS):
        o = out[g, 0] + out[g, 1]
        for p in range(pairs):
            o_ref[:, (g * pairs + p) * LANES:(g * pairs + p + 1) * LANES] = (
                o[p * C_BLOCK:(p + 1) * C_BLOCK].astype(o_ref.dtype))


def _window_attention(qkv_lat, qkv_ctx, sink, seq, ctx_len):
    rows = qkv_lat.shape[0]
    batch = rows // seq
    nq = seq // C_BLOCK
    kblk = C_WIDTH // C_EXP
    kern = functools.partial(_window_kernel, seq=seq)
    resident = dict(pipeline_mode=pl.Buffered(1))
    return pl.pallas_call(
        kern,
        grid=(batch, nq),
        in_specs=[
            pl.BlockSpec(memory_space=pltpu.SMEM),
            pl.BlockSpec((C_BLOCK, C_WIDTH), lambda b, i: (b * nq + i, 0)),
            pl.BlockSpec((seq, C_EXP), lambda b, i: (b, kblk), **resident),
            pl.BlockSpec((seq, C_EXP), lambda b, i: (b, kblk + 1), **resident),
            pl.BlockSpec((ctx_len, C_EXP), lambda b, i: (b, kblk)),
            pl.BlockSpec((ctx_len, C_EXP), lambda b, i: (b, kblk + 1)),
        ],
        out_specs=pl.BlockSpec((C_BLOCK, C_WIDTH), lambda b, i: (b * nq + i, 0)),
        out_shape=jax.ShapeDtypeStruct((rows, C_WIDTH), bf16),
        compiler_params=_params("arbitrary", "arbitrary"),
        name="window_attention",
    )(sink.astype(f32), qkv_lat, qkv_lat, qkv_lat, qkv_ctx, qkv_ctx)


def _ctx_attn_kernel(*refs, heads, dh, k_cols, v_cols, has_sink):
    if has_sink:
        sink_ref, qkv_ref, o_ref = refs
    else:
        qkv_ref, o_ref = refs
    group = heads // len(k_cols)
    for kh, (kc, vc) in enumerate(zip(k_cols, v_cols)):
        k = qkv_ref[:, kc:kc + dh]
        v = qkv_ref[:, vc:vc + dh]
        for g in range(group):
            hq = kh * group + g
            q = qkv_ref[:, hq * dh:(hq + 1) * dh]
            s = lax.dot_general(q, k, _NT, preferred_element_type=f32)
            o = _attend([(s, v)], sink=sink_ref[hq] if has_sink else None)
            o_ref[:, hq * dh:(hq + 1) * dh] = o.astype(o_ref.dtype)


def _context_attention(qkv_ctx, ctx_len, heads, dh, k_cols, v_cols, sink=None):
    rows, width = qkv_ctx.shape
    kern = functools.partial(_ctx_attn_kernel, heads=heads, dh=dh, k_cols=k_cols, v_cols=v_cols,
                             has_sink=sink is not None)
    args, specs = [qkv_ctx], [pl.BlockSpec((ctx_len, width), lambda b: (b, 0))]
    if sink is not None:
        args.insert(0, sink.astype(f32))
        specs.insert(0, pl.BlockSpec(memory_space=pltpu.SMEM))
    return pl.pallas_call(
        kern,
        grid=(rows // ctx_len,),
        in_specs=specs,
        out_specs=pl.BlockSpec((ctx_len, heads * dh), lambda b: (b, 0)),
        out_shape=jax.ShapeDtypeStruct((rows, heads * dh), bf16),
        compiler_params=_params("arbitrary"),
        name="context_attention",
    )(*args)


def _merge_kernel(oa_ref, ob_ref, oc_ref, wa_ref, wb_ref, wc_ref, ga_ref, gb_ref, gc_ref, o_ref):
    br_a = jnp.dot(oa_ref[...], wa_ref[...], preferred_element_type=f32)
    br_b = jnp.dot(ob_ref[...], wb_ref[...], preferred_element_type=f32)
    br_c = jnp.dot(oc_ref[...], wc_ref[...], preferred_element_type=f32)
    m = ga_ref[...] * br_a + gb_ref[...] * br_b + gc_ref[...] * br_c
    o_ref[...] = m.astype(o_ref.dtype)


def _merge(o_a, o_b, o_c, gates, w_branch):
    rows = o_a.shape[0]
    d = w_branch.shape[1]
    tm = _row_tile(rows, 1024)
    tn = 512
    nj = d // tn
    assert A_WIDTH % B_WIDTH == 0 and B_WIDTH == C_WIDTH
    b_blk = A_WIDTH // B_WIDTH
    return pl.pallas_call(
        _merge_kernel,
        grid=(rows // tm, nj),
        in_specs=[
            pl.BlockSpec((tm, A_WIDTH), lambda i, j: (i, 0)),
            pl.BlockSpec((tm, B_WIDTH), lambda i, j: (i, 0)),
            pl.BlockSpec((tm, C_WIDTH), lambda i, j: (i, 0)),
            pl.BlockSpec((A_WIDTH, tn), lambda i, j: (0, j)),
            pl.BlockSpec((B_WIDTH, tn), lambda i, j: (b_blk, j)),
            pl.BlockSpec((C_WIDTH, tn), lambda i, j: (b_blk + 1, j)),
            pl.BlockSpec((tm, tn), lambda i, j: (i, j)),
            pl.BlockSpec((tm, tn), lambda i, j: (i, nj + j)),
            pl.BlockSpec((tm, tn), lambda i, j: (i, 2 * nj + j)),
        ],
        out_specs=pl.BlockSpec((tm, tn), lambda i, j: (i, j)),
        out_shape=jax.ShapeDtypeStruct((rows, d), bf16),
        compiler_params=_params("arbitrary", "arbitrary"),
        name="merge_branches",
    )(o_a, o_b, o_c, w_branch, w_branch, w_branch, gates, gates, gates)


def _residual_kernel(a_ref, w_ref, x_ref, gate_ref, o_ref):
    y = jnp.dot(a_ref[...], w_ref[...], preferred_element_type=f32)
    o_ref[...] = x_ref[...] + gate_ref[...] * y


def _gated_residual_matmul(a, w, x, mod, gate_idx, rows_per_group, tn, name):
    rows, kdim = a.shape
    d = w.shape[1]
    tm = _row_tile(rows_per_group, 1024)
    per = rows_per_group // tm
    nj = d // tn
    return pl.pallas_call(
        _residual_kernel,
        grid=(rows // tm, nj),
        in_specs=[
            pl.BlockSpec((tm, kdim), lambda i, j: (i, 0), pipeline_mode=pl.Buffered(1)),
            pl.BlockSpec((kdim, tn), lambda i, j: (0, j)),
            pl.BlockSpec((tm, tn), lambda i, j: (i, j)),
            pl.BlockSpec((None, 1, tn), lambda i, j: (i // per, 0, gate_idx * nj + j)),
        ],
        out_specs=pl.BlockSpec((tm, tn), lambda i, j: (i, j)),
        out_shape=jax.ShapeDtypeStruct((rows, d), f32),
        compiler_params=_params("arbitrary", "arbitrary"),
        name=name,
    )(a, w, x, mod)


def _swiglu_kernel(h_ref, wg_ref, wu_ref, o_ref):
    h = h_ref[...]
    g = jnp.dot(h, wg_ref[...], preferred_element_type=f32)
    u = jnp.dot(h, wu_ref[...], preferred_element_type=f32)
    o_ref[...] = (_silu(g) * u).astype(o_ref.dtype)


def _swiglu_up(h, w_gate, w_up):
    rows, d = h.shape
    n = w_gate.shape[1]
    tm = _row_tile(rows, 1024)
    tn = FFN_TN
    w_spec = pl.BlockSpec((d, tn), lambda i, j: (0, j))
    return pl.pallas_call(
        _swiglu_kernel,
        grid=(rows // tm, n // tn),
        in_specs=[pl.BlockSpec((tm, d), lambda i, j: (i, 0)), w_spec, w_spec],
        out_specs=pl.BlockSpec((tm, tn), lambda i, j: (i, j)),
        out_shape=jax.ShapeDtypeStruct((rows, n), bf16),
        compiler_params=_params("arbitrary", "arbitrary"),
        name="swiglu_up",
    )(h, w_gate, w_up)


def _rope_tables(seq):
    half = C_DH // 2
    pos = jnp.arange(seq)
    inv = ROPE_BASE ** (-jnp.arange(0, half, 2, dtype=f32) / half)
    ang_row = (pos // GRID_W).astype(f32)[:, None] * inv[None, :]
    ang_col = (pos % GRID_W).astype(f32)[:, None] * inv[None, :]

    def one(ang):
        return (jnp.concatenate([jnp.cos(ang), jnp.cos(ang)], axis=-1),
                jnp.concatenate([-jnp.sin(ang), jnp.sin(ang)], axis=-1))

    cr, sr = one(ang_row)
    cc, sc = one(ang_col)
    cos = jnp.concatenate([cr, cc], axis=-1)
    sin = jnp.concatenate([sr, sc], axis=-1)
    reps = LANES // C_DH
    return jnp.tile(cos, (1, reps)), jnp.tile(sin, (1, reps))


def kernel(x, c, ctx, c_ctx, norm1_g, norm2_g, w_mod, b_mod, w_in, hgrn_lb, a_norm_g, na_rpb,
           c_sink, w_branch, w_out, w_ffn_gate, w_ffn_up, w_ffn_down, final_norm_g):
    batch, seq, d = x.shape
    ctx_len = ctx.shape[1]
    depth = w_in.shape[0]
    n_lat, n_ctx = batch * seq, batch * ctx_len

    lb_w = jax.nn.softmax(hgrn_lb.astype(f32), axis=0)
    lower_bounds = jnp.cumsum(lb_w, axis=0) - lb_w[:1]

    mod_rows = 8 * (-(-(batch + 1) // 8))
    c_rows = jnp.zeros((mod_rows, d), f32).at[:batch].set(c).at[batch].set(c_ctx)
    mod_all = _modulation(c_rows, w_mod, b_mod)

    cos_t, sin_t = _rope_tables(seq)
    ones_t = jnp.ones((n_ctx, LANES), f32)
    zeros_t = jnp.zeros((n_ctx, LANES), f32)

    a_hi = 3 * A_QK + 2 * A_WIDTH
    b_hi = a_hi + 3 * B_WIDTH
    c_hi = b_hi + C_WIDTH + 2 * C_KV_WIDTH
    in_width = w_in.shape[2]
    b_scale = jnp.concatenate([jnp.full((B_WIDTH,), B_DH ** -0.5, f32),
                               jnp.ones((2 * B_WIDTH,), f32)]).reshape(1, -1)
    ck_cols = tuple(C_WIDTH + 2 * g * LANES for g in range(C_KV_HEADS))
    cv_cols = tuple(C_WIDTH + C_EXP + 2 * g * LANES for g in range(C_KV_HEADS))
    bk_cols = tuple(B_WIDTH + h * B_DH for h in range(B_HEADS))
    bv_cols = tuple(2 * B_WIDTH + h * B_DH for h in range(B_HEADS))

    x_lat = x.reshape(n_lat, d)
    x_ctx = ctx.reshape(n_ctx, d)

    for l in range(depth):
        need_ctx = l < depth - 1
        mod_l = mod_all[l, :batch].reshape(batch, 1, N_MOD * d)
        mod_c = mod_all[l, batch:batch + 1].reshape(1, 1, N_MOD * d)
        w_a, w_b, w_c, w_g = _cast_weight(w_in, l, ((0, a_hi), (a_hi, b_hi), (b_hi, c_hi), (c_hi, in_width)))
        wbr = _cast_weight(w_branch, l)
        wo = _cast_weight(w_out, l)
        wfg = _cast_weight(w_ffn_gate, l)
        wfu = _cast_weight(w_ffn_up, l)
        wfd = _cast_weight(w_ffn_down, l)
        na_tables, na_types = _na_tables(na_rpb[l], seq // GRID_W)

        h_lat = _norm_modulate(x_lat, norm1_g[l], mod_l, 0, seq)
        h_ctx = _norm_modulate(x_ctx, norm1_g[l], mod_c, 0, n_ctx)

        pa_lat = _project(h_lat, w_a, f32, 1024, name="proj_hgrn")
        pa_ctx = _project(h_ctx, w_a, f32, 1024, name="proj_hgrn")
        pb_lat = _project(h_lat, w_b, bf16, 512, tm_want=2048, col_scale=b_scale, name="proj_na")
        pb_ctx = _project(h_ctx, w_b, bf16, 512, tm_want=2048, col_scale=b_scale, name="proj_na")
        pc_lat = _project_window(h_lat, w_c, cos_t, sin_t)
        pc_ctx = _project_window(h_ctx, w_c, ones_t, zeros_t)
        g_lat = _project(h_lat, w_g, bf16, 1024, sigmoid=True, name="proj_gates")

        a_lat, a_ctx = _hgrn_mixer(pa_lat, pa_ctx, lower_bounds[l], a_norm_g[l], seq, ctx_len, need_ctx)
        b_lat = _neighborhood_attention(pb_lat, pb_ctx, na_tables, na_types, seq, ctx_len)
        c_lat = _window_attention(pc_lat, pc_ctx, c_sink[l], seq, ctx_len)

        m_lat = _merge(a_lat, b_lat, c_lat, g_lat, wbr)
        x_lat = _gated_residual_matmul(m_lat, wo, x_lat, mod_l, 2, seq, 1024, "out_proj")
        h2 = _norm_modulate(x_lat, norm2_g[l], mod_l, 3, seq)
        u = _swiglu_up(h2, wfg, wfu)
        x_lat = _gated_residual_matmul(u, wfd, x_lat, mod_l, 5, seq, FFN_TN, "ffn_down")

        if need_ctx:
            g_ctx = _project(h_ctx, w_g, bf16, 1024, sigmoid=True, name="proj_gates")
            b_ctx = _context_attention(pb_ctx, ctx_len, B_HEADS, B_DH, bk_cols, bv_cols)
            c_ctx_o = _context_attention(pc_ctx, ctx_len, C_HEADS, C_DH, ck_cols, cv_cols, sink=c_sink[l])
            m_ctx = _merge(a_ctx, b_ctx, c_ctx_o, g_ctx, wbr)
            x_ctx = _gated_residual_matmul(m_ctx, wo, x_ctx, mod_c, 2, n_ctx, 1024, "out_proj")
            h2c = _norm_modulate(x_ctx, norm2_g[l], mod_c, 3, n_ctx)
            uc = _swiglu_up(h2c, wfg, wfu)
            x_ctx = _gated_residual_matmul(uc, wfd, x_ctx, mod_c, 5, n_ctx, FFN_TN, "ffn_down")

    return _final_norm(x_lat, final_norm_g).reshape(batch, seq, d)
```

```python
import functools

import jax
import jax.numpy as jnp
import numpy as np
from jax import lax
from jax.experimental import pallas as pl
from jax.experimental.pallas import tpu as pltpu

GRID_W = 64
EPS = 1e-6
NEG_INF = -1e30
N_MOD = 6
A_HEADS, A_DK, A_DV, A_CHUNK = 16, 128, 128, 32
A_QK = A_HEADS * A_DK
A_WIDTH = A_HEADS * A_DV
B_HEADS, B_DH = 8, 128
B_WIDTH = B_HEADS * B_DH
NA_ROWS, NA_COLS = 8, 16
C_HEADS, C_KV_HEADS, C_DH = 16, 2, 64
C_WIDTH = C_HEADS * C_DH
C_KV_WIDTH = C_KV_HEADS * C_DH
C_WINDOW = 128
C_BLOCK = 128
ROPE_BASE = 10000.0
N_BRANCH = 3

LANES = 128
VMEM_LIMIT_BYTES = 56 * 1024 * 1024

HGRN_BLOCK = 256
HGRN_HEADS_PER_STEP = 4
NA_QROWS = 4
NA_UNION = NA_ROWS + NA_QROWS
NA_HEADS_PER_STEP = 2
FFN_TN = 256
CAST_BLOCK_BYTES = 4 * 1024 * 1024

C_PAIRS = C_WIDTH // LANES
C_EXP = 2 * C_KV_HEADS * LANES
C_OUT_WIDTH = C_WIDTH + 2 * C_EXP

_NT = (((1,), (1,)), ((), ()))
_TN = (((0,), (0,)), ((), ()))

bf16 = jnp.bfloat16
f32 = jnp.float32


def _params(*sem):
    return pltpu.CompilerParams(dimension_semantics=sem, vmem_limit_bytes=VMEM_LIMIT_BYTES)


def _row_tile(rows, want):
    t = min(rows, want)
    while rows % t:
        t //= 2
    return t


def _sigmoid(x):
    return 1.0 / (1.0 + jnp.exp(-x))


def _silu(x):
    return x * _sigmoid(x)


def _mod_kernel(c_ref, w_ref, b_ref, o_ref):
    a = _silu(c_ref[...]).astype(bf16)
    o_ref[...] = jnp.dot(a, w_ref[...].astype(bf16), preferred_element_type=f32) + b_ref[...]


def _modulation(c_rows, w_mod, b_mod):
    depth, d, n = w_mod.shape
    rows = c_rows.shape[0]
    tn = 512
    return pl.pallas_call(
        _mod_kernel,
        grid=(depth, n // tn),
        in_specs=[
            pl.BlockSpec((rows, d), lambda l, j: (0, 0)),
            pl.BlockSpec((None, d, tn), lambda l, j: (l, 0, j)),
            pl.BlockSpec((None, 1, tn), lambda l, j: (l, 0, j)),
        ],
        out_specs=pl.BlockSpec((None, rows, tn), lambda l, j: (l, 0, j)),
        out_shape=jax.ShapeDtypeStruct((depth, rows, n), f32),
        compiler_params=_params("arbitrary", "arbitrary"),
        name="modulation",
    )(c_rows, w_mod, b_mod.reshape(depth, 1, n))


def _norm_mod_kernel(x_ref, g_ref, shift_ref, scale_ref, o_ref):
    x = x_ref[...]
    y = x * lax.rsqrt(jnp.mean(x * x, axis=-1, keepdims=True) + EPS) * g_ref[...]
    o_ref[...] = (y * (1.0 + scale_ref[...]) + shift_ref[...]).astype(o_ref.dtype)


def _norm_kernel(x_ref, g_ref, o_ref):
    x = x_ref[...]
    y = x * lax.rsqrt(jnp.mean(x * x, axis=-1, keepdims=True) + EPS) * g_ref[...]
    o_ref[...] = y.astype(o_ref.dtype)


def _norm_modulate(x, g, mod, shift_idx, rows_per_group):
    rows, d = x.shape
    tm = _row_tile(rows_per_group, 256)
    per = rows_per_group // tm
    return pl.pallas_call(
        _norm_mod_kernel,
        grid=(rows // tm,),
        in_specs=[
            pl.BlockSpec((tm, d), lambda i: (i, 0)),
            pl.BlockSpec((1, d), lambda i: (0, 0)),
            pl.BlockSpec((None, 1, d), lambda i: (i // per, 0, shift_idx)),
            pl.BlockSpec((None, 1, d), lambda i: (i // per, 0, shift_idx + 1)),
        ],
        out_specs=pl.BlockSpec((tm, d), lambda i: (i, 0)),
        out_shape=jax.ShapeDtypeStruct((rows, d), bf16),
        compiler_params=_params("arbitrary"),
        name="norm_modulate",
    )(x, g.reshape(1, d), mod, mod)


def _final_norm(x, g):
    rows, d = x.shape
    tm = _row_tile(rows, 256)
    return pl.pallas_call(
        _norm_kernel,
        grid=(rows // tm,),
        in_specs=[pl.BlockSpec((tm, d), lambda i: (i, 0)), pl.BlockSpec((1, d), lambda i: (0, 0))],
        out_specs=pl.BlockSpec((tm, d), lambda i: (i, 0)),
        out_shape=jax.ShapeDtypeStruct((rows, d), f32),
        compiler_params=_params("arbitrary"),
        name="final_norm",
    )(x, g.reshape(1, d))


def _proj_kernel(h_ref, w_ref, o_ref):
    o_ref[...] = jnp.dot(h_ref[...], w_ref[...], preferred_element_type=f32).astype(o_ref.dtype)


def _proj_scale_kernel(h_ref, w_ref, s_ref, o_ref):
    acc = jnp.dot(h_ref[...], w_ref[...], preferred_element_type=f32)
    o_ref[...] = (acc * s_ref[...]).astype(o_ref.dtype)


def _proj_sigmoid_kernel(h_ref, w_ref, o_ref):
    acc = jnp.dot(h_ref[...], w_ref[...], preferred_element_type=f32)
    o_ref[...] = _sigmoid(acc).astype(o_ref.dtype)


def _cast_kernel(w_ref, *o_refs, bounds):
    for o_ref, (lo, hi) in zip(o_refs, bounds):
        o_ref[...] = w_ref[:, lo:hi].astype(o_ref.dtype)


def _cast_weight(w, layer, splits=None):
    _, k, n = w.shape
    bounds = tuple(splits) if splits else ((0, n),)
    tk = 1 << ((CAST_BLOCK_BYTES // (4 * n)).bit_length() - 1)
    while k % tk:
        tk //= 2
    outs = pl.pallas_call(
        functools.partial(_cast_kernel, bounds=bounds),
        grid=(k // tk,),
        in_specs=[pl.BlockSpec((None, tk, n), lambda i: (layer, i, 0))],
        out_specs=[pl.BlockSpec((tk, hi - lo), lambda i: (i, 0)) for lo, hi in bounds],
        out_shape=[jax.ShapeDtypeStruct((k, hi - lo), bf16) for lo, hi in bounds],
        compiler_params=_params("arbitrary"),
        name="cast_weight",
    )(w)
    return outs if splits else outs[0]


class _Riders:
    def __init__(self, riders, steps, step_of):
        self.args, self.in_specs, self.out_specs, self.out_shapes, self.bounds = [], [], [], [], []
        self.plan = []
        for w, layer, splits in riders:
            _, k, n = w.shape
            rows = 16
            while rows < k and (k % rows or k // rows > steps):
                rows *= 2
            carried = k % rows == 0 and k // rows <= steps
            self.plan.append((carried, w, layer, splits))
            if not carried:
                continue
            last = k // rows - 1

            def blk(*g, last=last):
                return jnp.minimum(step_of(*g), last)

            bounds = tuple(splits) if splits else ((0, n),)
            self.args.append(w)
            self.in_specs.append(pl.BlockSpec((None, rows, n), lambda *g, b=blk, l=layer: (l, b(*g), 0)))
            for lo, hi in bounds:
                self.out_specs.append(pl.BlockSpec((rows, hi - lo), lambda *g, b=blk: (b(*g), 0)))
                self.out_shapes.append(jax.ShapeDtypeStruct((k, hi - lo), bf16))
            self.bounds.append(bounds)

    def wrap(self, body, n_in, n_out):
        n_src = len(self.bounds)
        n_dst = len(self.out_specs)
        bounds = self.bounds

        def kern(*refs):
            ins, srcs = refs[:n_in], refs[n_in:n_in + n_src]
            o0 = n_in + n_src
            outs, dsts = refs[o0:o0 + n_out], iter(refs[o0 + n_out:o0 + n_out + n_dst])
            body(*ins, *outs, *refs[o0 + n_out + n_dst:])
            for src, bnd in zip(srcs, bounds):
                for lo, hi in bnd:
                    dst = next(dsts)
                    dst[...] = src[:, lo:hi].astype(dst.dtype)

        return kern

    def split(self, results, n_out):
        host, rest = results[:n_out], list(results[n_out:])
        per = []
        for carried, w, layer, splits in self.plan:
            if carried:
                count = len(splits) if splits else 1
                got, rest = rest[:count], rest[count:]
                per.append(got[0] if not splits else tuple(got))
            else:
                got = _cast_weight(w, layer, splits)
                per.append(tuple(got) if splits else got)
        return host, per


def _project(h, w, out_dtype, tn, *, tm_want=1024, h_buffers=2, col_scale=None, sigmoid=False,
             riders=(), name="proj"):
    rows, d = h.shape
    n = w.shape[1]
    nj = n // tn
    tm = _row_tile(rows, tm_want)
    single = dict(pipeline_mode=pl.Buffered(1)) if h_buffers == 1 else {}
    h_spec = pl.BlockSpec((tm, d), lambda i, j: (i, 0), **single)
    w_spec = pl.BlockSpec((d, tn), lambda i, j: (0, j))
    args, specs = [h, w], [h_spec, w_spec]
    if sigmoid:
        kern = _proj_sigmoid_kernel
    elif col_scale is not None:
        kern = _proj_scale_kernel
        args.append(col_scale)
        specs.append(pl.BlockSpec((1, tn), lambda i, j: (0, j)))
    else:
        kern = _proj_kernel
    ride = _Riders(riders, (rows // tm) * nj, lambda i, j: i * nj + j)
    res = pl.pallas_call(
        ride.wrap(kern, len(args), 1),
        grid=(rows // tm, nj),
        in_specs=specs + ride.in_specs,
        out_specs=[pl.BlockSpec((tm, tn), lambda i, j: (i, j))] + ride.out_specs,
        out_shape=[jax.ShapeDtypeStruct((rows, n), out_dtype)] + ride.out_shapes,
        compiler_params=_params("arbitrary", "arbitrary"),
        name=name,
    )(*args, *ride.args)
    (out,), extra = ride.split(res, 1)
    return out, extra


def _rope_rotate(x, cos, sin):
    n = x.shape[-1]
    lane = lax.broadcasted_iota(jnp.int32, x.shape, x.ndim - 1)
    up = pltpu.roll(x, n - 16, x.ndim - 1)
    down = pltpu.roll(x, 16, x.ndim - 1)
    return x * cos + jnp.where((lane % 32) < 16, up, down) * sin


def _proj_window_kernel(h_ref, w_ref, cos_ref, sin_ref, o_ref):
    acc = jnp.dot(h_ref[...], w_ref[...], preferred_element_type=f32)
    cos = cos_ref[...]
    sin = sin_ref[...]
    for j in range(C_PAIRS):
        x = _rope_rotate(acc[:, j * LANES:(j + 1) * LANES], cos, sin) * (C_DH ** -0.5)
        o_ref[:, j * LANES:(j + 1) * LANES] = x.astype(o_ref.dtype)
    k = _rope_rotate(acc[:, C_WIDTH:C_WIDTH + LANES], cos, sin)
    v = acc[:, C_WIDTH + LANES:C_WIDTH + 2 * LANES]
    low = lax.broadcasted_iota(jnp.int32, k.shape, 1) < C_DH
    for idx, x in enumerate((k, v)):
        swapped = pltpu.roll(x, C_DH, 1)
        groups = (jnp.where(low, x, 0.0), jnp.where(low, 0.0, swapped),
                  jnp.where(low, swapped, 0.0), jnp.where(low, 0.0, x))
        base = C_WIDTH + idx * C_EXP
        for c, val in enumerate(groups):
            o_ref[:, base + c * LANES:base + (c + 1) * LANES] = val.astype(o_ref.dtype)


def _project_window(h, w, cos, sin):
    rows, d = h.shape
    n = w.shape[1]
    tm = _row_tile(rows, 1024)
    tok_tiles = cos.shape[0] // tm
    t_spec = pl.BlockSpec((tm, LANES), lambda i: (i % tok_tiles, 0))
    return pl.pallas_call(
        _proj_window_kernel,
        grid=(rows // tm,),
        in_specs=[pl.BlockSpec((tm, d), lambda i: (i, 0)),
                  pl.BlockSpec((d, n), lambda i: (0, 0), pipeline_mode=pl.Buffered(1)),
                  t_spec, t_spec],
        out_specs=pl.BlockSpec((tm, C_OUT_WIDTH), lambda i: (i, 0)),
        out_shape=jax.ShapeDtypeStruct((rows, C_OUT_WIDTH), bf16),
        compiler_params=_params("arbitrary"),
        name="proj_window",
    )(h, w, cos, sin)


def _hgrn_kernel(*refs, rev, has_s0, emit_state, readout, hp):
    it = iter(refs)
    q_ref, v_ref, f_ref, lb_ref = next(it), next(it), next(it), next(it)
    s0_ref = next(it) if has_s0 else None
    if readout:
        g_ref, oprev_ref, ng_ref = next(it), next(it), next(it)
    o_ref = next(it)
    sout_ref = next(it) if emit_state else None
    st_ref = next(it)

    blk = pl.program_id(2)
    nblk = pl.num_programs(2)
    tb = q_ref.shape[0]
    nchunk = tb // A_CHUNK
    width = hp * A_DK
    heads = range(hp)

    def head(a, h):
        return a[:, h * A_DK:(h + 1) * A_DK]

    @pl.when(blk == 0)
    def _():
        if has_s0:
            st_ref[...] = s0_ref[...]
        else:
            st_ref[...] = jnp.zeros_like(st_ref)

    lb = lb_ref[...]
    f = lb + (1.0 - lb) * _sigmoid(f_ref[...])
    logf = jnp.log(f)
    k = 1.0 - f

    row = lax.broadcasted_iota(jnp.int32, (tb, tb), 0)
    col = lax.broadcasted_iota(jnp.int32, (tb, tb), 1)
    same = (row // A_CHUNK) == (col // A_CHUNK)
    causal = (col >= row) if rev else (col <= row)
    mask = same & causal
    tri = jnp.where(mask, 1.0, 0.0).astype(bf16)

    hi = logf.astype(bf16)
    lo = (logf - hi.astype(f32)).astype(bf16)
    cum2 = jnp.dot(tri, jnp.concatenate([hi, lo], axis=1), preferred_element_type=f32)
    cum = cum2[:, :width] + cum2[:, width:]

    q_dec = (_silu(q_ref[...]) * jnp.exp(cum)).astype(bf16)
    k_inv = (k * jnp.exp(-cum)).astype(bf16)
    v = v_ref[...].astype(bf16)
    att = [lax.dot_general(head(q_dec, h), head(k_inv, h), _NT, preferred_element_type=f32)
           for h in heads]
    att = [jnp.where(mask, a, 0.0).astype(bf16) for a in att]
    o_intra = [jnp.dot(att[h], head(v, h), preferred_element_type=f32) for h in heads]

    order = range(nchunk - 1, -1, -1) if rev else range(nchunk)
    k_end, dec = {}, {}
    for j in order:
        sl = slice(j * A_CHUNK, (j + 1) * A_CHUNK)
        last = j * A_CHUNK if rev else (j + 1) * A_CHUNK - 1
        tot = cum[last:last + 1, :]
        k_end[j] = (k[sl] * jnp.exp(tot - cum[sl])).astype(bf16)
        dec[j] = jnp.exp(tot)
    upd = {(h, j): lax.dot_general(head(v[j * A_CHUNK:(j + 1) * A_CHUNK], h), head(k_end[j], h), _TN,
                                   preferred_element_type=f32)
           for j in order for h in heads}

    before = {}
    final = []
    for h in heads:
        s = st_ref[h]
        for j in order:
            before[h, j] = s.astype(bf16)
            s = s * head(dec[j], h) + upd[h, j]
        st_ref[h] = s
        final.append(s)

    outs = []
    for h in heads:
        o_inter = [lax.dot_general(head(q_dec[j * A_CHUNK:(j + 1) * A_CHUNK], h), before[h, j], _NT,
                                   preferred_element_type=f32) for j in range(nchunk)]
        outs.append(o_intra[h] + jnp.concatenate(o_inter, axis=0))

    if readout:
        ng = ng_ref[...]
        normed = []
        for h in heads:
            o = outs[h] + head(oprev_ref[...], h)
            normed.append(o * lax.rsqrt(jnp.mean(o * o, axis=-1, keepdims=True) + EPS) * ng)
        o_ref[...] = (jnp.concatenate(normed, axis=1) * _silu(g_ref[...])).astype(o_ref.dtype)
    else:
        o_ref[...] = jnp.concatenate(outs, axis=1)

    if emit_state:
        @pl.when(blk == nblk - 1)
        def _():
            for h in heads:
                sout_ref[h] = final[h]


def _hgrn_scan(proj, lb_dir, seq, direction, *, s0=None, emit_state=False, readout=None):
    rows = proj.shape[0]
    batch = rows // seq
    tb = min(HGRN_BLOCK, seq)
    nblk = seq // tb
    rev = direction == 1
    hp = HGRN_HEADS_PER_STEP
    hblocks = A_HEADS // hp
    width = hp * A_DK

    def tok(b, h, i):
        return b * nblk + (nblk - 1 - i if rev else i)

    def col_spec(group):
        return pl.BlockSpec((tb, width), lambda b, h, i: (tok(b, h, i), group * hblocks + h))

    state_spec = pl.BlockSpec((None, hp, A_DV, A_DK), lambda b, h, i: (b, h, 0, 0))
    args = [proj, proj, proj, lb_dir.reshape(1, A_QK)]
    specs = [col_spec(0), col_spec(1), col_spec(2 + direction),
             pl.BlockSpec((1, width), lambda b, h, i: (0, h))]
    if s0 is not None:
        args.append(s0)
        specs.append(state_spec)
    if readout is not None:
        o_prev, norm_g = readout
        args += [proj, o_prev, norm_g.reshape(1, A_DV)]
        specs += [col_spec(4),
                  pl.BlockSpec((tb, width), lambda b, h, i: (tok(b, h, i), h)),
                  pl.BlockSpec((1, A_DV), lambda b, h, i: (0, 0))]
    out_shape = [jax.ShapeDtypeStruct((rows, A_WIDTH), bf16 if readout is not None else f32)]
    out_specs = [pl.BlockSpec((tb, width), lambda b, h, i: (tok(b, h, i), h))]
    if emit_state:
        out_shape.append(jax.ShapeDtypeStruct((batch, A_HEADS, A_DV, A_DK), f32))
        out_specs.append(state_spec)
    kern = functools.partial(_hgrn_kernel, rev=rev, has_s0=s0 is not None,
                             emit_state=emit_state, readout=readout is not None, hp=hp)
    res = pl.pallas_call(
        kern,
        grid=(batch, hblocks, nblk),
        in_specs=specs,
        out_specs=out_specs,
        out_shape=out_shape,
        scratch_shapes=[pltpu.VMEM((hp, A_DV, A_DK), f32)],
        compiler_params=_params("arbitrary", "arbitrary", "arbitrary"),
        name="hgrn_scan",
    )(*args)
    return res if emit_state else res[0]


def _hgrn_mixer(proj_lat, proj_ctx, lb, norm_g, seq, ctx_len, need_ctx):
    o_c_f, s_f = _hgrn_scan(proj_ctx, lb[0], ctx_len, 0, emit_state=True)
    o_l_f = _hgrn_scan(proj_lat, lb[0], seq, 0, s0=s_f)
    if need_ctx:
        a_ctx, s_b = _hgrn_scan(proj_ctx, lb[1], ctx_len, 1, emit_state=True, readout=(o_c_f, norm_g))
    else:
        _, s_b = _hgrn_scan(proj_ctx, lb[1], ctx_len, 1, emit_state=True)
        a_ctx = None
    a_lat = _hgrn_scan(proj_lat, lb[1], seq, 1, s0=s_b, readout=(o_l_f, norm_g))
    return a_lat, a_ctx


def _attend(parts, sink=None):
    m = parts[0][0].max(axis=-1, keepdims=True)
    for s, _ in parts[1:]:
        m = jnp.maximum(m, s.max(axis=-1, keepdims=True))
    if sink is not None:
        m = jnp.maximum(m, sink)
    den = None
    acc = None
    for s, v in parts:
        e = jnp.exp(s - m)
        d = e.sum(axis=-1, keepdims=True)
        o = jnp.dot(e.astype(bf16), v, preferred_element_type=f32)
        den = d if den is None else den + d
        acc = o if acc is None else acc + o
    if sink is not None:
        den = den + jnp.exp(sink - m)
    return acc / den


def _na_kernel(types_ref, q_ref, k_ref, v_ref, kc_ref, vc_ref, tbl_ref, o_ref, *, grid_rows, hp):
    del types_ref
    step = pl.program_id(2)
    span = NA_UNION * GRID_W
    first = jnp.clip(step * NA_QROWS - NA_ROWS // 2, 0, grid_rows - NA_UNION)
    start = pl.multiple_of(first * GRID_W, GRID_W)
    for h in range(hp):
        hs = slice(h * B_DH, (h + 1) * B_DH)
        q = q_ref[:, hs]
        kn = k_ref[pl.ds(start, span), hs]
        vn = v_ref[pl.ds(start, span), hs]
        s_nb = lax.dot_general(q, kn, _NT, preferred_element_type=f32) + tbl_ref[h]
        s_cx = lax.dot_general(q, kc_ref[:, hs], _NT, preferred_element_type=f32)
        o = _attend([(s_nb, vn), (s_cx, vc_ref[:, hs])])
        o_ref[:, hs] = o.astype(o_ref.dtype)


def _na_tables(rpb, grid_rows):
    assert grid_rows >= NA_UNION and grid_rows % NA_QROWS == 0
    col = np.arange(GRID_W)
    col_off = np.clip(col[None, :] - col[:, None] + NA_COLS - 1, 0, 2 * NA_COLS - 2)
    col_start = np.clip(col - NA_COLS // 2, 0, GRID_W - NA_COLS)
    col_ok = (col[None, :] >= col_start[:, None]) & (col[None, :] < col_start[:, None] + NA_COLS)
    seen, types = {}, []
    for i in range(grid_rows // NA_QROWS):
        first = int(np.clip(i * NA_QROWS - NA_ROWS // 2, 0, grid_rows - NA_UNION))
        key_row = first + np.arange(NA_UNION)[None, :]
        r = i * NA_QROWS + np.arange(NA_QROWS)[:, None]
        row_start = np.clip(r - NA_ROWS // 2, 0, grid_rows - NA_ROWS)
        ok = (key_row >= row_start) & (key_row < row_start + NA_ROWS)
        assert (ok.sum(axis=1) == NA_ROWS).all()
        off = np.where(ok, key_row - r + NA_ROWS - 1, 0)
        sig = (ok.tobytes(), off.tobytes())
        if sig not in seen:
            seen[sig] = (len(seen), ok, off)
        types.append(seen[sig][0])
    toe = jnp.where(col_ok[None, None], rpb.astype(f32)[:, :, col_off], NEG_INF)
    neg = jnp.full((rpb.shape[0], GRID_W, GRID_W), NEG_INF, f32)
    tables = []
    for _, ok, off in sorted(seen.values(), key=lambda t: t[0]):
        rows = [jnp.concatenate([toe[:, off[rl, a]] if ok[rl, a] else neg for a in range(NA_UNION)],
                                axis=-1) for rl in range(NA_QROWS)]
        tables.append(jnp.concatenate(rows, axis=-2))
    return jnp.stack(tables, axis=1), jnp.asarray(np.array(types, np.int32))


def _neighborhood_attention(qkv_lat, qkv_ctx, tables, types, seq, ctx_len):
    rows = qkv_lat.shape[0]
    batch = rows // seq
    tq = NA_QROWS * GRID_W
    nq = seq // tq
    hp = NA_HEADS_PER_STEP
    hb = B_HEADS // hp
    width = hp * B_DH
    kern = functools.partial(_na_kernel, grid_rows=seq // GRID_W, hp=hp)
    grid_spec = pltpu.PrefetchScalarGridSpec(
        num_scalar_prefetch=1,
        grid=(batch, hb, nq),
        in_specs=[
            pl.BlockSpec((tq, width), lambda b, h, i, t: (b * nq + i, h)),
            pl.BlockSpec((seq, width), lambda b, h, i, t: (b, hb + h)),
            pl.BlockSpec((seq, width), lambda b, h, i, t: (b, 2 * hb + h)),
            pl.BlockSpec((ctx_len, width), lambda b, h, i, t: (b, hb + h)),
            pl.BlockSpec((ctx_len, width), lambda b, h, i, t: (b, 2 * hb + h)),
            pl.BlockSpec((hp, None, tq, NA_UNION * GRID_W), lambda b, h, i, t: (h, t[i], 0, 0)),
        ],
        out_specs=pl.BlockSpec((tq, width), lambda b, h, i, t: (b * nq + i, h)),
    )
    return pl.pallas_call(
        kern,
        grid_spec=grid_spec,
        out_shape=jax.ShapeDtypeStruct((rows, B_WIDTH), bf16),
        compiler_params=_params("arbitrary", "arbitrary", "arbitrary"),
        name="neighborhood_attention",
    )(types, qkv_lat, qkv_lat, qkv_lat, qkv_ctx, qkv_ctx, tables)


def _window_kernel(sink_ref, q_ref, k_ref, v_ref, kc_ref, vc_ref, o_ref, *, seq):
    n = pl.program_id(1)
    span = 3 * C_BLOCK
    start = pl.multiple_of(jnp.clip((n - 1) * C_BLOCK, 0, seq - span), C_BLOCK)
    pairs = C_PAIRS // C_KV_HEADS
    rows = pairs * C_BLOCK
    qpos = n * C_BLOCK + lax.broadcasted_iota(jnp.int32, (rows, span), 0) % C_BLOCK
    kpos = start + lax.broadcasted_iota(jnp.int32, (rows, span), 1)
    valid = jnp.abs(qpos - kpos) <= C_WINDOW
    pair_id = lax.broadcasted_iota(jnp.int32, (rows, 1), 0) // C_BLOCK

    chains = [(g, e) for g in range(C_KV_HEADS) for e in range(2)]
    q = {g: jnp.concatenate([q_ref[:, (g * pairs + p) * LANES:(g * pairs + p + 1) * LANES]
                             for p in range(pairs)], axis=0) for g in range(C_KV_HEADS)}
    s_w, s_c, sink = {}, {}, {}
    for g, e in chains:
        cs = slice((2 * g + e) * LANES, (2 * g + e + 1) * LANES)
        s = lax.dot_general(q[g], k_ref[pl.ds(start, span), cs], _NT, preferred_element_type=f32)
        s_w[g, e] = jnp.where(valid, s, NEG_INF)
        s_c[g, e] = lax.dot_general(q[g], kc_ref[:, cs], _NT, preferred_element_type=f32)
        col = jnp.zeros((rows, 1), f32)
        for p in range(pairs):
            col = jnp.where(pair_id == p, sink_ref[(g * pairs + p) * 2 + e], col)
        sink[g, e] = col
    out = {}
    for g, e in chains:
        cs = slice((2 * g + e) * LANES, (2 * g + e + 1) * LANES)
        out[g, e] = _attend([(s_w[g, e], v_ref[pl.ds(start, span), cs]), (s_c[g, e], vc_ref[:, cs])],
                            sink=sink[g, e])
    for g in range(C_KV_HEADS):
        o = out[g, 0] + out[g, 1]
        for p in range(pairs):
            o_ref[:, (g * pairs + p) * LANES:(g * pairs + p + 1) * LANES] = (
                o[p * C_BLOCK:(p + 1) * C_BLOCK].astype(o_ref.dtype))


def _window_attention(qkv_lat, qkv_ctx, sink, seq, ctx_len):
    rows = qkv_lat.shape[0]
    batch = rows // seq
    nq = seq // C_BLOCK
    kblk = C_WIDTH // C_EXP
    kern = functools.partial(_window_kernel, seq=seq)
    resident = dict(pipeline_mode=pl.Buffered(1))
    return pl.pallas_call(
        kern,
        grid=(batch, nq),
        in_specs=[
            pl.BlockSpec(memory_space=pltpu.SMEM),
            pl.BlockSpec((C_BLOCK, C_WIDTH), lambda b, i: (b * nq + i, 0)),
            pl.BlockSpec((seq, C_EXP), lambda b, i: (b, kblk), **resident),
            pl.BlockSpec((seq, C_EXP), lambda b, i: (b, kblk + 1), **resident),
            pl.BlockSpec((ctx_len, C_EXP), lambda b, i: (b, kblk)),
            pl.BlockSpec((ctx_len, C_EXP), lambda b, i: (b, kblk + 1)),
        ],
        out_specs=pl.BlockSpec((C_BLOCK, C_WIDTH), lambda b, i: (b * nq + i, 0)),
        out_shape=jax.ShapeDtypeStruct((rows, C_WIDTH), bf16),
        compiler_params=_params("arbitrary", "arbitrary"),
        name="window_attention",
    )(sink.astype(f32), qkv_lat, qkv_lat, qkv_lat, qkv_ctx, qkv_ctx)


def _ctx_attn_kernel(*refs, heads, dh, k_cols, v_cols, has_sink):
    if has_sink:
        sink_ref, qkv_ref, o_ref = refs
    else:
        qkv_ref, o_ref = refs
    group = heads // len(k_cols)
    for kh, (kc, vc) in enumerate(zip(k_cols, v_cols)):
        k = qkv_ref[:, kc:kc + dh]
        v = qkv_ref[:, vc:vc + dh]
        for g in range(group):
            hq = kh * group + g
            q = qkv_ref[:, hq * dh:(hq + 1) * dh]
            s = lax.dot_general(q, k, _NT, preferred_element_type=f32)
            o = _attend([(s, v)], sink=sink_ref[hq] if has_sink else None)
            o_ref[:, hq * dh:(hq + 1) * dh] = o.astype(o_ref.dtype)


def _context_attention(qkv_ctx, ctx_len, heads, dh, k_cols, v_cols, sink=None):
    rows, width = qkv_ctx.shape
    kern = functools.partial(_ctx_attn_kernel, heads=heads, dh=dh, k_cols=k_cols, v_cols=v_cols,
                             has_sink=sink is not None)
    args, specs = [qkv_ctx], [pl.BlockSpec((ctx_len, width), lambda b: (b, 0))]
    if sink is not None:
        args.insert(0, sink.astype(f32))
        specs.insert(0, pl.BlockSpec(memory_space=pltpu.SMEM))
    return pl.pallas_call(
        kern,
        grid=(rows // ctx_len,),
        in_specs=specs,
        out_specs=pl.BlockSpec((ctx_len, heads * dh), lambda b: (b, 0)),
        out_shape=jax.ShapeDtypeStruct((rows, heads * dh), bf16),
        compiler_params=_params("arbitrary"),
        name="context_attention",
    )(*args)


def _merge_kernel(oa_ref, ob_ref, oc_ref, wa_ref, wb_ref, wc_ref, ga_ref, gb_ref, gc_ref, o_ref):
    br_a = jnp.dot(oa_ref[...], wa_ref[...], preferred_element_type=f32)
    br_b = jnp.dot(ob_ref[...], wb_ref[...], preferred_element_type=f32)
    br_c = jnp.dot(oc_ref[...], wc_ref[...], preferred_element_type=f32)
    m = ga_ref[...] * br_a + gb_ref[...] * br_b + gc_ref[...] * br_c
    o_ref[...] = m.astype(o_ref.dtype)


def _merge(o_a, o_b, o_c, gates, w_branch, riders=()):
    rows = o_a.shape[0]
    d = w_branch.shape[1]
    tm = _row_tile(rows, 1024)
    tn = 512
    nj = d // tn
    assert A_WIDTH % B_WIDTH == 0 and B_WIDTH == C_WIDTH
    b_blk = A_WIDTH // B_WIDTH
    ride = _Riders(riders, (rows // tm) * nj, lambda i, j: i * nj + j)
    res = pl.pallas_call(
        ride.wrap(_merge_kernel, 9, 1),
        grid=(rows // tm, nj),
        in_specs=[
            pl.BlockSpec((tm, A_WIDTH), lambda i, j: (i, 0)),
            pl.BlockSpec((tm, B_WIDTH), lambda i, j: (i, 0)),
            pl.BlockSpec((tm, C_WIDTH), lambda i, j: (i, 0)),
            pl.BlockSpec((A_WIDTH, tn), lambda i, j: (0, j)),
            pl.BlockSpec((B_WIDTH, tn), lambda i, j: (b_blk, j)),
            pl.BlockSpec((C_WIDTH, tn), lambda i, j: (b_blk + 1, j)),
            pl.BlockSpec((tm, tn), lambda i, j: (i, j)),
            pl.BlockSpec((tm, tn), lambda i, j: (i, nj + j)),
            pl.BlockSpec((tm, tn), lambda i, j: (i, 2 * nj + j)),
        ] + ride.in_specs,
        out_specs=[pl.BlockSpec((tm, tn), lambda i, j: (i, j))] + ride.out_specs,
        out_shape=[jax.ShapeDtypeStruct((rows, d), bf16)] + ride.out_shapes,
        compiler_params=_params("arbitrary", "arbitrary"),
        name="merge_branches",
    )(o_a, o_b, o_c, w_branch, w_branch, w_branch, gates, gates, gates, *ride.args)
    (out,), extra = ride.split(res, 1)
    return out, extra


def _residual_kernel(a_ref, w_ref, x_ref, gate_ref, o_ref):
    y = jnp.dot(a_ref[...], w_ref[...], preferred_element_type=f32)
    o_ref[...] = x_ref[...] + gate_ref[...] * y


def _gated_residual_matmul(a, w, x, mod, gate_idx, rows_per_group, tn, name):
    rows, kdim = a.shape
    d = w.shape[1]
    tm = _row_tile(rows_per_group, 1024)
    per = rows_per_group // tm
    nj = d // tn
    return pl.pallas_call(
        _residual_kernel,
        grid=(rows // tm, nj),
        in_specs=[
            pl.BlockSpec((tm, kdim), lambda i, j: (i, 0), pipeline_mode=pl.Buffered(1)),
            pl.BlockSpec((kdim, tn), lambda i, j: (0, j)),
            pl.BlockSpec((tm, tn), lambda i, j: (i, j)),
            pl.BlockSpec((None, 1, tn), lambda i, j: (i // per, 0, gate_idx * nj + j)),
        ],
        out_specs=pl.BlockSpec((tm, tn), lambda i, j: (i, j)),
        out_shape=jax.ShapeDtypeStruct((rows, d), f32),
        compiler_params=_params("arbitrary", "arbitrary"),
        name=name,
    )(a, w, x, mod)


def _swiglu_kernel(h_ref, wg_ref, wu_ref, o_ref):
    h = h_ref[...]
    g = jnp.dot(h, wg_ref[...], preferred_element_type=f32)
    u = jnp.dot(h, wu_ref[...], preferred_element_type=f32)
    o_ref[...] = (_silu(g) * u).astype(o_ref.dtype)


def _swiglu_up(h, w_gate, w_up, riders=()):
    rows, d = h.shape
    n = w_gate.shape[1]
    tm = _row_tile(rows, 1024)
    tn = FFN_TN
    nj = n // tn
    w_spec = pl.BlockSpec((d, tn), lambda i, j: (0, j))
    ride = _Riders(riders, (rows // tm) * nj, lambda i, j: i * nj + j)
    res = pl.pallas_call(
        ride.wrap(_swiglu_kernel, 3, 1),
        grid=(rows // tm, nj),
        in_specs=[pl.BlockSpec((tm, d), lambda i, j: (i, 0)), w_spec, w_spec] + ride.in_specs,
        out_specs=[pl.BlockSpec((tm, tn), lambda i, j: (i, j))] + ride.out_specs,
        out_shape=[jax.ShapeDtypeStruct((rows, n), bf16)] + ride.out_shapes,
        compiler_params=_params("arbitrary", "arbitrary"),
        name="swiglu_up",
    )(h, w_gate, w_up, *ride.args)
    (out,), extra = ride.split(res, 1)
    return out, extra


def _rope_tables(seq):
    half = C_DH // 2
    pos = jnp.arange(seq)
    inv = ROPE_BASE ** (-jnp.arange(0, half, 2, dtype=f32) / half)
    ang_row = (pos // GRID_W).astype(f32)[:, None] * inv[None, :]
    ang_col = (pos % GRID_W).astype(f32)[:, None] * inv[None, :]

    def one(ang):
        return (jnp.concatenate([jnp.cos(ang), jnp.cos(ang)], axis=-1),
                jnp.concatenate([-jnp.sin(ang), jnp.sin(ang)], axis=-1))

    cr, sr = one(ang_row)
    cc, sc = one(ang_col)
    cos = jnp.concatenate([cr, cc], axis=-1)
    sin = jnp.concatenate([sr, sc], axis=-1)
    reps = LANES // C_DH
    return jnp.tile(cos, (1, reps)), jnp.tile(sin, (1, reps))


def kernel(x, c, ctx, c_ctx, norm1_g, norm2_g, w_mod, b_mod, w_in, hgrn_lb, a_norm_g, na_rpb,
           c_sink, w_branch, w_out, w_ffn_gate, w_ffn_up, w_ffn_down, final_norm_g):
    batch, seq, d = x.shape
    ctx_len = ctx.shape[1]
    depth = w_in.shape[0]
    n_lat, n_ctx = batch * seq, batch * ctx_len

    lb_w = jax.nn.softmax(hgrn_lb.astype(f32), axis=0)
    lower_bounds = jnp.cumsum(lb_w, axis=0) - lb_w[:1]

    mod_rows = 8 * (-(-(batch + 1) // 8))
    c_rows = jnp.zeros((mod_rows, d), f32).at[:batch].set(c).at[batch].set(c_ctx)
    mod_all = _modulation(c_rows, w_mod, b_mod)

    cos_t, sin_t = _rope_tables(seq)
    ones_t = jnp.ones((n_ctx, LANES), f32)
    zeros_t = jnp.zeros((n_ctx, LANES), f32)

    a_hi = 3 * A_QK + 2 * A_WIDTH
    b_hi = a_hi + 3 * B_WIDTH
    c_hi = b_hi + C_WIDTH + 2 * C_KV_WIDTH
    in_width = w_in.shape[2]
    b_scale = jnp.concatenate([jnp.full((B_WIDTH,), B_DH ** -0.5, f32),
                               jnp.ones((2 * B_WIDTH,), f32)]).reshape(1, -1)
    ck_cols = tuple(C_WIDTH + 2 * g * LANES for g in range(C_KV_HEADS))
    cv_cols = tuple(C_WIDTH + C_EXP + 2 * g * LANES for g in range(C_KV_HEADS))
    bk_cols = tuple(B_WIDTH + h * B_DH for h in range(B_HEADS))
    bv_cols = tuple(2 * B_WIDTH + h * B_DH for h in range(B_HEADS))

    x_lat = x.reshape(n_lat, d)
    x_ctx = ctx.reshape(n_ctx, d)

    in_splits = ((0, a_hi), (a_hi, b_hi), (b_hi, c_hi), (c_hi, in_width))
    w_in_cast = _cast_weight(w_in, 0, in_splits)

    for l in range(depth):
        need_ctx = l < depth - 1
        mod_l = mod_all[l, :batch].reshape(batch, 1, N_MOD * d)
        mod_c = mod_all[l, batch:batch + 1].reshape(1, 1, N_MOD * d)
        w_a, w_b, w_c, w_g = w_in_cast
        na_tables, na_types = _na_tables(na_rpb[l], seq // GRID_W)

        h_lat = _norm_modulate(x_lat, norm1_g[l], mod_l, 0, seq)
        h_ctx = _norm_modulate(x_ctx, norm1_g[l], mod_c, 0, n_ctx)

        pa_lat, (wfu,) = _project(h_lat, w_a, f32, 1024, riders=[(w_ffn_up, l, None)], name="proj_hgrn")
        pb_lat, (wbr, wo) = _project(h_lat, w_b, bf16, 512, tm_want=2048, h_buffers=1, col_scale=b_scale,
                                     riders=[(w_branch, l, None), (w_out, l, None)], name="proj_na")
        pc_lat = _project_window(h_lat, w_c, cos_t, sin_t)
        g_lat, (wfg,) = _project(h_lat, w_g, bf16, 1024, sigmoid=True, riders=[(w_ffn_gate, l, None)],
                                 name="proj_gates")
        pa_ctx, _ = _project(h_ctx, w_a, f32, 1024, name="proj_hgrn")
        pb_ctx, _ = _project(h_ctx, w_b, bf16, 512, col_scale=b_scale, name="proj_na")
        pc_ctx = _project_window(h_ctx, w_c, ones_t, zeros_t)

        a_lat, a_ctx = _hgrn_mixer(pa_lat, pa_ctx, lower_bounds[l], a_norm_g[l], seq, ctx_len, need_ctx)
        b_lat = _neighborhood_attention(pb_lat, pb_ctx, na_tables, na_types, seq, ctx_len)
        c_lat = _window_attention(pc_lat, pc_ctx, c_sink[l], seq, ctx_len)

        m_lat, (wfd,) = _merge(a_lat, b_lat, c_lat, g_lat, wbr, riders=[(w_ffn_down, l, None)])
        x_lat = _gated_residual_matmul(m_lat, wo, x_lat, mod_l, 2, seq, 1024, "out_proj")
        h2 = _norm_modulate(x_lat, norm2_g[l], mod_l, 3, seq)
        next_in = [(w_in, l + 1, in_splits)] if l + 1 < depth else []
        u, nxt = _swiglu_up(h2, wfg, wfu, riders=next_in)
        if nxt:
            w_in_cast = nxt[0]
        x_lat = _gated_residual_matmul(u, wfd, x_lat, mod_l, 5, seq, FFN_TN, "ffn_down")

        if need_ctx:
            g_ctx, _ = _project(h_ctx, w_g, bf16, 1024, sigmoid=True, name="proj_gates")
            b_ctx = _context_attention(pb_ctx, ctx_len, B_HEADS, B_DH, bk_cols, bv_cols)
            c_ctx_o = _context_attention(pc_ctx, ctx_len, C_HEADS, C_DH, ck_cols, cv_cols, sink=c_sink[l])
            m_ctx, _ = _merge(a_ctx, b_ctx, c_ctx_o, g_ctx, wbr)
            x_ctx = _gated_residual_matmul(m_ctx, wo, x_ctx, mod_c, 2, n_ctx, 1024, "out_proj")
            h2c = _norm_modulate(x_ctx, norm2_g[l], mod_c, 3, n_ctx)
            uc, _ = _swiglu_up(h2c, wfg, wfu)
            x_ctx = _gated_residual_matmul(uc, wfd, x_ctx, mod_c, 5, n_ctx, FFN_TN, "ffn_down")

    return _final_norm(x_lat, final_norm_g).reshape(batch, seq, d)
```

```python
import functools

import jax
import jax.numpy as jnp
import numpy as np
from jax import lax
from jax.experimental import pallas as pl
from jax.experimental.pallas import tpu as pltpu

GRID_W = 64
EPS = 1e-6
NEG_INF = -1e30
N_MOD = 6
A_HEADS, A_DK, A_DV, A_CHUNK = 16, 128, 128, 32
A_QK = A_HEADS * A_DK
A_WIDTH = A_HEADS * A_DV
B_HEADS, B_DH = 8, 128
B_WIDTH = B_HEADS * B_DH
NA_ROWS, NA_COLS = 8, 16
C_HEADS, C_KV_HEADS, C_DH = 16, 2, 64
C_WIDTH = C_HEADS * C_DH
C_KV_WIDTH = C_KV_HEADS * C_DH
C_WINDOW = 128
C_BLOCK = 128
ROPE_BASE = 10000.0
N_BRANCH = 3

LANES = 128
VMEM_LIMIT_BYTES = 56 * 1024 * 1024

HGRN_BLOCK = 256
HGRN_HEADS_PER_STEP = 16
NA_QROWS = 4
NA_UNION = NA_ROWS + NA_QROWS
NA_HEADS_PER_STEP = 8
FFN_TN = 256
CAST_BLOCK_BYTES = 4 * 1024 * 1024

C_PAIRS = C_WIDTH // LANES
C_EXP = 2 * C_KV_HEADS * LANES
C_OUT_WIDTH = C_WIDTH + 2 * C_EXP

_NT = (((1,), (1,)), ((), ()))
_TN = (((0,), (0,)), ((), ()))

bf16 = jnp.bfloat16
f32 = jnp.float32


def _params(*sem):
    return pltpu.CompilerParams(dimension_semantics=sem, vmem_limit_bytes=VMEM_LIMIT_BYTES)


def _row_tile(rows, want):
    t = min(rows, want)
    while rows % t:
        t //= 2
    return t


def _sigmoid(x):
    return 1.0 / (1.0 + jnp.exp(-x))


def _silu(x):
    return x * _sigmoid(x)


def _mod_kernel(c_ref, w_ref, b_ref, o_ref):
    a = _silu(c_ref[...]).astype(bf16)
    o_ref[...] = jnp.dot(a, w_ref[...].astype(bf16), preferred_element_type=f32) + b_ref[...]


def _modulation(c_rows, w_mod, b_mod):
    depth, d, n = w_mod.shape
    rows = c_rows.shape[0]
    tn = 512
    return pl.pallas_call(
        _mod_kernel,
        grid=(depth, n // tn),
        in_specs=[
            pl.BlockSpec((rows, d), lambda l, j: (0, 0)),
            pl.BlockSpec((None, d, tn), lambda l, j: (l, 0, j)),
            pl.BlockSpec((None, 1, tn), lambda l, j: (l, 0, j)),
        ],
        out_specs=pl.BlockSpec((None, rows, tn), lambda l, j: (l, 0, j)),
        out_shape=jax.ShapeDtypeStruct((depth, rows, n), f32),
        compiler_params=_params("arbitrary", "arbitrary"),
        name="modulation",
    )(c_rows, w_mod, b_mod.reshape(depth, 1, n))


def _norm_mod_kernel(x_ref, g_ref, shift_ref, scale_ref, o_ref):
    x = x_ref[...]
    y = x * lax.rsqrt(jnp.mean(x * x, axis=-1, keepdims=True) + EPS) * g_ref[...]
    o_ref[...] = (y * (1.0 + scale_ref[...]) + shift_ref[...]).astype(o_ref.dtype)


def _norm_kernel(x_ref, g_ref, o_ref):
    x = x_ref[...]
    y = x * lax.rsqrt(jnp.mean(x * x, axis=-1, keepdims=True) + EPS) * g_ref[...]
    o_ref[...] = y.astype(o_ref.dtype)


def _norm_modulate(x, g, mod, shift_idx, rows_per_group):
    rows, d = x.shape
    tm = _row_tile(rows_per_group, 256)
    per = rows_per_group // tm
    return pl.pallas_call(
        _norm_mod_kernel,
        grid=(rows // tm,),
        in_specs=[
            pl.BlockSpec((tm, d), lambda i: (i, 0)),
            pl.BlockSpec((1, d), lambda i: (0, 0)),
            pl.BlockSpec((None, 1, d), lambda i: (i // per, 0, shift_idx)),
            pl.BlockSpec((None, 1, d), lambda i: (i // per, 0, shift_idx + 1)),
        ],
        out_specs=pl.BlockSpec((tm, d), lambda i: (i, 0)),
        out_shape=jax.ShapeDtypeStruct((rows, d), bf16),
        compiler_params=_params("arbitrary"),
        name="norm_modulate",
    )(x, g.reshape(1, d), mod, mod)


def _final_norm(x, g):
    rows, d = x.shape
    tm = _row_tile(rows, 256)
    return pl.pallas_call(
        _norm_kernel,
        grid=(rows // tm,),
        in_specs=[pl.BlockSpec((tm, d), lambda i: (i, 0)), pl.BlockSpec((1, d), lambda i: (0, 0))],
        out_specs=pl.BlockSpec((tm, d), lambda i: (i, 0)),
        out_shape=jax.ShapeDtypeStruct((rows, d), f32),
        compiler_params=_params("arbitrary"),
        name="final_norm",
    )(x, g.reshape(1, d))


def _proj_kernel(h_ref, w_ref, o_ref):
    o_ref[...] = jnp.dot(h_ref[...], w_ref[...], preferred_element_type=f32).astype(o_ref.dtype)


def _proj_scale_kernel(h_ref, w_ref, s_ref, o_ref):
    acc = jnp.dot(h_ref[...], w_ref[...], preferred_element_type=f32)
    o_ref[...] = (acc * s_ref[...]).astype(o_ref.dtype)


def _proj_sigmoid_kernel(h_ref, w_ref, o_ref):
    acc = jnp.dot(h_ref[...], w_ref[...], preferred_element_type=f32)
    o_ref[...] = _sigmoid(acc).astype(o_ref.dtype)


def _cast_kernel(w_ref, *o_refs, bounds):
    for o_ref, (lo, hi) in zip(o_refs, bounds):
        o_ref[...] = w_ref[:, lo:hi].astype(o_ref.dtype)


def _cast_weight(w, layer, splits=None):
    _, k, n = w.shape
    bounds = tuple(splits) if splits else ((0, n),)
    tk = 1 << ((CAST_BLOCK_BYTES // (4 * n)).bit_length() - 1)
    while k % tk:
        tk //= 2
    outs = pl.pallas_call(
        functools.partial(_cast_kernel, bounds=bounds),
        grid=(k // tk,),
        in_specs=[pl.BlockSpec((None, tk, n), lambda i: (layer, i, 0))],
        out_specs=[pl.BlockSpec((tk, hi - lo), lambda i: (i, 0)) for lo, hi in bounds],
        out_shape=[jax.ShapeDtypeStruct((k, hi - lo), bf16) for lo, hi in bounds],
        compiler_params=_params("arbitrary"),
        name="cast_weight",
    )(w)
    return outs if splits else outs[0]


class _Riders:
    def __init__(self, riders, steps, step_of):
        self.args, self.in_specs, self.out_specs, self.out_shapes, self.bounds = [], [], [], [], []
        self.plan = []
        for w, layer, splits in riders:
            _, k, n = w.shape
            rows = 16
            while rows < k and (k % rows or k // rows > steps):
                rows *= 2
            carried = k % rows == 0 and k // rows <= steps
            self.plan.append((carried, w, layer, splits))
            if not carried:
                continue
            last = k // rows - 1

            def blk(*g, last=last):
                return jnp.minimum(step_of(*g), last)

            bounds = tuple(splits) if splits else ((0, n),)
            self.args.append(w)
            self.in_specs.append(pl.BlockSpec((None, rows, n), lambda *g, b=blk, l=layer: (l, b(*g), 0)))
            for lo, hi in bounds:
                self.out_specs.append(pl.BlockSpec((rows, hi - lo), lambda *g, b=blk: (b(*g), 0)))
                self.out_shapes.append(jax.ShapeDtypeStruct((k, hi - lo), bf16))
            self.bounds.append(bounds)

    def wrap(self, body, n_in, n_out):
        n_src = len(self.bounds)
        n_dst = len(self.out_specs)
        bounds = self.bounds

        def kern(*refs):
            ins, srcs = refs[:n_in], refs[n_in:n_in + n_src]
            o0 = n_in + n_src
            outs, dsts = refs[o0:o0 + n_out], iter(refs[o0 + n_out:o0 + n_out + n_dst])
            body(*ins, *outs, *refs[o0 + n_out + n_dst:])
            for src, bnd in zip(srcs, bounds):
                for lo, hi in bnd:
                    dst = next(dsts)
                    dst[...] = src[:, lo:hi].astype(dst.dtype)

        return kern

    def split(self, results, n_out):
        host, rest = results[:n_out], list(results[n_out:])
        per = []
        for carried, w, layer, splits in self.plan:
            if carried:
                count = len(splits) if splits else 1
                got, rest = rest[:count], rest[count:]
                per.append(got[0] if not splits else tuple(got))
            else:
                got = _cast_weight(w, layer, splits)
                per.append(tuple(got) if splits else got)
        return host, per


def _project(h, w, out_dtype, tn, *, tm_want=1024, h_buffers=2, col_scale=None, sigmoid=False,
             riders=(), name="proj"):
    rows, d = h.shape
    n = w.shape[1]
    nj = n // tn
    tm = _row_tile(rows, tm_want)
    single = dict(pipeline_mode=pl.Buffered(1)) if h_buffers == 1 else {}
    h_spec = pl.BlockSpec((tm, d), lambda i, j: (i, 0), **single)
    w_spec = pl.BlockSpec((d, tn), lambda i, j: (0, j))
    args, specs = [h, w], [h_spec, w_spec]
    if sigmoid:
        kern = _proj_sigmoid_kernel
    elif col_scale is not None:
        kern = _proj_scale_kernel
        args.append(col_scale)
        specs.append(pl.BlockSpec((1, tn), lambda i, j: (0, j)))
    else:
        kern = _proj_kernel
    ride = _Riders(riders, (rows // tm) * nj, lambda i, j: i * nj + j)
    res = pl.pallas_call(
        ride.wrap(kern, len(args), 1),
        grid=(rows // tm, nj),
        in_specs=specs + ride.in_specs,
        out_specs=[pl.BlockSpec((tm, tn), lambda i, j: (i, j))] + ride.out_specs,
        out_shape=[jax.ShapeDtypeStruct((rows, n), out_dtype)] + ride.out_shapes,
        compiler_params=_params("arbitrary", "arbitrary"),
        name=name,
    )(*args, *ride.args)
    (out,), extra = ride.split(res, 1)
    return out, extra


def _rope_rotate(x, cos, sin):
    n = x.shape[-1]
    lane = lax.broadcasted_iota(jnp.int32, x.shape, x.ndim - 1)
    up = pltpu.roll(x, n - 16, x.ndim - 1)
    down = pltpu.roll(x, 16, x.ndim - 1)
    return x * cos + jnp.where((lane % 32) < 16, up, down) * sin


def _proj_window_kernel(h_ref, w_ref, cos_ref, sin_ref, o_ref):
    acc = jnp.dot(h_ref[...], w_ref[...], preferred_element_type=f32)
    cos = cos_ref[...]
    sin = sin_ref[...]
    for j in range(C_PAIRS):
        x = _rope_rotate(acc[:, j * LANES:(j + 1) * LANES], cos, sin) * (C_DH ** -0.5)
        o_ref[:, j * LANES:(j + 1) * LANES] = x.astype(o_ref.dtype)
    k = _rope_rotate(acc[:, C_WIDTH:C_WIDTH + LANES], cos, sin)
    v = acc[:, C_WIDTH + LANES:C_WIDTH + 2 * LANES]
    low = lax.broadcasted_iota(jnp.int32, k.shape, 1) < C_DH
    for idx, x in enumerate((k, v)):
        swapped = pltpu.roll(x, C_DH, 1)
        groups = (jnp.where(low, x, 0.0), jnp.where(low, 0.0, swapped),
                  jnp.where(low, swapped, 0.0), jnp.where(low, 0.0, x))
        base = C_WIDTH + idx * C_EXP
        for c, val in enumerate(groups):
            o_ref[:, base + c * LANES:base + (c + 1) * LANES] = val.astype(o_ref.dtype)


def _project_window(h, w, cos, sin):
    rows, d = h.shape
    n = w.shape[1]
    tm = _row_tile(rows, 1024)
    tok_tiles = cos.shape[0] // tm
    t_spec = pl.BlockSpec((tm, LANES), lambda i: (i % tok_tiles, 0))
    return pl.pallas_call(
        _proj_window_kernel,
        grid=(rows // tm,),
        in_specs=[pl.BlockSpec((tm, d), lambda i: (i, 0)),
                  pl.BlockSpec((d, n), lambda i: (0, 0), pipeline_mode=pl.Buffered(1)),
                  t_spec, t_spec],
        out_specs=pl.BlockSpec((tm, C_OUT_WIDTH), lambda i: (i, 0)),
        out_shape=jax.ShapeDtypeStruct((rows, C_OUT_WIDTH), bf16),
        compiler_params=_params("arbitrary"),
        name="proj_window",
    )(h, w, cos, sin)


def _hgrn_kernel(*refs, rev, has_s0, emit_state, readout, hp):
    it = iter(refs)
    q_ref, v_ref, f_ref, lb_ref = next(it), next(it), next(it), next(it)
    s0_ref = next(it) if has_s0 else None
    if readout:
        g_ref, oprev_ref, ng_ref = next(it), next(it), next(it)
    o_ref = next(it)
    sout_ref = next(it) if emit_state else None
    st_ref = next(it)

    blk = pl.program_id(2)
    nblk = pl.num_programs(2)
    tb = q_ref.shape[0]
    nchunk = tb // A_CHUNK
    width = hp * A_DK
    heads = range(hp)

    def head(a, h):
        return a[:, h * A_DK:(h + 1) * A_DK]

    @pl.when(blk == 0)
    def _():
        if has_s0:
            st_ref[...] = s0_ref[...]
        else:
            st_ref[...] = jnp.zeros_like(st_ref)

    lb = lb_ref[...]
    f = lb + (1.0 - lb) * _sigmoid(f_ref[...])
    logf = jnp.log(f)
    k = 1.0 - f

    row = lax.broadcasted_iota(jnp.int32, (tb, tb), 0)
    col = lax.broadcasted_iota(jnp.int32, (tb, tb), 1)
    same = (row // A_CHUNK) == (col // A_CHUNK)
    causal = (col >= row) if rev else (col <= row)
    mask = same & causal
    tri = jnp.where(mask, 1.0, 0.0).astype(bf16)

    hi = logf.astype(bf16)
    lo = (logf - hi.astype(f32)).astype(bf16)
    cum2 = jnp.dot(tri, jnp.concatenate([hi, lo], axis=1), preferred_element_type=f32)
    cum = cum2[:, :width] + cum2[:, width:]

    q_dec = (_silu(q_ref[...]) * jnp.exp(cum)).astype(bf16)
    k_inv = (k * jnp.exp(-cum)).astype(bf16)
    v = v_ref[...].astype(bf16)
    att = [lax.dot_general(head(q_dec, h), head(k_inv, h), _NT, preferred_element_type=f32)
           for h in heads]
    att = [jnp.where(mask, a, 0.0).astype(bf16) for a in att]
    o_intra = [jnp.dot(att[h], head(v, h), preferred_element_type=f32) for h in heads]

    order = range(nchunk - 1, -1, -1) if rev else range(nchunk)
    k_end, dec = {}, {}
    for j in order:
        sl = slice(j * A_CHUNK, (j + 1) * A_CHUNK)
        last = j * A_CHUNK if rev else (j + 1) * A_CHUNK - 1
        tot = cum[last:last + 1, :]
        k_end[j] = (k[sl] * jnp.exp(tot - cum[sl])).astype(bf16)
        dec[j] = jnp.exp(tot)
    zero = jnp.zeros((A_CHUNK, A_DK), bf16)
    upd = {}
    for h in heads:
        k_diag = jnp.concatenate(
            [jnp.concatenate([head(k_end[j], h) if c == j else zero for c in range(nchunk)], axis=1)
             for j in range(nchunk)], axis=0)
        u_all = lax.dot_general(head(v, h), k_diag, _TN, preferred_element_type=f32)
        for j in range(nchunk):
            upd[h, j] = u_all[:, j * A_DK:(j + 1) * A_DK]

    before = {}
    final = []
    for h in heads:
        s = st_ref[h]
        for j in order:
            before[h, j] = s.astype(bf16)
            s = s * head(dec[j], h) + upd[h, j]
        st_ref[h] = s
        final.append(s)

    outs = []
    for h in heads:
        o_inter = [lax.dot_general(head(q_dec[j * A_CHUNK:(j + 1) * A_CHUNK], h), before[h, j], _NT,
                                   preferred_element_type=f32) for j in range(nchunk)]
        outs.append(o_intra[h] + jnp.concatenate(o_inter, axis=0))

    if readout:
        ng = ng_ref[...]
        normed = []
        for h in heads:
            o = outs[h] + head(oprev_ref[...], h)
            normed.append(o * lax.rsqrt(jnp.mean(o * o, axis=-1, keepdims=True) + EPS) * ng)
        o_ref[...] = (jnp.concatenate(normed, axis=1) * _silu(g_ref[...])).astype(o_ref.dtype)
    else:
        o_ref[...] = jnp.concatenate(outs, axis=1)

    if emit_state:
        @pl.when(blk == nblk - 1)
        def _():
            for h in heads:
                sout_ref[h] = final[h]


def _hgrn_scan(proj, lb_dir, seq, direction, *, s0=None, emit_state=False, readout=None):
    rows = proj.shape[0]
    batch = rows // seq
    tb = min(HGRN_BLOCK, seq)
    nblk = seq // tb
    rev = direction == 1
    hp = HGRN_HEADS_PER_STEP
    hblocks = A_HEADS // hp
    width = hp * A_DK

    def tok(b, h, i):
        return b * nblk + (nblk - 1 - i if rev else i)

    def col_spec(group):
        return pl.BlockSpec((tb, width), lambda b, h, i: (tok(b, h, i), group * hblocks + h))

    state_spec = pl.BlockSpec((None, hp, A_DV, A_DK), lambda b, h, i: (b, h, 0, 0))
    args = [proj, proj, proj, lb_dir.reshape(1, A_QK)]
    specs = [col_spec(0), col_spec(1), col_spec(2 + direction),
             pl.BlockSpec((1, width), lambda b, h, i: (0, h))]
    if s0 is not None:
        args.append(s0)
        specs.append(state_spec)
    if readout is not None:
        o_prev, norm_g = readout
        args += [proj, o_prev, norm_g.reshape(1, A_DV)]
        specs += [col_spec(4),
                  pl.BlockSpec((tb, width), lambda b, h, i: (tok(b, h, i), h)),
                  pl.BlockSpec((1, A_DV), lambda b, h, i: (0, 0))]
    out_shape = [jax.ShapeDtypeStruct((rows, A_WIDTH), bf16 if readout is not None else f32)]
    out_specs = [pl.BlockSpec((tb, width), lambda b, h, i: (tok(b, h, i), h))]
    if emit_state:
        out_shape.append(jax.ShapeDtypeStruct((batch, A_HEADS, A_DV, A_DK), f32))
        out_specs.append(state_spec)
    kern = functools.partial(_hgrn_kernel, rev=rev, has_s0=s0 is not None,
                             emit_state=emit_state, readout=readout is not None, hp=hp)
    res = pl.pallas_call(
        kern,
        grid=(batch, hblocks, nblk),
        in_specs=specs,
        out_specs=out_specs,
        out_shape=out_shape,
        scratch_shapes=[pltpu.VMEM((hp, A_DV, A_DK), f32)],
        compiler_params=_params("arbitrary", "arbitrary", "arbitrary"),
        name="hgrn_scan",
    )(*args)
    return res if emit_state else res[0]


def _hgrn_mixer(proj_lat, proj_ctx, lb, norm_g, seq, ctx_len, need_ctx):
    o_c_f, s_f = _hgrn_scan(proj_ctx, lb[0], ctx_len, 0, emit_state=True)
    o_l_f = _hgrn_scan(proj_lat, lb[0], seq, 0, s0=s_f)
    if need_ctx:
        a_ctx, s_b = _hgrn_scan(proj_ctx, lb[1], ctx_len, 1, emit_state=True, readout=(o_c_f, norm_g))
    else:
        _, s_b = _hgrn_scan(proj_ctx, lb[1], ctx_len, 1, emit_state=True)
        a_ctx = None
    a_lat = _hgrn_scan(proj_lat, lb[1], seq, 1, s0=s_b, readout=(o_l_f, norm_g))
    return a_lat, a_ctx


def _attend(parts, sink=None):
    m = parts[0][0].max(axis=-1, keepdims=True)
    for s, _ in parts[1:]:
        m = jnp.maximum(m, s.max(axis=-1, keepdims=True))
    if sink is not None:
        m = jnp.maximum(m, sink)
    den = None
    acc = None
    for s, v in parts:
        e = jnp.exp(s - m)
        d = e.sum(axis=-1, keepdims=True)
        o = jnp.dot(e.astype(bf16), v, preferred_element_type=f32)
        den = d if den is None else den + d
        acc = o if acc is None else acc + o
    if sink is not None:
        den = den + jnp.exp(sink - m)
    return acc / den


def _na_kernel(types_ref, q_ref, k_ref, v_ref, kc_ref, vc_ref, tbl_ref, o_ref, *, grid_rows, hp):
    del types_ref
    step = pl.program_id(2)
    span = NA_UNION * GRID_W
    first = jnp.clip(step * NA_QROWS - NA_ROWS // 2, 0, grid_rows - NA_UNION)
    start = pl.multiple_of(first * GRID_W, GRID_W)
    for h in range(hp):
        hs = slice(h * B_DH, (h + 1) * B_DH)
        q = q_ref[:, hs]
        kn = k_ref[pl.ds(start, span), hs]
        vn = v_ref[pl.ds(start, span), hs]
        s_nb = lax.dot_general(q, kn, _NT, preferred_element_type=f32) + tbl_ref[h]
        s_cx = lax.dot_general(q, kc_ref[:, hs], _NT, preferred_element_type=f32)
        o = _attend([(s_nb, vn), (s_cx, vc_ref[:, hs])])
        o_ref[:, hs] = o.astype(o_ref.dtype)


def _na_tables(rpb, grid_rows):
    assert grid_rows >= NA_UNION and grid_rows % NA_QROWS == 0
    col = np.arange(GRID_W)
    col_off = np.clip(col[None, :] - col[:, None] + NA_COLS - 1, 0, 2 * NA_COLS - 2)
    col_start = np.clip(col - NA_COLS // 2, 0, GRID_W - NA_COLS)
    col_ok = (col[None, :] >= col_start[:, None]) & (col[None, :] < col_start[:, None] + NA_COLS)
    seen, types = {}, []
    for i in range(grid_rows // NA_QROWS):
        first = int(np.clip(i * NA_QROWS - NA_ROWS // 2, 0, grid_rows - NA_UNION))
        key_row = first + np.arange(NA_UNION)[None, :]
        r = i * NA_QROWS + np.arange(NA_QROWS)[:, None]
        row_start = np.clip(r - NA_ROWS // 2, 0, grid_rows - NA_ROWS)
        ok = (key_row >= row_start) & (key_row < row_start + NA_ROWS)
        assert (ok.sum(axis=1) == NA_ROWS).all()
        off = np.where(ok, key_row - r + NA_ROWS - 1, 0)
        sig = (ok.tobytes(), off.tobytes())
        if sig not in seen:
            seen[sig] = (len(seen), ok, off)
        types.append(seen[sig][0])
    toe = jnp.where(col_ok[None, None], rpb.astype(f32)[:, :, col_off], NEG_INF)
    neg = jnp.full((rpb.shape[0], GRID_W, GRID_W), NEG_INF, f32)
    tables = []
    for _, ok, off in sorted(seen.values(), key=lambda t: t[0]):
        rows = [jnp.concatenate([toe[:, off[rl, a]] if ok[rl, a] else neg for a in range(NA_UNION)],
                                axis=-1) for rl in range(NA_QROWS)]
        tables.append(jnp.concatenate(rows, axis=-2))
    return jnp.stack(tables, axis=1), jnp.asarray(np.array(types, np.int32))


def _neighborhood_attention(qkv_lat, qkv_ctx, tables, types, seq, ctx_len):
    rows = qkv_lat.shape[0]
    batch = rows // seq
    tq = NA_QROWS * GRID_W
    nq = seq // tq
    hp = NA_HEADS_PER_STEP
    hb = B_HEADS // hp
    width = hp * B_DH
    kern = functools.partial(_na_kernel, grid_rows=seq // GRID_W, hp=hp)
    grid_spec = pltpu.PrefetchScalarGridSpec(
        num_scalar_prefetch=1,
        grid=(batch, hb, nq),
        in_specs=[
            pl.BlockSpec((tq, width), lambda b, h, i, t: (b * nq + i, h)),
            pl.BlockSpec((seq, width), lambda b, h, i, t: (b, hb + h), pipeline_mode=pl.Buffered(1)),
            pl.BlockSpec((seq, width), lambda b, h, i, t: (b, 2 * hb + h), pipeline_mode=pl.Buffered(1)),
            pl.BlockSpec((ctx_len, width), lambda b, h, i, t: (b, hb + h)),
            pl.BlockSpec((ctx_len, width), lambda b, h, i, t: (b, 2 * hb + h)),
            pl.BlockSpec((hp, None, tq, NA_UNION * GRID_W), lambda b, h, i, t: (h, t[i], 0, 0)),
        ],
        out_specs=pl.BlockSpec((tq, width), lambda b, h, i, t: (b * nq + i, h)),
    )
    return pl.pallas_call(
        kern,
        grid_spec=grid_spec,
        out_shape=jax.ShapeDtypeStruct((rows, B_WIDTH), bf16),
        compiler_params=_params("arbitrary", "arbitrary", "arbitrary"),
        name="neighborhood_attention",
    )(types, qkv_lat, qkv_lat, qkv_lat, qkv_ctx, qkv_ctx, tables)


def _window_kernel(sink_ref, q_ref, k_ref, v_ref, kc_ref, vc_ref, o_ref, *, seq):
    n = pl.program_id(1)
    span = 3 * C_BLOCK
    start = pl.multiple_of(jnp.clip((n - 1) * C_BLOCK, 0, seq - span), C_BLOCK)
    pairs = C_PAIRS // C_KV_HEADS
    rows = pairs * C_BLOCK
    qpos = n * C_BLOCK + lax.broadcasted_iota(jnp.int32, (rows, span), 0) % C_BLOCK
    kpos = start + lax.broadcasted_iota(jnp.int32, (rows, span), 1)
    valid = jnp.abs(qpos - kpos) <= C_WINDOW
    pair_id = lax.broadcasted_iota(jnp.int32, (rows, 1), 0) // C_BLOCK

    chains = [(g, e) for g in range(C_KV_HEADS) for e in range(2)]
    q = {g: jnp.concatenate([q_ref[:, (g * pairs + p) * LANES:(g * pairs + p + 1) * LANES]
                             for p in range(pairs)], axis=0) for g in range(C_KV_HEADS)}
    s_w, s_c, sink = {}, {}, {}
    for g, e in chains:
        cs = slice((2 * g + e) * LANES, (2 * g + e + 1) * LANES)
        s = lax.dot_general(q[g], k_ref[pl.ds(start, span), cs], _NT, preferred_element_type=f32)
        s_w[g, e] = jnp.where(valid, s, NEG_INF)
        s_c[g, e] = lax.dot_general(q[g], kc_ref[:, cs], _NT, preferred_element_type=f32)
        col = jnp.zeros((rows, 1), f32)
        for p in range(pairs):
            col = jnp.where(pair_id == p, sink_ref[(g * pairs + p) * 2 + e], col)
        sink[g, e] = col
    out = {}
    for g, e in chains:
        cs = slice((2 * g + e) * LANES, (2 * g + e + 1) * LANES)
        out[g, e] = _attend([(s_w[g, e], v_ref[pl.ds(start, span), cs]), (s_c[g, e], vc_ref[:, cs])],
                            sink=sink[g, e])
    for g in range(C_KV_HEADS):
        o = out[g, 0] + out[g, 1]
        for p in range(pairs):
            o_ref[:, (g * pairs + p) * LANES:(g * pairs + p + 1) * LANES] = (
                o[p * C_BLOCK:(p + 1) * C_BLOCK].astype(o_ref.dtype))


def _window_attention(qkv_lat, qkv_ctx, sink, seq, ctx_len):
    rows = qkv_lat.shape[0]
    batch = rows // seq
    nq = seq // C_BLOCK
    kblk = C_WIDTH // C_EXP
    kern = functools.partial(_window_kernel, seq=seq)
    resident = dict(pipeline_mode=pl.Buffered(1))
    return pl.pallas_call(
        kern,
        grid=(batch, nq),
        in_specs=[
            pl.BlockSpec(memory_space=pltpu.SMEM),
            pl.BlockSpec((C_BLOCK, C_WIDTH), lambda b, i: (b * nq + i, 0)),
            pl.BlockSpec((seq, C_EXP), lambda b, i: (b, kblk), **resident),
            pl.BlockSpec((seq, C_EXP), lambda b, i: (b, kblk + 1), **resident),
            pl.BlockSpec((ctx_len, C_EXP), lambda b, i: (b, kblk)),
            pl.BlockSpec((ctx_len, C_EXP), lambda b, i: (b, kblk + 1)),
        ],
        out_specs=pl.BlockSpec((C_BLOCK, C_WIDTH), lambda b, i: (b * nq + i, 0)),
        out_shape=jax.ShapeDtypeStruct((rows, C_WIDTH), bf16),
        compiler_params=_params("arbitrary", "arbitrary"),
        name="window_attention",
    )(sink.astype(f32), qkv_lat, qkv_lat, qkv_lat, qkv_ctx, qkv_ctx)


def _ctx_attn_kernel(*refs, heads, dh, k_cols, v_cols, has_sink):
    if has_sink:
        sink_ref, qkv_ref, o_ref = refs
    else:
        qkv_ref, o_ref = refs
    group = heads // len(k_cols)
    for kh, (kc, vc) in enumerate(zip(k_cols, v_cols)):
        k = qkv_ref[:, kc:kc + dh]
        v = qkv_ref[:, vc:vc + dh]
        for g in range(group):
            hq = kh * group + g
            q = qkv_ref[:, hq * dh:(hq + 1) * dh]
            s = lax.dot_general(q, k, _NT, preferred_element_type=f32)
            o = _attend([(s, v)], sink=sink_ref[hq] if has_sink else None)
            o_ref[:, hq * dh:(hq + 1) * dh] = o.astype(o_ref.dtype)


def _context_attention(qkv_ctx, ctx_len, heads, dh, k_cols, v_cols, sink=None):
    rows, width = qkv_ctx.shape
    kern = functools.partial(_ctx_attn_kernel, heads=heads, dh=dh, k_cols=k_cols, v_cols=v_cols,
                             has_sink=sink is not None)
    args, specs = [qkv_ctx], [pl.BlockSpec((ctx_len, width), lambda b: (b, 0))]
    if sink is not None:
        args.insert(0, sink.astype(f32))
        specs.insert(0, pl.BlockSpec(memory_space=pltpu.SMEM))
    return pl.pallas_call(
        kern,
        grid=(rows // ctx_len,),
        in_specs=specs,
        out_specs=pl.BlockSpec((ctx_len, heads * dh), lambda b: (b, 0)),
        out_shape=jax.ShapeDtypeStruct((rows, heads * dh), bf16),
        compiler_params=_params("arbitrary"),
        name="context_attention",
    )(*args)


def _merge_kernel(oa_ref, ob_ref, oc_ref, wa_ref, wb_ref, wc_ref, ga_ref, gb_ref, gc_ref, o_ref):
    br_a = jnp.dot(oa_ref[...], wa_ref[...], preferred_element_type=f32)
    br_b = jnp.dot(ob_ref[...], wb_ref[...], preferred_element_type=f32)
    br_c = jnp.dot(oc_ref[...], wc_ref[...], preferred_element_type=f32)
    m = ga_ref[...] * br_a + gb_ref[...] * br_b + gc_ref[...] * br_c
    o_ref[...] = m.astype(o_ref.dtype)


def _merge(o_a, o_b, o_c, gates, w_branch, riders=()):
    rows = o_a.shape[0]
    d = w_branch.shape[1]
    tm = _row_tile(rows, 1024)
    tn = 512
    nj = d // tn
    assert A_WIDTH % B_WIDTH == 0 and B_WIDTH == C_WIDTH
    b_blk = A_WIDTH // B_WIDTH
    ride = _Riders(riders, (rows // tm) * nj, lambda i, j: i * nj + j)
    res = pl.pallas_call(
        ride.wrap(_merge_kernel, 9, 1),
        grid=(rows // tm, nj),
        in_specs=[
            pl.BlockSpec((tm, A_WIDTH), lambda i, j: (i, 0)),
            pl.BlockSpec((tm, B_WIDTH), lambda i, j: (i, 0)),
            pl.BlockSpec((tm, C_WIDTH), lambda i, j: (i, 0)),
            pl.BlockSpec((A_WIDTH, tn), lambda i, j: (0, j)),
            pl.BlockSpec((B_WIDTH, tn), lambda i, j: (b_blk, j)),
            pl.BlockSpec((C_WIDTH, tn), lambda i, j: (b_blk + 1, j)),
            pl.BlockSpec((tm, tn), lambda i, j: (i, j)),
            pl.BlockSpec((tm, tn), lambda i, j: (i, nj + j)),
            pl.BlockSpec((tm, tn), lambda i, j: (i, 2 * nj + j)),
        ] + ride.in_specs,
        out_specs=[pl.BlockSpec((tm, tn), lambda i, j: (i, j))] + ride.out_specs,
        out_shape=[jax.ShapeDtypeStruct((rows, d), bf16)] + ride.out_shapes,
        compiler_params=_params("arbitrary", "arbitrary"),
        name="merge_branches",
    )(o_a, o_b, o_c, w_branch, w_branch, w_branch, gates, gates, gates, *ride.args)
    (out,), extra = ride.split(res, 1)
    return out, extra


def _residual_kernel(a_ref, w_ref, x_ref, gate_ref, o_ref):
    y = jnp.dot(a_ref[...], w_ref[...], preferred_element_type=f32)
    o_ref[...] = x_ref[...] + gate_ref[...] * y


def _gated_residual_matmul(a, w, x, mod, gate_idx, rows_per_group, tn, name, a_buffers=1):
    rows, kdim = a.shape
    d = w.shape[1]
    tm = _row_tile(rows_per_group, 1024)
    per = rows_per_group // tm
    nj = d // tn
    single = dict(pipeline_mode=pl.Buffered(1)) if a_buffers == 1 else {}
    return pl.pallas_call(
        _residual_kernel,
        grid=(rows // tm, nj),
        in_specs=[
            pl.BlockSpec((tm, kdim), lambda i, j: (i, 0), **single),
            pl.BlockSpec((kdim, tn), lambda i, j: (0, j)),
            pl.BlockSpec((tm, tn), lambda i, j: (i, j)),
            pl.BlockSpec((None, 1, tn), lambda i, j: (i // per, 0, gate_idx * nj + j)),
        ],
        out_specs=pl.BlockSpec((tm, tn), lambda i, j: (i, j)),
        out_shape=jax.ShapeDtypeStruct((rows, d), f32),
        compiler_params=_params("arbitrary", "arbitrary"),
        name=name,
    )(a, w, x, mod)


def _swiglu_kernel(h_ref, wg_ref, wu_ref, o_ref):
    h = h_ref[...]
    g = jnp.dot(h, wg_ref[...], preferred_element_type=f32)
    u = jnp.dot(h, wu_ref[...], preferred_element_type=f32)
    o_ref[...] = (_silu(g) * u).astype(o_ref.dtype)


def _swiglu_up(h, w_gate, w_up, riders=()):
    rows, d = h.shape
    n = w_gate.shape[1]
    tm = _row_tile(rows, 1024)
    tn = FFN_TN
    nj = n // tn
    w_spec = pl.BlockSpec((d, tn), lambda i, j: (0, j))
    ride = _Riders(riders, (rows // tm) * nj, lambda i, j: i * nj + j)
    res = pl.pallas_call(
        ride.wrap(_swiglu_kernel, 3, 1),
        grid=(rows // tm, nj),
        in_specs=[pl.BlockSpec((tm, d), lambda i, j: (i, 0)), w_spec, w_spec] + ride.in_specs,
        out_specs=[pl.BlockSpec((tm, tn), lambda i, j: (i, j))] + ride.out_specs,
        out_shape=[jax.ShapeDtypeStruct((rows, n), bf16)] + ride.out_shapes,
        compiler_params=_params("arbitrary", "arbitrary"),
        name="swiglu_up",
    )(h, w_gate, w_up, *ride.args)
    (out,), extra = ride.split(res, 1)
    return out, extra


def _rope_tables(seq):
    half = C_DH // 2
    pos = jnp.arange(seq)
    inv = ROPE_BASE ** (-jnp.arange(0, half, 2, dtype=f32) / half)
    ang_row = (pos // GRID_W).astype(f32)[:, None] * inv[None, :]
    ang_col = (pos % GRID_W).astype(f32)[:, None] * inv[None, :]

    def one(ang):
        return (jnp.concatenate([jnp.cos(ang), jnp.cos(ang)], axis=-1),
                jnp.concatenate([-jnp.sin(ang), jnp.sin(ang)], axis=-1))

    cr, sr = one(ang_row)
    cc, sc = one(ang_col)
    cos = jnp.concatenate([cr, cc], axis=-1)
    sin = jnp.concatenate([sr, sc], axis=-1)
    reps = LANES // C_DH
    return jnp.tile(cos, (1, reps)), jnp.tile(sin, (1, reps))


def kernel(x, c, ctx, c_ctx, norm1_g, norm2_g, w_mod, b_mod, w_in, hgrn_lb, a_norm_g, na_rpb,
           c_sink, w_branch, w_out, w_ffn_gate, w_ffn_up, w_ffn_down, final_norm_g):
    batch, seq, d = x.shape
    ctx_len = ctx.shape[1]
    depth = w_in.shape[0]
    n_lat, n_ctx = batch * seq, batch * ctx_len

    lb_w = jax.nn.softmax(hgrn_lb.astype(f32), axis=0)
    lower_bounds = jnp.cumsum(lb_w, axis=0) - lb_w[:1]

    mod_rows = 8 * (-(-(batch + 1) // 8))
    c_rows = jnp.zeros((mod_rows, d), f32).at[:batch].set(c).at[batch].set(c_ctx)
    mod_all = _modulation(c_rows, w_mod, b_mod)

    cos_t, sin_t = _rope_tables(seq)
    ones_t = jnp.ones((n_ctx, LANES), f32)
    zeros_t = jnp.zeros((n_ctx, LANES), f32)

    a_hi = 3 * A_QK + 2 * A_WIDTH
    b_hi = a_hi + 3 * B_WIDTH
    c_hi = b_hi + C_WIDTH + 2 * C_KV_WIDTH
    in_width = w_in.shape[2]
    b_scale = jnp.concatenate([jnp.full((B_WIDTH,), B_DH ** -0.5, f32),
                               jnp.ones((2 * B_WIDTH,), f32)]).reshape(1, -1)
    ck_cols = tuple(C_WIDTH + 2 * g * LANES for g in range(C_KV_HEADS))
    cv_cols = tuple(C_WIDTH + C_EXP + 2 * g * LANES for g in range(C_KV_HEADS))
    bk_cols = tuple(B_WIDTH + h * B_DH for h in range(B_HEADS))
    bv_cols = tuple(2 * B_WIDTH + h * B_DH for h in range(B_HEADS))

    x_lat = x.reshape(n_lat, d)
    x_ctx = ctx.reshape(n_ctx, d)

    in_splits = ((0, a_hi), (a_hi, b_hi), (b_hi, c_hi), (c_hi, in_width))
    w_in_cast = _cast_weight(w_in, 0, in_splits)

    for l in range(depth):
        need_ctx = l < depth - 1
        mod_l = mod_all[l, :batch].reshape(batch, 1, N_MOD * d)
        mod_c = mod_all[l, batch:batch + 1].reshape(1, 1, N_MOD * d)
        w_a, w_b, w_c, w_g = w_in_cast
        na_tables, na_types = _na_tables(na_rpb[l], seq // GRID_W)

        h_lat = _norm_modulate(x_lat, norm1_g[l], mod_l, 0, seq)
        h_ctx = _norm_modulate(x_ctx, norm1_g[l], mod_c, 0, n_ctx)

        pa_lat, (wfu,) = _project(h_lat, w_a, f32, 1024, riders=[(w_ffn_up, l, None)], name="proj_hgrn")
        pb_lat, (wbr, wo) = _project(h_lat, w_b, bf16, 1024, col_scale=b_scale,
                                     riders=[(w_branch, l, None), (w_out, l, None)], name="proj_na")
        pc_lat = _project_window(h_lat, w_c, cos_t, sin_t)
        g_lat, (wfg,) = _project(h_lat, w_g, bf16, 1024, sigmoid=True, riders=[(w_ffn_gate, l, None)],
                                 name="proj_gates")
        pa_ctx, _ = _project(h_ctx, w_a, f32, 1024, name="proj_hgrn")
        pb_ctx, _ = _project(h_ctx, w_b, bf16, 512, col_scale=b_scale, name="proj_na")
        pc_ctx = _project_window(h_ctx, w_c, ones_t, zeros_t)

        a_lat, a_ctx = _hgrn_mixer(pa_lat, pa_ctx, lower_bounds[l], a_norm_g[l], seq, ctx_len, need_ctx)
        b_lat = _neighborhood_attention(pb_lat, pb_ctx, na_tables, na_types, seq, ctx_len)
        c_lat = _window_attention(pc_lat, pc_ctx, c_sink[l], seq, ctx_len)

        m_lat, (wfd,) = _merge(a_lat, b_lat, c_lat, g_lat, wbr, riders=[(w_ffn_down, l, None)])
        x_lat = _gated_residual_matmul(m_lat, wo, x_lat, mod_l, 2, seq, 512, "out_proj", a_buffers=2)
        h2 = _norm_modulate(x_lat, norm2_g[l], mod_l, 3, seq)
        next_in = [(w_in, l + 1, in_splits)] if l + 1 < depth else []
        u, nxt = _swiglu_up(h2, wfg, wfu, riders=next_in)
        if nxt:
            w_in_cast = nxt[0]
        x_lat = _gated_residual_matmul(u, wfd, x_lat, mod_l, 5, seq, FFN_TN, "ffn_down")

        if need_ctx:
            g_ctx, _ = _project(h_ctx, w_g, bf16, 1024, sigmoid=True, name="proj_gates")
            b_ctx = _context_attention(pb_ctx, ctx_len, B_HEADS, B_DH, bk_cols, bv_cols)
            c_ctx_o = _context_attention(pc_ctx, ctx_len, C_HEADS, C_DH, ck_cols, cv_cols, sink=c_sink[l])
            m_ctx, _ = _merge(a_ctx, b_ctx, c_ctx_o, g_ctx, wbr)
            x_ctx = _gated_residual_matmul(m_ctx, wo, x_ctx, mod_c, 2, n_ctx, 1024, "out_proj")
            h2c = _norm_modulate(x_ctx, norm2_g[l], mod_c, 3, n_ctx)
            uc, _ = _swiglu_up(h2c, wfg, wfu)
            x_ctx = _gated_residual_matmul(uc, wfd, x_ctx, mod_c, 5, n_ctx, FFN_TN, "ffn_down")

    return _final_norm(x_lat, final_norm_g).reshape(batch, seq, d)
```

```python
import functools

import jax
import jax.numpy as jnp
import numpy as np
from jax import lax
from jax.experimental import pallas as pl
from jax.experimental.pallas import tpu as pltpu

GRID_W = 64
EPS = 1e-6
NEG_INF = -1e30
N_MOD = 6
A_HEADS, A_DK, A_DV, A_CHUNK = 16, 128, 128, 32
A_QK = A_HEADS * A_DK
A_WIDTH = A_HEADS * A_DV
B_HEADS, B_DH = 8, 128
B_WIDTH = B_HEADS * B_DH
NA_ROWS, NA_COLS = 8, 16
C_HEADS, C_KV_HEADS, C_DH = 16, 2, 64
C_WIDTH = C_HEADS * C_DH
C_KV_WIDTH = C_KV_HEADS * C_DH
C_WINDOW = 128
C_BLOCK = 128
ROPE_BASE = 10000.0
N_BRANCH = 3

LANES = 128
VMEM_LIMIT_BYTES = 56 * 1024 * 1024

HGRN_BLOCK = 256
HGRN_HEADS_PER_STEP = 16
NA_QROWS = 4
NA_UNION = NA_ROWS + NA_QROWS
NA_HEADS_PER_STEP = 8
FFN_TN = 256
CAST_BLOCK_BYTES = 4 * 1024 * 1024

C_PAIRS = C_WIDTH // LANES
C_EXP = 2 * C_KV_HEADS * LANES
C_OUT_WIDTH = C_WIDTH + 2 * C_EXP

_NT = (((1,), (1,)), ((), ()))
_TN = (((0,), (0,)), ((), ()))

bf16 = jnp.bfloat16
f32 = jnp.float32


def _params(*sem):
    return pltpu.CompilerParams(dimension_semantics=sem, vmem_limit_bytes=VMEM_LIMIT_BYTES)


def _row_tile(rows, want):
    t = min(rows, want)
    while rows % t:
        t //= 2
    return t


def _sigmoid(x):
    return 1.0 / (1.0 + jnp.exp(-x))


def _silu(x):
    return x * _sigmoid(x)


def _sigmoid_tanh(x):
    return 0.5 * jnp.tanh(0.5 * x) + 0.5


def _col_chunks(n, width=256):
    width = min(width, n)
    return [slice(c, c + width) for c in range(0, n, width)]


def _mod_kernel(c_ref, w_ref, b_ref, o_ref):
    a = _silu(c_ref[...]).astype(bf16)
    o_ref[...] = jnp.dot(a, w_ref[...].astype(bf16), preferred_element_type=f32) + b_ref[...]


def _modulation(c_rows, w_mod, b_mod):
    depth, d, n = w_mod.shape
    rows = c_rows.shape[0]
    tn = 512
    return pl.pallas_call(
        _mod_kernel,
        grid=(depth, n // tn),
        in_specs=[
            pl.BlockSpec((rows, d), lambda l, j: (0, 0)),
            pl.BlockSpec((None, d, tn), lambda l, j: (l, 0, j)),
            pl.BlockSpec((None, 1, tn), lambda l, j: (l, 0, j)),
        ],
        out_specs=pl.BlockSpec((None, rows, tn), lambda l, j: (l, 0, j)),
        out_shape=jax.ShapeDtypeStruct((depth, rows, n), f32),
        compiler_params=_params("arbitrary", "arbitrary"),
        name="modulation",
    )(c_rows, w_mod, b_mod.reshape(depth, 1, n))


def _norm_mod_kernel(x_ref, g_ref, shift_ref, scale_ref, o_ref):
    x = x_ref[...]
    y = x * lax.rsqrt(jnp.mean(x * x, axis=-1, keepdims=True) + EPS) * g_ref[...]
    o_ref[...] = (y * (1.0 + scale_ref[...]) + shift_ref[...]).astype(o_ref.dtype)


def _norm_kernel(x_ref, g_ref, o_ref):
    x = x_ref[...]
    y = x * lax.rsqrt(jnp.mean(x * x, axis=-1, keepdims=True) + EPS) * g_ref[...]
    o_ref[...] = y.astype(o_ref.dtype)


def _norm_modulate(x, g, mod, shift_idx, rows_per_group):
    rows, d = x.shape
    tm = _row_tile(rows_per_group, 256)
    per = rows_per_group // tm
    return pl.pallas_call(
        _norm_mod_kernel,
        grid=(rows // tm,),
        in_specs=[
            pl.BlockSpec((tm, d), lambda i: (i, 0)),
            pl.BlockSpec((1, d), lambda i: (0, 0)),
            pl.BlockSpec((None, 1, d), lambda i: (i // per, 0, shift_idx)),
            pl.BlockSpec((None, 1, d), lambda i: (i // per, 0, shift_idx + 1)),
        ],
        out_specs=pl.BlockSpec((tm, d), lambda i: (i, 0)),
        out_shape=jax.ShapeDtypeStruct((rows, d), bf16),
        compiler_params=_params("arbitrary"),
        name="norm_modulate",
    )(x, g.reshape(1, d), mod, mod)


def _final_norm(x, g):
    rows, d = x.shape
    tm = _row_tile(rows, 256)
    return pl.pallas_call(
        _norm_kernel,
        grid=(rows // tm,),
        in_specs=[pl.BlockSpec((tm, d), lambda i: (i, 0)), pl.BlockSpec((1, d), lambda i: (0, 0))],
        out_specs=pl.BlockSpec((tm, d), lambda i: (i, 0)),
        out_shape=jax.ShapeDtypeStruct((rows, d), f32),
        compiler_params=_params("arbitrary"),
        name="final_norm",
    )(x, g.reshape(1, d))


def _proj_kernel(h_ref, w_ref, o_ref):
    h = h_ref[...]
    for cs in _col_chunks(o_ref.shape[1]):
        o_ref[:, cs] = jnp.dot(h, w_ref[:, cs], preferred_element_type=f32).astype(o_ref.dtype)


def _proj_scale_kernel(h_ref, w_ref, s_ref, o_ref):
    h = h_ref[...]
    for cs in _col_chunks(o_ref.shape[1]):
        acc = jnp.dot(h, w_ref[:, cs], preferred_element_type=f32)
        o_ref[:, cs] = (acc * s_ref[:, cs]).astype(o_ref.dtype)


def _proj_sigmoid_kernel(h_ref, w_ref, o_ref):
    h = h_ref[...]
    for cs in _col_chunks(o_ref.shape[1]):
        acc = jnp.dot(h, w_ref[:, cs], preferred_element_type=f32)
        o_ref[:, cs] = _sigmoid_tanh(acc).astype(o_ref.dtype)


def _cast_kernel(w_ref, *o_refs, bounds):
    for o_ref, (lo, hi) in zip(o_refs, bounds):
        o_ref[...] = w_ref[:, lo:hi].astype(o_ref.dtype)


def _cast_weight(w, layer, splits=None):
    _, k, n = w.shape
    bounds = tuple(splits) if splits else ((0, n),)
    tk = 1 << ((CAST_BLOCK_BYTES // (4 * n)).bit_length() - 1)
    while k % tk:
        tk //= 2
    outs = pl.pallas_call(
        functools.partial(_cast_kernel, bounds=bounds),
        grid=(k // tk,),
        in_specs=[pl.BlockSpec((None, tk, n), lambda i: (layer, i, 0))],
        out_specs=[pl.BlockSpec((tk, hi - lo), lambda i: (i, 0)) for lo, hi in bounds],
        out_shape=[jax.ShapeDtypeStruct((k, hi - lo), bf16) for lo, hi in bounds],
        compiler_params=_params("arbitrary"),
        name="cast_weight",
    )(w)
    return outs if splits else outs[0]


class _Riders:
    def __init__(self, riders, steps, step_of):
        self.args, self.in_specs, self.out_specs, self.out_shapes, self.bounds = [], [], [], [], []
        self.plan = []
        for w, layer, splits in riders:
            _, k, n = w.shape
            rows = 16
            while rows < k and (k % rows or k // rows > steps):
                rows *= 2
            carried = k % rows == 0 and k // rows <= steps
            self.plan.append((carried, w, layer, splits))
            if not carried:
                continue
            last = k // rows - 1

            def blk(*g, last=last):
                return jnp.minimum(step_of(*g), last)

            bounds = tuple(splits) if splits else ((0, n),)
            self.args.append(w)
            self.in_specs.append(pl.BlockSpec((None, rows, n), lambda *g, b=blk, l=layer: (l, b(*g), 0)))
            for lo, hi in bounds:
                self.out_specs.append(pl.BlockSpec((rows, hi - lo), lambda *g, b=blk: (b(*g), 0)))
                self.out_shapes.append(jax.ShapeDtypeStruct((k, hi - lo), bf16))
            self.bounds.append(bounds)

    def wrap(self, body, n_in, n_out):
        n_src = len(self.bounds)
        n_dst = len(self.out_specs)
        bounds = self.bounds

        def kern(*refs):
            ins, srcs = refs[:n_in], refs[n_in:n_in + n_src]
            o0 = n_in + n_src
            outs, dsts = refs[o0:o0 + n_out], iter(refs[o0 + n_out:o0 + n_out + n_dst])
            body(*ins, *outs, *refs[o0 + n_out + n_dst:])
            for src, bnd in zip(srcs, bounds):
                for lo, hi in bnd:
                    dst = next(dsts)
                    dst[...] = src[:, lo:hi].astype(dst.dtype)

        return kern

    def split(self, results, n_out):
        host, rest = results[:n_out], list(results[n_out:])
        per = []
        for carried, w, layer, splits in self.plan:
            if carried:
                count = len(splits) if splits else 1
                got, rest = rest[:count], rest[count:]
                per.append(got[0] if not splits else tuple(got))
            else:
                got = _cast_weight(w, layer, splits)
                per.append(tuple(got) if splits else got)
        return host, per


def _project(h, w, out_dtype, tn, *, tm_want=1024, h_buffers=2, col_scale=None, sigmoid=False,
             riders=(), name="proj"):
    rows, d = h.shape
    n = w.shape[1]
    nj = n // tn
    tm = _row_tile(rows, tm_want)
    single = dict(pipeline_mode=pl.Buffered(1)) if h_buffers == 1 else {}
    h_spec = pl.BlockSpec((tm, d), lambda i, j: (i, 0), **single)
    w_spec = pl.BlockSpec((d, tn), lambda i, j: (0, j))
    args, specs = [h, w], [h_spec, w_spec]
    if sigmoid:
        kern = _proj_sigmoid_kernel
    elif col_scale is not None:
        kern = _proj_scale_kernel
        args.append(col_scale)
        specs.append(pl.BlockSpec((1, tn), lambda i, j: (0, j)))
    else:
        kern = _proj_kernel
    ride = _Riders(riders, (rows // tm) * nj, lambda i, j: i * nj + j)
    res = pl.pallas_call(
        ride.wrap(kern, len(args), 1),
        grid=(rows // tm, nj),
        in_specs=specs + ride.in_specs,
        out_specs=[pl.BlockSpec((tm, tn), lambda i, j: (i, j))] + ride.out_specs,
        out_shape=[jax.ShapeDtypeStruct((rows, n), out_dtype)] + ride.out_shapes,
        compiler_params=_params("arbitrary", "arbitrary"),
        name=name,
    )(*args, *ride.args)
    (out,), extra = ride.split(res, 1)
    return out, extra


def _rope_rotate(x, cos, sin):
    n = x.shape[-1]
    lane = lax.broadcasted_iota(jnp.int32, x.shape, x.ndim - 1)
    up = pltpu.roll(x, n - 16, x.ndim - 1)
    down = pltpu.roll(x, 16, x.ndim - 1)
    return x * cos + jnp.where((lane % 32) < 16, up, down) * sin


def _proj_window_kernel(h_ref, w_ref, cos_ref, sin_ref, o_ref):
    acc = jnp.dot(h_ref[...], w_ref[...], preferred_element_type=f32)
    cos = cos_ref[...]
    sin = sin_ref[...]
    for j in range(C_PAIRS):
        x = _rope_rotate(acc[:, j * LANES:(j + 1) * LANES], cos, sin) * (C_DH ** -0.5)
        o_ref[:, j * LANES:(j + 1) * LANES] = x.astype(o_ref.dtype)
    k = _rope_rotate(acc[:, C_WIDTH:C_WIDTH + LANES], cos, sin)
    v = acc[:, C_WIDTH + LANES:C_WIDTH + 2 * LANES]
    low = lax.broadcasted_iota(jnp.int32, k.shape, 1) < C_DH
    for idx, x in enumerate((k, v)):
        swapped = pltpu.roll(x, C_DH, 1)
        groups = (jnp.where(low, x, 0.0), jnp.where(low, 0.0, swapped),
                  jnp.where(low, swapped, 0.0), jnp.where(low, 0.0, x))
        base = C_WIDTH + idx * C_EXP
        for c, val in enumerate(groups):
            o_ref[:, base + c * LANES:base + (c + 1) * LANES] = val.astype(o_ref.dtype)


def _project_window(h, w, cos, sin):
    rows, d = h.shape
    n = w.shape[1]
    tm = _row_tile(rows, 1024)
    tok_tiles = cos.shape[0] // tm
    t_spec = pl.BlockSpec((tm, LANES), lambda i: (i % tok_tiles, 0))
    return pl.pallas_call(
        _proj_window_kernel,
        grid=(rows // tm,),
        in_specs=[pl.BlockSpec((tm, d), lambda i: (i, 0)),
                  pl.BlockSpec((d, n), lambda i: (0, 0), pipeline_mode=pl.Buffered(1)),
                  t_spec, t_spec],
        out_specs=pl.BlockSpec((tm, C_OUT_WIDTH), lambda i: (i, 0)),
        out_shape=jax.ShapeDtypeStruct((rows, C_OUT_WIDTH), bf16),
        compiler_params=_params("arbitrary"),
        name="proj_window",
    )(h, w, cos, sin)


def _hgrn_kernel(*refs, rev, has_s0, emit_state, readout, hp):
    it = iter(refs)
    q_ref, v_ref, f_ref, lb_ref = next(it), next(it), next(it), next(it)
    s0_ref = next(it) if has_s0 else None
    if readout:
        g_ref, oprev_ref, ng_ref = next(it), next(it), next(it)
    o_ref = next(it)
    sout_ref = next(it) if emit_state else None
    st_ref = next(it)

    blk = pl.program_id(2)
    nblk = pl.num_programs(2)
    tb = q_ref.shape[0]
    nchunk = tb // A_CHUNK
    width = hp * A_DK
    heads = range(hp)

    def head(a, h):
        return a[:, h * A_DK:(h + 1) * A_DK]

    @pl.when(blk == 0)
    def _():
        if has_s0:
            st_ref[...] = s0_ref[...]
        else:
            st_ref[...] = jnp.zeros_like(st_ref)

    lb = lb_ref[...]
    f = lb + (1.0 - lb) * _sigmoid(f_ref[...])
    logf = jnp.log(f)
    k = 1.0 - f

    row = lax.broadcasted_iota(jnp.int32, (tb, tb), 0)
    col = lax.broadcasted_iota(jnp.int32, (tb, tb), 1)
    same = (row // A_CHUNK) == (col // A_CHUNK)
    causal = (col >= row) if rev else (col <= row)
    mask = same & causal
    tri = jnp.where(mask, 1.0, 0.0).astype(bf16)

    hi = logf.astype(bf16)
    lo = (logf - hi.astype(f32)).astype(bf16)
    cum2 = jnp.dot(tri, jnp.concatenate([hi, lo], axis=1), preferred_element_type=f32)
    cum = cum2[:, :width] + cum2[:, width:]

    q_dec = (_silu(q_ref[...]) * jnp.exp(cum)).astype(bf16)
    k_inv = (k * jnp.exp(-cum)).astype(bf16)
    v = v_ref[...].astype(bf16)
    att = [lax.dot_general(head(q_dec, h), head(k_inv, h), _NT, preferred_element_type=f32)
           for h in heads]
    att = [jnp.where(mask, a, 0.0).astype(bf16) for a in att]
    o_intra = [jnp.dot(att[h], head(v, h), preferred_element_type=f32) for h in heads]

    order = range(nchunk - 1, -1, -1) if rev else range(nchunk)
    k_end, dec = {}, {}
    for j in order:
        sl = slice(j * A_CHUNK, (j + 1) * A_CHUNK)
        last = j * A_CHUNK if rev else (j + 1) * A_CHUNK - 1
        tot = cum[last:last + 1, :]
        k_end[j] = (k[sl] * jnp.exp(tot - cum[sl])).astype(bf16)
        dec[j] = jnp.exp(tot)
    zero = jnp.zeros((A_CHUNK, A_DK), bf16)
    upd = {}
    for h in heads:
        k_diag = jnp.concatenate(
            [jnp.concatenate([head(k_end[j], h) if c == j else zero for c in range(nchunk)], axis=1)
             for j in range(nchunk)], axis=0)
        u_all = lax.dot_general(head(v, h), k_diag, _TN, preferred_element_type=f32)
        for j in range(nchunk):
            upd[h, j] = u_all[:, j * A_DK:(j + 1) * A_DK]

    before = {}
    final = []
    for h in heads:
        s = st_ref[h]
        for j in order:
            before[h, j] = s.astype(bf16)
            s = s * head(dec[j], h) + upd[h, j]
        st_ref[h] = s
        final.append(s)

    outs = []
    for h in heads:
        o_inter = [lax.dot_general(head(q_dec[j * A_CHUNK:(j + 1) * A_CHUNK], h), before[h, j], _NT,
                                   preferred_element_type=f32) for j in range(nchunk)]
        outs.append(o_intra[h] + jnp.concatenate(o_inter, axis=0))

    if readout:
        ng = ng_ref[...]
        normed = []
        for h in heads:
            o = outs[h] + head(oprev_ref[...], h)
            normed.append(o * lax.rsqrt(jnp.mean(o * o, axis=-1, keepdims=True) + EPS) * ng)
        o_ref[...] = (jnp.concatenate(normed, axis=1) * _silu(g_ref[...])).astype(o_ref.dtype)
    else:
        o_ref[...] = jnp.concatenate(outs, axis=1)

    if emit_state:
        @pl.when(blk == nblk - 1)
        def _():
            for h in heads:
                sout_ref[h] = final[h]


def _hgrn_scan(proj, lb_dir, seq, direction, *, s0=None, emit_state=False, readout=None):
    rows = proj.shape[0]
    batch = rows // seq
    tb = min(HGRN_BLOCK, seq)
    nblk = seq // tb
    rev = direction == 1
    hp = HGRN_HEADS_PER_STEP
    hblocks = A_HEADS // hp
    width = hp * A_DK

    def tok(b, h, i):
        return b * nblk + (nblk - 1 - i if rev else i)

    def col_spec(group):
        return pl.BlockSpec((tb, width), lambda b, h, i: (tok(b, h, i), group * hblocks + h))

    state_spec = pl.BlockSpec((None, hp, A_DV, A_DK), lambda b, h, i: (b, h, 0, 0))
    args = [proj, proj, proj, lb_dir.reshape(1, A_QK)]
    specs = [col_spec(0), col_spec(1), col_spec(2 + direction),
             pl.BlockSpec((1, width), lambda b, h, i: (0, h))]
    if s0 is not None:
        args.append(s0)
        specs.append(state_spec)
    if readout is not None:
        o_prev, norm_g = readout
        args += [proj, o_prev, norm_g.reshape(1, A_DV)]
        specs += [col_spec(4),
                  pl.BlockSpec((tb, width), lambda b, h, i: (tok(b, h, i), h)),
                  pl.BlockSpec((1, A_DV), lambda b, h, i: (0, 0))]
    out_shape = [jax.ShapeDtypeStruct((rows, A_WIDTH), bf16 if readout is not None else f32)]
    out_specs = [pl.BlockSpec((tb, width), lambda b, h, i: (tok(b, h, i), h))]
    if emit_state:
        out_shape.append(jax.ShapeDtypeStruct((batch, A_HEADS, A_DV, A_DK), f32))
        out_specs.append(state_spec)
    kern = functools.partial(_hgrn_kernel, rev=rev, has_s0=s0 is not None,
                             emit_state=emit_state, readout=readout is not None, hp=hp)
    res = pl.pallas_call(
        kern,
        grid=(batch, hblocks, nblk),
        in_specs=specs,
        out_specs=out_specs,
        out_shape=out_shape,
        scratch_shapes=[pltpu.VMEM((hp, A_DV, A_DK), f32)],
        compiler_params=_params("arbitrary", "arbitrary", "arbitrary"),
        name="hgrn_scan",
    )(*args)
    return res if emit_state else res[0]


def _hgrn_mixer(proj_lat, proj_ctx, lb, norm_g, seq, ctx_len, need_ctx):
    o_c_f, s_f = _hgrn_scan(proj_ctx, lb[0], ctx_len, 0, emit_state=True)
    o_l_f = _hgrn_scan(proj_lat, lb[0], seq, 0, s0=s_f)
    if need_ctx:
        a_ctx, s_b = _hgrn_scan(proj_ctx, lb[1], ctx_len, 1, emit_state=True, readout=(o_c_f, norm_g))
    else:
        _, s_b = _hgrn_scan(proj_ctx, lb[1], ctx_len, 1, emit_state=True)
        a_ctx = None
    a_lat = _hgrn_scan(proj_lat, lb[1], seq, 1, s0=s_b, readout=(o_l_f, norm_g))
    return a_lat, a_ctx


def _attend(parts, sink=None):
    m = parts[0][0].max(axis=-1, keepdims=True)
    for s, _ in parts[1:]:
        m = jnp.maximum(m, s.max(axis=-1, keepdims=True))
    if sink is not None:
        m = jnp.maximum(m, sink)
    den = None
    acc = None
    for s, v in parts:
        e = jnp.exp(s - m)
        d = e.sum(axis=-1, keepdims=True)
        o = jnp.dot(e.astype(bf16), v, preferred_element_type=f32)
        den = d if den is None else den + d
        acc = o if acc is None else acc + o
    if sink is not None:
        den = den + jnp.exp(sink - m)
    return acc / den


def _na_kernel(types_ref, q_ref, k_ref, v_ref, kc_ref, vc_ref, tbl_ref, o_ref, *, grid_rows, hp):
    del types_ref
    step = pl.program_id(2)
    span = NA_UNION * GRID_W
    first = jnp.clip(step * NA_QROWS - NA_ROWS // 2, 0, grid_rows - NA_UNION)
    start = pl.multiple_of(first * GRID_W, GRID_W)
    for h in range(hp):
        hs = slice(h * B_DH, (h + 1) * B_DH)
        q = q_ref[:, hs]
        kn = k_ref[pl.ds(start, span), hs]
        vn = v_ref[pl.ds(start, span), hs]
        s_nb = lax.dot_general(q, kn, _NT, preferred_element_type=f32) + tbl_ref[h]
        s_cx = lax.dot_general(q, kc_ref[:, hs], _NT, preferred_element_type=f32)
        o = _attend([(s_nb, vn), (s_cx, vc_ref[:, hs])])
        o_ref[:, hs] = o.astype(o_ref.dtype)


def _na_tables(rpb, grid_rows):
    assert grid_rows >= NA_UNION and grid_rows % NA_QROWS == 0
    col = np.arange(GRID_W)
    col_off = np.clip(col[None, :] - col[:, None] + NA_COLS - 1, 0, 2 * NA_COLS - 2)
    col_start = np.clip(col - NA_COLS // 2, 0, GRID_W - NA_COLS)
    col_ok = (col[None, :] >= col_start[:, None]) & (col[None, :] < col_start[:, None] + NA_COLS)
    seen, types = {}, []
    for i in range(grid_rows // NA_QROWS):
        first = int(np.clip(i * NA_QROWS - NA_ROWS // 2, 0, grid_rows - NA_UNION))
        key_row = first + np.arange(NA_UNION)[None, :]
        r = i * NA_QROWS + np.arange(NA_QROWS)[:, None]
        row_start = np.clip(r - NA_ROWS // 2, 0, grid_rows - NA_ROWS)
        ok = (key_row >= row_start) & (key_row < row_start + NA_ROWS)
        assert (ok.sum(axis=1) == NA_ROWS).all()
        off = np.where(ok, key_row - r + NA_ROWS - 1, 0)
        sig = (ok.tobytes(), off.tobytes())
        if sig not in seen:
            seen[sig] = (len(seen), ok, off)
        types.append(seen[sig][0])
    toe = jnp.where(col_ok[None, None], rpb.astype(f32)[:, :, col_off], NEG_INF)
    neg = jnp.full((rpb.shape[0], GRID_W, GRID_W), NEG_INF, f32)
    tables = []
    for _, ok, off in sorted(seen.values(), key=lambda t: t[0]):
        rows = [jnp.concatenate([toe[:, off[rl, a]] if ok[rl, a] else neg for a in range(NA_UNION)],
                                axis=-1) for rl in range(NA_QROWS)]
        tables.append(jnp.concatenate(rows, axis=-2))
    return jnp.stack(tables, axis=1), jnp.asarray(np.array(types, np.int32))


def _neighborhood_attention(qkv_lat, qkv_ctx, tables, types, seq, ctx_len):
    rows = qkv_lat.shape[0]
    batch = rows // seq
    tq = NA_QROWS * GRID_W
    nq = seq // tq
    hp = NA_HEADS_PER_STEP
    hb = B_HEADS // hp
    width = hp * B_DH
    kern = functools.partial(_na_kernel, grid_rows=seq // GRID_W, hp=hp)
    grid_spec = pltpu.PrefetchScalarGridSpec(
        num_scalar_prefetch=1,
        grid=(batch, hb, nq),
        in_specs=[
            pl.BlockSpec((tq, width), lambda b, h, i, t: (b * nq + i, h)),
            pl.BlockSpec((seq, width), lambda b, h, i, t: (b, hb + h), pipeline_mode=pl.Buffered(1)),
            pl.BlockSpec((seq, width), lambda b, h, i, t: (b, 2 * hb + h), pipeline_mode=pl.Buffered(1)),
            pl.BlockSpec((ctx_len, width), lambda b, h, i, t: (b, hb + h)),
            pl.BlockSpec((ctx_len, width), lambda b, h, i, t: (b, 2 * hb + h)),
            pl.BlockSpec((hp, None, tq, NA_UNION * GRID_W), lambda b, h, i, t: (h, t[i], 0, 0)),
        ],
        out_specs=pl.BlockSpec((tq, width), lambda b, h, i, t: (b * nq + i, h)),
    )
    return pl.pallas_call(
        kern,
        grid_spec=grid_spec,
        out_shape=jax.ShapeDtypeStruct((rows, B_WIDTH), bf16),
        compiler_params=_params("arbitrary", "arbitrary", "arbitrary"),
        name="neighborhood_attention",
    )(types, qkv_lat, qkv_lat, qkv_lat, qkv_ctx, qkv_ctx, tables)


def _window_kernel(sink_ref, q_ref, k_ref, v_ref, kc_ref, vc_ref, o_ref, *, seq):
    n = pl.program_id(1)
    span = 3 * C_BLOCK
    start = pl.multiple_of(jnp.clip((n - 1) * C_BLOCK, 0, seq - span), C_BLOCK)
    pairs = C_PAIRS // C_KV_HEADS
    rows = pairs * C_BLOCK
    qpos = n * C_BLOCK + lax.broadcasted_iota(jnp.int32, (rows, span), 0) % C_BLOCK
    kpos = start + lax.broadcasted_iota(jnp.int32, (rows, span), 1)
    valid = jnp.abs(qpos - kpos) <= C_WINDOW
    pair_id = lax.broadcasted_iota(jnp.int32, (rows, 1), 0) // C_BLOCK

    chains = [(g, e) for g in range(C_KV_HEADS) for e in range(2)]
    q = {g: jnp.concatenate([q_ref[:, (g * pairs + p) * LANES:(g * pairs + p + 1) * LANES]
                             for p in range(pairs)], axis=0) for g in range(C_KV_HEADS)}
    s_w, s_c, sink = {}, {}, {}
    for g, e in chains:
        cs = slice((2 * g + e) * LANES, (2 * g + e + 1) * LANES)
        s = lax.dot_general(q[g], k_ref[pl.ds(start, span), cs], _NT, preferred_element_type=f32)
        s_w[g, e] = jnp.where(valid, s, NEG_INF)
        s_c[g, e] = lax.dot_general(q[g], kc_ref[:, cs], _NT, preferred_element_type=f32)
        col = jnp.zeros((rows, 1), f32)
        for p in range(pairs):
            col = jnp.where(pair_id == p, sink_ref[(g * pairs + p) * 2 + e], col)
        sink[g, e] = col
    out = {}
    for g, e in chains:
        cs = slice((2 * g + e) * LANES, (2 * g + e + 1) * LANES)
        out[g, e] = _attend([(s_w[g, e], v_ref[pl.ds(start, span), cs]), (s_c[g, e], vc_ref[:, cs])],
                            sink=sink[g, e])
    for g in range(C_KV_HEADS):
        o = out[g, 0] + out[g, 1]
        for p in range(pairs):
            o_ref[:, (g * pairs + p) * LANES:(g * pairs + p + 1) * LANES] = (
                o[p * C_BLOCK:(p + 1) * C_BLOCK].astype(o_ref.dtype))


def _window_attention(qkv_lat, qkv_ctx, sink, seq, ctx_len):
    rows = qkv_lat.shape[0]
    batch = rows // seq
    nq = seq // C_BLOCK
    kblk = C_WIDTH // C_EXP
    kern = functools.partial(_window_kernel, seq=seq)
    resident = dict(pipeline_mode=pl.Buffered(1))
    return pl.pallas_call(
        kern,
        grid=(batch, nq),
        in_specs=[
            pl.BlockSpec(memory_space=pltpu.SMEM),
            pl.BlockSpec((C_BLOCK, C_WIDTH), lambda b, i: (b * nq + i, 0)),
            pl.BlockSpec((seq, C_EXP), lambda b, i: (b, kblk), **resident),
            pl.BlockSpec((seq, C_EXP), lambda b, i: (b, kblk + 1), **resident),
            pl.BlockSpec((ctx_len, C_EXP), lambda b, i: (b, kblk)),
            pl.BlockSpec((ctx_len, C_EXP), lambda b, i: (b, kblk + 1)),
        ],
        out_specs=pl.BlockSpec((C_BLOCK, C_WIDTH), lambda b, i: (b * nq + i, 0)),
        out_shape=jax.ShapeDtypeStruct((rows, C_WIDTH), bf16),
        compiler_params=_params("arbitrary", "arbitrary"),
        name="window_attention",
    )(sink.astype(f32), qkv_lat, qkv_lat, qkv_lat, qkv_ctx, qkv_ctx)


def _ctx_attn_kernel(*refs, heads, dh, k_cols, v_cols, has_sink):
    if has_sink:
        sink_ref, qkv_ref, o_ref = refs
    else:
        qkv_ref, o_ref = refs
    group = heads // len(k_cols)
    for kh, (kc, vc) in enumerate(zip(k_cols, v_cols)):
        k = qkv_ref[:, kc:kc + dh]
        v = qkv_ref[:, vc:vc + dh]
        for g in range(group):
            hq = kh * group + g
            q = qkv_ref[:, hq * dh:(hq + 1) * dh]
            s = lax.dot_general(q, k, _NT, preferred_element_type=f32)
            o = _attend([(s, v)], sink=sink_ref[hq] if has_sink else None)
            o_ref[:, hq * dh:(hq + 1) * dh] = o.astype(o_ref.dtype)


def _context_attention(qkv_ctx, ctx_len, heads, dh, k_cols, v_cols, sink=None):
    rows, width = qkv_ctx.shape
    kern = functools.partial(_ctx_attn_kernel, heads=heads, dh=dh, k_cols=k_cols, v_cols=v_cols,
                             has_sink=sink is not None)
    args, specs = [qkv_ctx], [pl.BlockSpec((ctx_len, width), lambda b: (b, 0))]
    if sink is not None:
        args.insert(0, sink.astype(f32))
        specs.insert(0, pl.BlockSpec(memory_space=pltpu.SMEM))
    return pl.pallas_call(
        kern,
        grid=(rows // ctx_len,),
        in_specs=specs,
        out_specs=pl.BlockSpec((ctx_len, heads * dh), lambda b: (b, 0)),
        out_shape=jax.ShapeDtypeStruct((rows, heads * dh), bf16),
        compiler_params=_params("arbitrary"),
        name="context_attention",
    )(*args)


def _merge_kernel(oa_ref, ob_ref, oc_ref, wa_ref, wb_ref, wc_ref, ga_ref, gb_ref, gc_ref, o_ref):
    oa, ob, oc = oa_ref[...], ob_ref[...], oc_ref[...]
    for cs in _col_chunks(o_ref.shape[1]):
        br_a = jnp.dot(oa, wa_ref[:, cs], preferred_element_type=f32)
        br_b = jnp.dot(ob, wb_ref[:, cs], preferred_element_type=f32)
        br_c = jnp.dot(oc, wc_ref[:, cs], preferred_element_type=f32)
        m = ga_ref[:, cs] * br_a + gb_ref[:, cs] * br_b + gc_ref[:, cs] * br_c
        o_ref[:, cs] = m.astype(o_ref.dtype)


def _merge(o_a, o_b, o_c, gates, w_branch, riders=()):
    rows = o_a.shape[0]
    d = w_branch.shape[1]
    tm = _row_tile(rows, 1024)
    tn = 512
    nj = d // tn
    assert A_WIDTH % B_WIDTH == 0 and B_WIDTH == C_WIDTH
    b_blk = A_WIDTH // B_WIDTH
    ride = _Riders(riders, (rows // tm) * nj, lambda i, j: i * nj + j)
    res = pl.pallas_call(
        ride.wrap(_merge_kernel, 9, 1),
        grid=(rows // tm, nj),
        in_specs=[
            pl.BlockSpec((tm, A_WIDTH), lambda i, j: (i, 0)),
            pl.BlockSpec((tm, B_WIDTH), lambda i, j: (i, 0)),
            pl.BlockSpec((tm, C_WIDTH), lambda i, j: (i, 0)),
            pl.BlockSpec((A_WIDTH, tn), lambda i, j: (0, j)),
            pl.BlockSpec((B_WIDTH, tn), lambda i, j: (b_blk, j)),
            pl.BlockSpec((C_WIDTH, tn), lambda i, j: (b_blk + 1, j)),
            pl.BlockSpec((tm, tn), lambda i, j: (i, j)),
            pl.BlockSpec((tm, tn), lambda i, j: (i, nj + j)),
            pl.BlockSpec((tm, tn), lambda i, j: (i, 2 * nj + j)),
        ] + ride.in_specs,
        out_specs=[pl.BlockSpec((tm, tn), lambda i, j: (i, j))] + ride.out_specs,
        out_shape=[jax.ShapeDtypeStruct((rows, d), bf16)] + ride.out_shapes,
        compiler_params=_params("arbitrary", "arbitrary"),
        name="merge_branches",
    )(o_a, o_b, o_c, w_branch, w_branch, w_branch, gates, gates, gates, *ride.args)
    (out,), extra = ride.split(res, 1)
    return out, extra


def _row_halves(m):
    return [slice(0, m // 2), slice(m // 2, m)] if m % 32 == 0 else [slice(0, m)]


def _residual_kernel(a_ref, w_ref, x_ref, gate_ref, o_ref):
    if o_ref.shape[1] > 256:
        a = a_ref[...]
        for cs in _col_chunks(o_ref.shape[1]):
            y = jnp.dot(a, w_ref[:, cs], preferred_element_type=f32)
            o_ref[:, cs] = x_ref[:, cs] + gate_ref[:, cs] * y
    else:
        w = w_ref[...]
        for rs in _row_halves(o_ref.shape[0]):
            y = jnp.dot(a_ref[rs, :], w, preferred_element_type=f32)
            o_ref[rs, :] = x_ref[rs, :] + gate_ref[...] * y


def _gated_residual_matmul(a, w, x, mod, gate_idx, rows_per_group, tn, name, a_buffers=1):
    rows, kdim = a.shape
    d = w.shape[1]
    tm = _row_tile(rows_per_group, 1024)
    per = rows_per_group // tm
    nj = d // tn
    single = dict(pipeline_mode=pl.Buffered(1)) if a_buffers == 1 else {}
    return pl.pallas_call(
        _residual_kernel,
        grid=(rows // tm, nj),
        in_specs=[
            pl.BlockSpec((tm, kdim), lambda i, j: (i, 0), **single),
            pl.BlockSpec((kdim, tn), lambda i, j: (0, j)),
            pl.BlockSpec((tm, tn), lambda i, j: (i, j)),
            pl.BlockSpec((None, 1, tn), lambda i, j: (i // per, 0, gate_idx * nj + j)),
        ],
        out_specs=pl.BlockSpec((tm, tn), lambda i, j: (i, j)),
        out_shape=jax.ShapeDtypeStruct((rows, d), f32),
        compiler_params=_params("arbitrary", "arbitrary"),
        name=name,
    )(a, w, x, mod)


def _residual_ksplit_kernel(a0_ref, a1_ref, w0_ref, w1_ref, x_ref, gate_ref, o_ref):
    w0, w1 = w0_ref[...], w1_ref[...]
    for rs in _row_halves(o_ref.shape[0]):
        y = (jnp.dot(a0_ref[rs, :], w0, preferred_element_type=f32)
             + jnp.dot(a1_ref[rs, :], w1, preferred_element_type=f32))
        o_ref[rs, :] = x_ref[rs, :] + gate_ref[...] * y


def _gated_residual_matmul_ksplit(a, w, x, mod, gate_idx, rows_per_group, tn, name):
    rows, kdim = a.shape
    d = w.shape[1]
    kh = kdim // 2
    assert kdim % 2 == 0 and kh % LANES == 0
    tm = _row_tile(rows_per_group, 1024)
    per = rows_per_group // tm
    nj = d // tn
    return pl.pallas_call(
        _residual_ksplit_kernel,
        grid=(rows // tm, nj),
        in_specs=[
            pl.BlockSpec((tm, kh), lambda i, j: (i, 0), pipeline_mode=pl.Buffered(1)),
            pl.BlockSpec((tm, kh), lambda i, j: (i, 1)),
            pl.BlockSpec((kh, tn), lambda i, j: (0, j)),
            pl.BlockSpec((kh, tn), lambda i, j: (1, j)),
            pl.BlockSpec((tm, tn), lambda i, j: (i, j)),
            pl.BlockSpec((None, 1, tn), lambda i, j: (i // per, 0, gate_idx * nj + j)),
        ],
        out_specs=pl.BlockSpec((tm, tn), lambda i, j: (i, j)),
        out_shape=jax.ShapeDtypeStruct((rows, d), f32),
        compiler_params=_params("arbitrary", "arbitrary"),
        name=name,
    )(a, a, w, w, x, mod)


def _swiglu_kernel(h_ref, wg_ref, wu_ref, o_ref):
    wg, wu = wg_ref[...], wu_ref[...]
    for rs in _row_halves(o_ref.shape[0]):
        h = h_ref[rs, :]
        g = jnp.dot(h, wg, preferred_element_type=f32)
        u = jnp.dot(h, wu, preferred_element_type=f32)
        o_ref[rs, :] = (g * _sigmoid_tanh(g) * u).astype(o_ref.dtype)


def _swiglu_up(h, w_gate, w_up, riders=()):
    rows, d = h.shape
    n = w_gate.shape[1]
    tm = _row_tile(rows, 1024)
    tn = FFN_TN
    nj = n // tn
    w_spec = pl.BlockSpec((d, tn), lambda i, j: (0, j))
    ride = _Riders(riders, (rows // tm) * nj, lambda i, j: i * nj + j)
    res = pl.pallas_call(
        ride.wrap(_swiglu_kernel, 3, 1),
        grid=(rows // tm, nj),
        in_specs=[pl.BlockSpec((tm, d), lambda i, j: (i, 0)), w_spec, w_spec] + ride.in_specs,
        out_specs=[pl.BlockSpec((tm, tn), lambda i, j: (i, j))] + ride.out_specs,
        out_shape=[jax.ShapeDtypeStruct((rows, n), bf16)] + ride.out_shapes,
        compiler_params=_params("arbitrary", "arbitrary"),
        name="swiglu_up",
    )(h, w_gate, w_up, *ride.args)
    (out,), extra = ride.split(res, 1)
    return out, extra


def _rope_tables(seq):
    half = C_DH // 2
    pos = jnp.arange(seq)
    inv = ROPE_BASE ** (-jnp.arange(0, half, 2, dtype=f32) / half)
    ang_row = (pos // GRID_W).astype(f32)[:, None] * inv[None, :]
    ang_col = (pos % GRID_W).astype(f32)[:, None] * inv[None, :]

    def one(ang):
        return (jnp.concatenate([jnp.cos(ang), jnp.cos(ang)], axis=-1),
                jnp.concatenate([-jnp.sin(ang), jnp.sin(ang)], axis=-1))

    cr, sr = one(ang_row)
    cc, sc = one(ang_col)
    cos = jnp.concatenate([cr, cc], axis=-1)
    sin = jnp.concatenate([sr, sc], axis=-1)
    reps = LANES // C_DH
    return jnp.tile(cos, (1, reps)), jnp.tile(sin, (1, reps))


def kernel(x, c, ctx, c_ctx, norm1_g, norm2_g, w_mod, b_mod, w_in, hgrn_lb, a_norm_g, na_rpb,
           c_sink, w_branch, w_out, w_ffn_gate, w_ffn_up, w_ffn_down, final_norm_g):
    batch, seq, d = x.shape
    ctx_len = ctx.shape[1]
    depth = w_in.shape[0]
    n_lat, n_ctx = batch * seq, batch * ctx_len

    lb_w = jax.nn.softmax(hgrn_lb.astype(f32), axis=0)
    lower_bounds = jnp.cumsum(lb_w, axis=0) - lb_w[:1]

    mod_rows = 8 * (-(-(batch + 1) // 8))
    c_rows = jnp.zeros((mod_rows, d), f32).at[:batch].set(c).at[batch].set(c_ctx)
    mod_all = _modulation(c_rows, w_mod, b_mod)

    cos_t, sin_t = _rope_tables(seq)
    ones_t = jnp.ones((n_ctx, LANES), f32)
    zeros_t = jnp.zeros((n_ctx, LANES), f32)

    a_hi = 3 * A_QK + 2 * A_WIDTH
    b_hi = a_hi + 3 * B_WIDTH
    c_hi = b_hi + C_WIDTH + 2 * C_KV_WIDTH
    in_width = w_in.shape[2]
    b_scale = jnp.concatenate([jnp.full((B_WIDTH,), B_DH ** -0.5, f32),
                               jnp.ones((2 * B_WIDTH,), f32)]).reshape(1, -1)
    ck_cols = tuple(C_WIDTH + 2 * g * LANES for g in range(C_KV_HEADS))
    cv_cols = tuple(C_WIDTH + C_EXP + 2 * g * LANES for g in range(C_KV_HEADS))
    bk_cols = tuple(B_WIDTH + h * B_DH for h in range(B_HEADS))
    bv_cols = tuple(2 * B_WIDTH + h * B_DH for h in range(B_HEADS))

    x_lat = x.reshape(n_lat, d)
    x_ctx = ctx.reshape(n_ctx, d)

    in_splits = ((0, a_hi), (a_hi, b_hi), (b_hi, c_hi), (c_hi, in_width))
    w_in_cast = _cast_weight(w_in, 0, in_splits)

    for l in range(depth):
        need_ctx = l < depth - 1
        mod_l = mod_all[l, :batch].reshape(batch, 1, N_MOD * d)
        mod_c = mod_all[l, batch:batch + 1].reshape(1, 1, N_MOD * d)
        w_a, w_b, w_c, w_g = w_in_cast
        na_tables, na_types = _na_tables(na_rpb[l], seq // GRID_W)

        h_lat = _norm_modulate(x_lat, norm1_g[l], mod_l, 0, seq)
        h_ctx = _norm_modulate(x_ctx, norm1_g[l], mod_c, 0, n_ctx)

        pa_lat, (wfu,) = _project(h_lat, w_a, f32, 1024, riders=[(w_ffn_up, l, None)], name="proj_hgrn")
        pb_lat, (wbr, wo) = _project(h_lat, w_b, bf16, 1024, col_scale=b_scale,
                                     riders=[(w_branch, l, None), (w_out, l, None)], name="proj_na")
        pc_lat = _project_window(h_lat, w_c, cos_t, sin_t)
        g_lat, (wfg,) = _project(h_lat, w_g, bf16, 1024, sigmoid=True, riders=[(w_ffn_gate, l, None)],
                                 name="proj_gates")
        pa_ctx, _ = _project(h_ctx, w_a, f32, 1024, name="proj_hgrn")
        pb_ctx, _ = _project(h_ctx, w_b, bf16, 512, col_scale=b_scale, name="proj_na")
        pc_ctx = _project_window(h_ctx, w_c, ones_t, zeros_t)

        a_lat, a_ctx = _hgrn_mixer(pa_lat, pa_ctx, lower_bounds[l], a_norm_g[l], seq, ctx_len, need_ctx)
        b_lat = _neighborhood_attention(pb_lat, pb_ctx, na_tables, na_types, seq, ctx_len)
        c_lat = _window_attention(pc_lat, pc_ctx, c_sink[l], seq, ctx_len)

        m_lat, (wfd,) = _merge(a_lat, b_lat, c_lat, g_lat, wbr, riders=[(w_ffn_down, l, None)])
        x_lat = _gated_residual_matmul(m_lat, wo, x_lat, mod_l, 2, seq, 512, "out_proj", a_buffers=2)
        h2 = _norm_modulate(x_lat, norm2_g[l], mod_l, 3, seq)
        next_in = [(w_in, l + 1, in_splits)] if l + 1 < depth else []
        u, nxt = _swiglu_up(h2, wfg, wfu, riders=next_in)
        if nxt:
            w_in_cast = nxt[0]
        x_lat = _gated_residual_matmul_ksplit(u, wfd, x_lat, mod_l, 5, seq, FFN_TN, "ffn_down")

        if need_ctx:
            g_ctx, _ = _project(h_ctx, w_g, bf16, 1024, sigmoid=True, name="proj_gates")
            b_ctx = _context_attention(pb_ctx, ctx_len, B_HEADS, B_DH, bk_cols, bv_cols)
            c_ctx_o = _context_attention(pc_ctx, ctx_len, C_HEADS, C_DH, ck_cols, cv_cols, sink=c_sink[l])
            m_ctx, _ = _merge(a_ctx, b_ctx, c_ctx_o, g_ctx, wbr)
            x_ctx = _gated_residual_matmul(m_ctx, wo, x_ctx, mod_c, 2, n_ctx, 1024, "out_proj")
            h2c = _norm_modulate(x_ctx, norm2_g[l], mod_c, 3, n_ctx)
            uc, _ = _swiglu_up(h2c, wfg, wfu)
            x_ctx = _gated_residual_matmul(uc, wfd, x_ctx, mod_c, 5, n_ctx, FFN_TN, "ffn_down")

    return _final_norm(x_lat, final_norm_g).reshape(batch, seq, d)
```

```python
import functools

import jax
import jax.numpy as jnp
import numpy as np
from jax import lax
from jax.experimental import pallas as pl
from jax.experimental.pallas import tpu as pltpu

GRID_W = 64
EPS = 1e-6
NEG_INF = -1e30
N_MOD = 6
A_HEADS, A_DK, A_DV, A_CHUNK = 16, 128, 128, 32
A_QK = A_HEADS * A_DK
A_WIDTH = A_HEADS * A_DV
B_HEADS, B_DH = 8, 128
B_WIDTH = B_HEADS * B_DH
NA_ROWS, NA_COLS = 8, 16
C_HEADS, C_KV_HEADS, C_DH = 16, 2, 64
C_WIDTH = C_HEADS * C_DH
C_KV_WIDTH = C_KV_HEADS * C_DH
C_WINDOW = 128
C_BLOCK = 128
ROPE_BASE = 10000.0
N_BRANCH = 3

LANES = 128
VMEM_LIMIT_BYTES = 56 * 1024 * 1024

HGRN_BLOCK = 256
HGRN_CHUNK = 2 * A_CHUNK
HGRN_HEADS_PER_STEP = 16
NA_QROWS = 4
NA_UNION = NA_ROWS + NA_QROWS
NA_HEADS_PER_STEP = 8
FFN_TN = 256
CAST_BLOCK_BYTES = 4 * 1024 * 1024

C_PAIRS = C_WIDTH // LANES
C_EXP = 2 * C_KV_HEADS * LANES
C_OUT_WIDTH = C_WIDTH + 2 * C_EXP

_NT = (((1,), (1,)), ((), ()))
_TN = (((0,), (0,)), ((), ()))

bf16 = jnp.bfloat16
f32 = jnp.float32


def _params(*sem):
    return pltpu.CompilerParams(dimension_semantics=sem, vmem_limit_bytes=VMEM_LIMIT_BYTES)


def _row_tile(rows, want):
    t = min(rows, want)
    while rows % t:
        t //= 2
    return t


def _sigmoid(x):
    return 1.0 / (1.0 + jnp.exp(-x))


def _silu(x):
    return x * _sigmoid(x)


def _sigmoid_tanh(x):
    return 0.5 * jnp.tanh(0.5 * x) + 0.5


def _col_chunks(n, width=256):
    width = min(width, n)
    return [slice(c, c + width) for c in range(0, n, width)]


def _mod_kernel(c_ref, w_ref, b_ref, o_ref):
    a = _silu(c_ref[...]).astype(bf16)
    o_ref[...] = jnp.dot(a, w_ref[...].astype(bf16), preferred_element_type=f32) + b_ref[...]


def _modulation(c_rows, w_mod, b_mod):
    depth, d, n = w_mod.shape
    rows = c_rows.shape[0]
    tn = 512
    return pl.pallas_call(
        _mod_kernel,
        grid=(depth, n // tn),
        in_specs=[
            pl.BlockSpec((rows, d), lambda l, j: (0, 0)),
            pl.BlockSpec((None, d, tn), lambda l, j: (l, 0, j)),
            pl.BlockSpec((None, 1, tn), lambda l, j: (l, 0, j)),
        ],
        out_specs=pl.BlockSpec((None, rows, tn), lambda l, j: (l, 0, j)),
        out_shape=jax.ShapeDtypeStruct((depth, rows, n), f32),
        compiler_params=_params("arbitrary", "arbitrary"),
        name="modulation",
    )(c_rows, w_mod, b_mod.reshape(depth, 1, n))


def _norm_mod_kernel(x_ref, g_ref, shift_ref, scale_ref, o_ref):
    x = x_ref[...]
    y = x * lax.rsqrt(jnp.mean(x * x, axis=-1, keepdims=True) + EPS) * g_ref[...]
    o_ref[...] = (y * (1.0 + scale_ref[...]) + shift_ref[...]).astype(o_ref.dtype)


def _norm_kernel(x_ref, g_ref, o_ref):
    x = x_ref[...]
    y = x * lax.rsqrt(jnp.mean(x * x, axis=-1, keepdims=True) + EPS) * g_ref[...]
    o_ref[...] = y.astype(o_ref.dtype)


def _norm_modulate(x, g, mod, shift_idx, rows_per_group):
    rows, d = x.shape
    tm = _row_tile(rows_per_group, 256)
    per = rows_per_group // tm
    return pl.pallas_call(
        _norm_mod_kernel,
        grid=(rows // tm,),
        in_specs=[
            pl.BlockSpec((tm, d), lambda i: (i, 0)),
            pl.BlockSpec((1, d), lambda i: (0, 0)),
            pl.BlockSpec((None, 1, d), lambda i: (i // per, 0, shift_idx)),
            pl.BlockSpec((None, 1, d), lambda i: (i // per, 0, shift_idx + 1)),
        ],
        out_specs=pl.BlockSpec((tm, d), lambda i: (i, 0)),
        out_shape=jax.ShapeDtypeStruct((rows, d), bf16),
        compiler_params=_params("arbitrary"),
        name="norm_modulate",
    )(x, g.reshape(1, d), mod, mod)


def _final_norm(x, g):
    rows, d = x.shape
    tm = _row_tile(rows, 256)
    return pl.pallas_call(
        _norm_kernel,
        grid=(rows // tm,),
        in_specs=[pl.BlockSpec((tm, d), lambda i: (i, 0)), pl.BlockSpec((1, d), lambda i: (0, 0))],
        out_specs=pl.BlockSpec((tm, d), lambda i: (i, 0)),
        out_shape=jax.ShapeDtypeStruct((rows, d), f32),
        compiler_params=_params("arbitrary"),
        name="final_norm",
    )(x, g.reshape(1, d))


def _proj_kernel(h_ref, w_ref, o_ref):
    h = h_ref[...]
    for cs in _col_chunks(o_ref.shape[1]):
        o_ref[:, cs] = jnp.dot(h, w_ref[:, cs], preferred_element_type=f32).astype(o_ref.dtype)


def _proj_scale_kernel(h_ref, w_ref, s_ref, o_ref):
    h = h_ref[...]
    for cs in _col_chunks(o_ref.shape[1]):
        acc = jnp.dot(h, w_ref[:, cs], preferred_element_type=f32)
        o_ref[:, cs] = (acc * s_ref[:, cs]).astype(o_ref.dtype)


def _proj_sigmoid_kernel(h_ref, w_ref, o_ref):
    h = h_ref[...]
    for cs in _col_chunks(o_ref.shape[1]):
        acc = jnp.dot(h, w_ref[:, cs], preferred_element_type=f32)
        o_ref[:, cs] = _sigmoid_tanh(acc).astype(o_ref.dtype)


def _cast_kernel(w_ref, *o_refs, bounds):
    for o_ref, (lo, hi) in zip(o_refs, bounds):
        o_ref[...] = w_ref[:, lo:hi].astype(o_ref.dtype)


def _cast_weight(w, layer, splits=None):
    _, k, n = w.shape
    bounds = tuple(splits) if splits else ((0, n),)
    tk = 1 << ((CAST_BLOCK_BYTES // (4 * n)).bit_length() - 1)
    while k % tk:
        tk //= 2
    outs = pl.pallas_call(
        functools.partial(_cast_kernel, bounds=bounds),
        grid=(k // tk,),
        in_specs=[pl.BlockSpec((None, tk, n), lambda i: (layer, i, 0))],
        out_specs=[pl.BlockSpec((tk, hi - lo), lambda i: (i, 0)) for lo, hi in bounds],
        out_shape=[jax.ShapeDtypeStruct((k, hi - lo), bf16) for lo, hi in bounds],
        compiler_params=_params("arbitrary"),
        name="cast_weight",
    )(w)
    return outs if splits else outs[0]


class _Riders:
    def __init__(self, riders, steps, step_of):
        self.args, self.in_specs, self.out_specs, self.out_shapes, self.bounds = [], [], [], [], []
        self.plan = []
        for w, layer, splits in riders:
            _, k, n = w.shape
            rows = 16
            while rows < k and (k % rows or k // rows > steps):
                rows *= 2
            carried = k % rows == 0 and k // rows <= steps
            self.plan.append((carried, w, layer, splits))
            if not carried:
                continue
            last = k // rows - 1

            def blk(*g, last=last):
                return jnp.minimum(step_of(*g), last)

            bounds = tuple(splits) if splits else ((0, n),)
            self.args.append(w)
            self.in_specs.append(pl.BlockSpec((None, rows, n), lambda *g, b=blk, l=layer: (l, b(*g), 0)))
            for lo, hi in bounds:
                self.out_specs.append(pl.BlockSpec((rows, hi - lo), lambda *g, b=blk: (b(*g), 0)))
                self.out_shapes.append(jax.ShapeDtypeStruct((k, hi - lo), bf16))
            self.bounds.append(bounds)

    def wrap(self, body, n_in, n_out):
        n_src = len(self.bounds)
        n_dst = len(self.out_specs)
        bounds = self.bounds

        def kern(*refs):
            ins, srcs = refs[:n_in], refs[n_in:n_in + n_src]
            o0 = n_in + n_src
            outs, dsts = refs[o0:o0 + n_out], iter(refs[o0 + n_out:o0 + n_out + n_dst])
            body(*ins, *outs, *refs[o0 + n_out + n_dst:])
            for src, bnd in zip(srcs, bounds):
                for lo, hi in bnd:
                    dst = next(dsts)
                    dst[...] = src[:, lo:hi].astype(dst.dtype)

        return kern

    def split(self, results, n_out):
        host, rest = results[:n_out], list(results[n_out:])
        per = []
        for carried, w, layer, splits in self.plan:
            if carried:
                count = len(splits) if splits else 1
                got, rest = rest[:count], rest[count:]
                per.append(got[0] if not splits else tuple(got))
            else:
                got = _cast_weight(w, layer, splits)
                per.append(tuple(got) if splits else got)
        return host, per


def _project(h, w, out_dtype, tn, *, tm_want=1024, h_buffers=2, col_scale=None, sigmoid=False,
             riders=(), name="proj"):
    rows, d = h.shape
    n = w.shape[1]
    nj = n // tn
    tm = _row_tile(rows, tm_want)
    single = dict(pipeline_mode=pl.Buffered(1)) if h_buffers == 1 else {}
    h_spec = pl.BlockSpec((tm, d), lambda i, j: (i, 0), **single)
    w_spec = pl.BlockSpec((d, tn), lambda i, j: (0, j))
    args, specs = [h, w], [h_spec, w_spec]
    if sigmoid:
        kern = _proj_sigmoid_kernel
    elif col_scale is not None:
        kern = _proj_scale_kernel
        args.append(col_scale)
        specs.append(pl.BlockSpec((1, tn), lambda i, j: (0, j)))
    else:
        kern = _proj_kernel
    ride = _Riders(riders, (rows // tm) * nj, lambda i, j: i * nj + j)
    res = pl.pallas_call(
        ride.wrap(kern, len(args), 1),
        grid=(rows // tm, nj),
        in_specs=specs + ride.in_specs,
        out_specs=[pl.BlockSpec((tm, tn), lambda i, j: (i, j))] + ride.out_specs,
        out_shape=[jax.ShapeDtypeStruct((rows, n), out_dtype)] + ride.out_shapes,
        compiler_params=_params("arbitrary", "arbitrary"),
        name=name,
    )(*args, *ride.args)
    (out,), extra = ride.split(res, 1)
    return out, extra


def _rope_rotate(x, cos, sin):
    n = x.shape[-1]
    lane = lax.broadcasted_iota(jnp.int32, x.shape, x.ndim - 1)
    up = pltpu.roll(x, n - 16, x.ndim - 1)
    down = pltpu.roll(x, 16, x.ndim - 1)
    return x * cos + jnp.where((lane % 32) < 16, up, down) * sin


def _proj_window_kernel(h_ref, w_ref, cos_ref, sin_ref, o_ref):
    acc = jnp.dot(h_ref[...], w_ref[...], preferred_element_type=f32)
    cos = cos_ref[...]
    sin = sin_ref[...]
    for j in range(C_PAIRS):
        x = _rope_rotate(acc[:, j * LANES:(j + 1) * LANES], cos, sin) * (C_DH ** -0.5)
        o_ref[:, j * LANES:(j + 1) * LANES] = x.astype(o_ref.dtype)
    k = _rope_rotate(acc[:, C_WIDTH:C_WIDTH + LANES], cos, sin)
    v = acc[:, C_WIDTH + LANES:C_WIDTH + 2 * LANES]
    low = lax.broadcasted_iota(jnp.int32, k.shape, 1) < C_DH
    for idx, x in enumerate((k, v)):
        swapped = pltpu.roll(x, C_DH, 1)
        groups = (jnp.where(low, x, 0.0), jnp.where(low, 0.0, swapped),
                  jnp.where(low, swapped, 0.0), jnp.where(low, 0.0, x))
        base = C_WIDTH + idx * C_EXP
        for c, val in enumerate(groups):
            o_ref[:, base + c * LANES:base + (c + 1) * LANES] = val.astype(o_ref.dtype)


def _project_window(h, w, cos, sin):
    rows, d = h.shape
    n = w.shape[1]
    tm = _row_tile(rows, 1024)
    tok_tiles = cos.shape[0] // tm
    t_spec = pl.BlockSpec((tm, LANES), lambda i: (i % tok_tiles, 0))
    return pl.pallas_call(
        _proj_window_kernel,
        grid=(rows // tm,),
        in_specs=[pl.BlockSpec((tm, d), lambda i: (i, 0)),
                  pl.BlockSpec((d, n), lambda i: (0, 0), pipeline_mode=pl.Buffered(1)),
                  t_spec, t_spec],
        out_specs=pl.BlockSpec((tm, C_OUT_WIDTH), lambda i: (i, 0)),
        out_shape=jax.ShapeDtypeStruct((rows, C_OUT_WIDTH), bf16),
        compiler_params=_params("arbitrary"),
        name="proj_window",
    )(h, w, cos, sin)


def _hgrn_kernel(*refs, rev, has_s0, emit_state, readout, hp):
    it = iter(refs)
    q_ref, v_ref, f_ref, lb_ref = next(it), next(it), next(it), next(it)
    s0_ref = next(it) if has_s0 else None
    if readout:
        g_ref, oprev_ref, ng_ref = next(it), next(it), next(it)
    o_ref = next(it)
    sout_ref = next(it) if emit_state else None
    st_ref = next(it)

    blk = pl.program_id(2)
    nblk = pl.num_programs(2)
    tb = q_ref.shape[0]
    chunk = HGRN_CHUNK
    nchunk = tb // chunk
    width = hp * A_DK
    heads = range(hp)

    def head(a, h):
        return a[:, h * A_DK:(h + 1) * A_DK]

    @pl.when(blk == 0)
    def _():
        if has_s0:
            st_ref[...] = s0_ref[...]
        else:
            st_ref[...] = jnp.zeros_like(st_ref)

    lb = lb_ref[...]
    f = lb + (1.0 - lb) * _sigmoid(f_ref[...])
    logf = jnp.log(f)
    k = 1.0 - f

    row = lax.broadcasted_iota(jnp.int32, (tb, tb), 0)
    col = lax.broadcasted_iota(jnp.int32, (tb, tb), 1)
    same = (row // chunk) == (col // chunk)
    causal = (col >= row) if rev else (col <= row)
    mask = same & causal
    tri = jnp.where(mask, 1.0, 0.0).astype(bf16)

    hi = logf.astype(bf16)
    lo = (logf - hi.astype(f32)).astype(bf16)
    cum2 = jnp.dot(tri, jnp.concatenate([hi, lo], axis=1), preferred_element_type=f32)
    cum = cum2[:, :width] + cum2[:, width:]

    mid_rows = []
    for j in range(nchunk):
        mid = j * chunk + (chunk // 2 if rev else chunk // 2 - 1)
        mid_rows.append(jnp.broadcast_to(cum[mid:mid + 1, :], (chunk, width)))
    rel = cum - jnp.concatenate(mid_rows, axis=0)
    q_act = _silu(q_ref[...])
    q_att = (q_act * jnp.exp(rel)).astype(bf16)
    k_inv = (k * jnp.exp(-rel)).astype(bf16)
    q_dec = (q_act * jnp.exp(cum)).astype(bf16)
    v = v_ref[...].astype(bf16)
    att = [lax.dot_general(head(q_att, h), head(k_inv, h), _NT, preferred_element_type=f32)
           for h in heads]
    att = [jnp.where(mask, a, 0.0).astype(bf16) for a in att]
    o_intra = [jnp.dot(att[h], head(v, h), preferred_element_type=f32) for h in heads]

    order = range(nchunk - 1, -1, -1) if rev else range(nchunk)
    k_end, dec = {}, {}
    for j in order:
        sl = slice(j * chunk, (j + 1) * chunk)
        last = j * chunk if rev else (j + 1) * chunk - 1
        tot = cum[last:last + 1, :]
        k_end[j] = (k[sl] * jnp.exp(tot - cum[sl])).astype(bf16)
        dec[j] = jnp.exp(tot)
    zero = jnp.zeros((chunk, A_DK), bf16)
    upd = {}
    for h in heads:
        k_diag = jnp.concatenate(
            [jnp.concatenate([head(k_end[j], h) if c == j else zero for c in range(nchunk)], axis=1)
             for j in range(nchunk)], axis=0)
        u_all = lax.dot_general(head(v, h), k_diag, _TN, preferred_element_type=f32)
        for j in range(nchunk):
            upd[h, j] = u_all[:, j * A_DK:(j + 1) * A_DK]

    before = {}
    final = []
    for h in heads:
        s = st_ref[h]
        for j in order:
            before[h, j] = s.astype(bf16)
            s = s * head(dec[j], h) + upd[h, j]
        st_ref[h] = s
        final.append(s)

    outs = []
    for h in heads:
        o_inter = [lax.dot_general(head(q_dec[j * chunk:(j + 1) * chunk], h), before[h, j], _NT,
                                   preferred_element_type=f32) for j in range(nchunk)]
        outs.append(o_intra[h] + jnp.concatenate(o_inter, axis=0))

    if readout:
        ng = ng_ref[...]
        normed = []
        for h in heads:
            o = outs[h] + head(oprev_ref[...], h)
            normed.append(o * lax.rsqrt(jnp.mean(o * o, axis=-1, keepdims=True) + EPS) * ng)
        o_ref[...] = (jnp.concatenate(normed, axis=1) * _silu(g_ref[...])).astype(o_ref.dtype)
    else:
        o_ref[...] = jnp.concatenate(outs, axis=1)

    if emit_state:
        @pl.when(blk == nblk - 1)
        def _():
            for h in heads:
                sout_ref[h] = final[h]


def _hgrn_scan(proj, lb_dir, seq, direction, *, s0=None, emit_state=False, readout=None):
    rows = proj.shape[0]
    batch = rows // seq
    tb = min(HGRN_BLOCK, seq)
    nblk = seq // tb
    rev = direction == 1
    hp = HGRN_HEADS_PER_STEP
    hblocks = A_HEADS // hp
    width = hp * A_DK

    def tok(b, h, i):
        return b * nblk + (nblk - 1 - i if rev else i)

    def col_spec(group):
        return pl.BlockSpec((tb, width), lambda b, h, i: (tok(b, h, i), group * hblocks + h))

    state_spec = pl.BlockSpec((None, hp, A_DV, A_DK), lambda b, h, i: (b, h, 0, 0))
    args = [proj, proj, proj, lb_dir.reshape(1, A_QK)]
    specs = [col_spec(0), col_spec(1), col_spec(2 + direction),
             pl.BlockSpec((1, width), lambda b, h, i: (0, h))]
    if s0 is not None:
        args.append(s0)
        specs.append(state_spec)
    if readout is not None:
        o_prev, norm_g = readout
        args += [proj, o_prev, norm_g.reshape(1, A_DV)]
        specs += [col_spec(4),
                  pl.BlockSpec((tb, width), lambda b, h, i: (tok(b, h, i), h)),
                  pl.BlockSpec((1, A_DV), lambda b, h, i: (0, 0))]
    out_shape = [jax.ShapeDtypeStruct((rows, A_WIDTH), bf16 if readout is not None else f32)]
    out_specs = [pl.BlockSpec((tb, width), lambda b, h, i: (tok(b, h, i), h))]
    if emit_state:
        out_shape.append(jax.ShapeDtypeStruct((batch, A_HEADS, A_DV, A_DK), f32))
        out_specs.append(state_spec)
    kern = functools.partial(_hgrn_kernel, rev=rev, has_s0=s0 is not None,
                             emit_state=emit_state, readout=readout is not None, hp=hp)
    res = pl.pallas_call(
        kern,
        grid=(batch, hblocks, nblk),
        in_specs=specs,
        out_specs=out_specs,
        out_shape=out_shape,
        scratch_shapes=[pltpu.VMEM((hp, A_DV, A_DK), f32)],
        compiler_params=_params("arbitrary", "arbitrary", "arbitrary"),
        name="hgrn_scan",
    )(*args)
    return res if emit_state else res[0]


def _hgrn_mixer(proj_lat, proj_ctx, lb, norm_g, seq, ctx_len, need_ctx):
    o_c_f, s_f = _hgrn_scan(proj_ctx, lb[0], ctx_len, 0, emit_state=True)
    o_l_f = _hgrn_scan(proj_lat, lb[0], seq, 0, s0=s_f)
    if need_ctx:
        a_ctx, s_b = _hgrn_scan(proj_ctx, lb[1], ctx_len, 1, emit_state=True, readout=(o_c_f, norm_g))
    else:
        _, s_b = _hgrn_scan(proj_ctx, lb[1], ctx_len, 1, emit_state=True)
        a_ctx = None
    a_lat = _hgrn_scan(proj_lat, lb[1], seq, 1, s0=s_b, readout=(o_l_f, norm_g))
    return a_lat, a_ctx


def _attend(parts, sink=None):
    m = parts[0][0].max(axis=-1, keepdims=True)
    for s, _ in parts[1:]:
        m = jnp.maximum(m, s.max(axis=-1, keepdims=True))
    if sink is not None:
        m = jnp.maximum(m, sink)
    den = None
    acc = None
    for s, v in parts:
        e = jnp.exp(s - m)
        d = e.sum(axis=-1, keepdims=True)
        o = jnp.dot(e.astype(bf16), v, preferred_element_type=f32)
        den = d if den is None else den + d
        acc = o if acc is None else acc + o
    if sink is not None:
        den = den + jnp.exp(sink - m)
    return acc / den


def _na_kernel(types_ref, q_ref, k_ref, v_ref, kc_ref, vc_ref, tbl_ref, o_ref, *, grid_rows, hp):
    del types_ref
    step = pl.program_id(2)
    span = NA_UNION * GRID_W
    first = jnp.clip(step * NA_QROWS - NA_ROWS // 2, 0, grid_rows - NA_UNION)
    start = pl.multiple_of(first * GRID_W, GRID_W)
    for h in range(hp):
        hs = slice(h * B_DH, (h + 1) * B_DH)
        q = q_ref[:, hs]
        kn = k_ref[pl.ds(start, span), hs]
        vn = v_ref[pl.ds(start, span), hs]
        s_nb = lax.dot_general(q, kn, _NT, preferred_element_type=f32) + tbl_ref[h]
        s_cx = lax.dot_general(q, kc_ref[:, hs], _NT, preferred_element_type=f32)
        o = _attend([(s_nb, vn), (s_cx, vc_ref[:, hs])])
        o_ref[:, hs] = o.astype(o_ref.dtype)


def _na_tables(rpb, grid_rows):
    assert grid_rows >= NA_UNION and grid_rows % NA_QROWS == 0
    col = np.arange(GRID_W)
    col_off = np.clip(col[None, :] - col[:, None] + NA_COLS - 1, 0, 2 * NA_COLS - 2)
    col_start = np.clip(col - NA_COLS // 2, 0, GRID_W - NA_COLS)
    col_ok = (col[None, :] >= col_start[:, None]) & (col[None, :] < col_start[:, None] + NA_COLS)
    seen, types = {}, []
    for i in range(grid_rows // NA_QROWS):
        first = int(np.clip(i * NA_QROWS - NA_ROWS // 2, 0, grid_rows - NA_UNION))
        key_row = first + np.arange(NA_UNION)[None, :]
        r = i * NA_QROWS + np.arange(NA_QROWS)[:, None]
        row_start = np.clip(r - NA_ROWS // 2, 0, grid_rows - NA_ROWS)
        ok = (key_row >= row_start) & (key_row < row_start + NA_ROWS)
        assert (ok.sum(axis=1) == NA_ROWS).all()
        off = np.where(ok, key_row - r + NA_ROWS - 1, 0)
        sig = (ok.tobytes(), off.tobytes())
        if sig not in seen:
            seen[sig] = (len(seen), ok, off)
        types.append(seen[sig][0])
    toe = jnp.where(col_ok[None, None], rpb.astype(f32)[:, :, col_off], NEG_INF)
    neg = jnp.full((rpb.shape[0], GRID_W, GRID_W), NEG_INF, f32)
    tables = []
    for _, ok, off in sorted(seen.values(), key=lambda t: t[0]):
        rows = [jnp.concatenate([toe[:, off[rl, a]] if ok[rl, a] else neg for a in range(NA_UNION)],
                                axis=-1) for rl in range(NA_QROWS)]
        tables.append(jnp.concatenate(rows, axis=-2))
    return jnp.stack(tables, axis=1), jnp.asarray(np.array(types, np.int32))


def _neighborhood_attention(qkv_lat, qkv_ctx, tables, types, seq, ctx_len):
    rows = qkv_lat.shape[0]
    batch = rows // seq
    tq = NA_QROWS * GRID_W
    nq = seq // tq
    hp = NA_HEADS_PER_STEP
    hb = B_HEADS // hp
    width = hp * B_DH
    kern = functools.partial(_na_kernel, grid_rows=seq // GRID_W, hp=hp)
    grid_spec = pltpu.PrefetchScalarGridSpec(
        num_scalar_prefetch=1,
        grid=(batch, hb, nq),
        in_specs=[
            pl.BlockSpec((tq, width), lambda b, h, i, t: (b * nq + i, h)),
            pl.BlockSpec((seq, width), lambda b, h, i, t: (b, hb + h), pipeline_mode=pl.Buffered(1)),
            pl.BlockSpec((seq, width), lambda b, h, i, t: (b, 2 * hb + h), pipeline_mode=pl.Buffered(1)),
            pl.BlockSpec((ctx_len, width), lambda b, h, i, t: (b, hb + h)),
            pl.BlockSpec((ctx_len, width), lambda b, h, i, t: (b, 2 * hb + h)),
            pl.BlockSpec((hp, None, tq, NA_UNION * GRID_W), lambda b, h, i, t: (h, t[i], 0, 0)),
        ],
        out_specs=pl.BlockSpec((tq, width), lambda b, h, i, t: (b * nq + i, h)),
    )
    return pl.pallas_call(
        kern,
        grid_spec=grid_spec,
        out_shape=jax.ShapeDtypeStruct((rows, B_WIDTH), bf16),
        compiler_params=_params("arbitrary", "arbitrary", "arbitrary"),
        name="neighborhood_attention",
    )(types, qkv_lat, qkv_lat, qkv_lat, qkv_ctx, qkv_ctx, tables)


def _window_kernel(sink_ref, q_ref, k_ref, v_ref, kc_ref, vc_ref, o_ref, *, seq):
    n = pl.program_id(1)
    span = 3 * C_BLOCK
    start = pl.multiple_of(jnp.clip((n - 1) * C_BLOCK, 0, seq - span), C_BLOCK)
    pairs = C_PAIRS // C_KV_HEADS
    rows = pairs * C_BLOCK
    qpos = n * C_BLOCK + lax.broadcasted_iota(jnp.int32, (C_BLOCK, span), 0)
    kpos = start + lax.broadcasted_iota(jnp.int32, (C_BLOCK, span), 1)
    off_window = jnp.where(jnp.abs(qpos - kpos) <= C_WINDOW, 0.0, NEG_INF)
    off_window = jnp.concatenate([off_window] * pairs, axis=0)
    pair_id = lax.broadcasted_iota(jnp.int32, (rows, 1), 0) // C_BLOCK

    chains = [(g, e) for g in range(C_KV_HEADS) for e in range(2)]
    q = {g: jnp.concatenate([q_ref[:, (g * pairs + p) * LANES:(g * pairs + p + 1) * LANES]
                             for p in range(pairs)], axis=0) for g in range(C_KV_HEADS)}
    s_w, s_c, sink = {}, {}, {}
    for g, e in chains:
        cs = slice((2 * g + e) * LANES, (2 * g + e + 1) * LANES)
        s = lax.dot_general(q[g], k_ref[pl.ds(start, span), cs], _NT, preferred_element_type=f32)
        s_w[g, e] = s + off_window
        s_c[g, e] = lax.dot_general(q[g], kc_ref[:, cs], _NT, preferred_element_type=f32)
        col = jnp.zeros((rows, 1), f32)
        for p in range(pairs):
            col = jnp.where(pair_id == p, sink_ref[(g * pairs + p) * 2 + e], col)
        sink[g, e] = col
    out = {}
    for g, e in chains:
        cs = slice((2 * g + e) * LANES, (2 * g + e + 1) * LANES)
        out[g, e] = _attend([(s_w[g, e], v_ref[pl.ds(start, span), cs]), (s_c[g, e], vc_ref[:, cs])],
                            sink=sink[g, e])
    for g in range(C_KV_HEADS):
        o = out[g, 0] + out[g, 1]
        for p in range(pairs):
            o_ref[:, (g * pairs + p) * LANES:(g * pairs + p + 1) * LANES] = (
                o[p * C_BLOCK:(p + 1) * C_BLOCK].astype(o_ref.dtype))


def _window_attention(qkv_lat, qkv_ctx, sink, seq, ctx_len):
    rows = qkv_lat.shape[0]
    batch = rows // seq
    nq = seq // C_BLOCK
    kblk = C_WIDTH // C_EXP
    kern = functools.partial(_window_kernel, seq=seq)
    resident = dict(pipeline_mode=pl.Buffered(1))
    return pl.pallas_call(
        kern,
        grid=(batch, nq),
        in_specs=[
            pl.BlockSpec(memory_space=pltpu.SMEM),
            pl.BlockSpec((C_BLOCK, C_WIDTH), lambda b, i: (b * nq + i, 0)),
            pl.BlockSpec((seq, C_EXP), lambda b, i: (b, kblk), **resident),
            pl.BlockSpec((seq, C_EXP), lambda b, i: (b, kblk + 1), **resident),
            pl.BlockSpec((ctx_len, C_EXP), lambda b, i: (b, kblk)),
            pl.BlockSpec((ctx_len, C_EXP), lambda b, i: (b, kblk + 1)),
        ],
        out_specs=pl.BlockSpec((C_BLOCK, C_WIDTH), lambda b, i: (b * nq + i, 0)),
        out_shape=jax.ShapeDtypeStruct((rows, C_WIDTH), bf16),
        compiler_params=_params("arbitrary", "arbitrary"),
        name="window_attention",
    )(sink.astype(f32), qkv_lat, qkv_lat, qkv_lat, qkv_ctx, qkv_ctx)


def _ctx_attn_kernel(*refs, heads, dh, k_cols, v_cols, has_sink):
    if has_sink:
        sink_ref, qkv_ref, o_ref = refs
    else:
        qkv_ref, o_ref = refs
    group = heads // len(k_cols)
    for kh, (kc, vc) in enumerate(zip(k_cols, v_cols)):
        k = qkv_ref[:, kc:kc + dh]
        v = qkv_ref[:, vc:vc + dh]
        for g in range(group):
            hq = kh * group + g
            q = qkv_ref[:, hq * dh:(hq + 1) * dh]
            s = lax.dot_general(q, k, _NT, preferred_element_type=f32)
            o = _attend([(s, v)], sink=sink_ref[hq] if has_sink else None)
            o_ref[:, hq * dh:(hq + 1) * dh] = o.astype(o_ref.dtype)


def _context_attention(qkv_ctx, ctx_len, heads, dh, k_cols, v_cols, sink=None):
    rows, width = qkv_ctx.shape
    kern = functools.partial(_ctx_attn_kernel, heads=heads, dh=dh, k_cols=k_cols, v_cols=v_cols,
                             has_sink=sink is not None)
    args, specs = [qkv_ctx], [pl.BlockSpec((ctx_len, width), lambda b: (b, 0))]
    if sink is not None:
        args.insert(0, sink.astype(f32))
        specs.insert(0, pl.BlockSpec(memory_space=pltpu.SMEM))
    return pl.pallas_call(
        kern,
        grid=(rows // ctx_len,),
        in_specs=specs,
        out_specs=pl.BlockSpec((ctx_len, heads * dh), lambda b: (b, 0)),
        out_shape=jax.ShapeDtypeStruct((rows, heads * dh), bf16),
        compiler_params=_params("arbitrary"),
        name="context_attention",
    )(*args)


def _merge_kernel(oa_ref, ob_ref, oc_ref, wa_ref, wb_ref, wc_ref, ga_ref, gb_ref, gc_ref, o_ref):
    oa, ob, oc = oa_ref[...], ob_ref[...], oc_ref[...]
    for cs in _col_chunks(o_ref.shape[1]):
        br_a = jnp.dot(oa, wa_ref[:, cs], preferred_element_type=f32)
        br_b = jnp.dot(ob, wb_ref[:, cs], preferred_element_type=f32)
        br_c = jnp.dot(oc, wc_ref[:, cs], preferred_element_type=f32)
        m = ga_ref[:, cs] * br_a + gb_ref[:, cs] * br_b + gc_ref[:, cs] * br_c
        o_ref[:, cs] = m.astype(o_ref.dtype)


def _merge(o_a, o_b, o_c, gates, w_branch, riders=()):
    rows = o_a.shape[0]
    d = w_branch.shape[1]
    tm = _row_tile(rows, 1024)
    tn = 512
    nj = d // tn
    assert A_WIDTH % B_WIDTH == 0 and B_WIDTH == C_WIDTH
    b_blk = A_WIDTH // B_WIDTH
    ride = _Riders(riders, (rows // tm) * nj, lambda i, j: i * nj + j)
    res = pl.pallas_call(
        ride.wrap(_merge_kernel, 9, 1),
        grid=(rows // tm, nj),
        in_specs=[
            pl.BlockSpec((tm, A_WIDTH), lambda i, j: (i, 0)),
            pl.BlockSpec((tm, B_WIDTH), lambda i, j: (i, 0)),
            pl.BlockSpec((tm, C_WIDTH), lambda i, j: (i, 0)),
            pl.BlockSpec((A_WIDTH, tn), lambda i, j: (0, j)),
            pl.BlockSpec((B_WIDTH, tn), lambda i, j: (b_blk, j)),
            pl.BlockSpec((C_WIDTH, tn), lambda i, j: (b_blk + 1, j)),
            pl.BlockSpec((tm, tn), lambda i, j: (i, j)),
            pl.BlockSpec((tm, tn), lambda i, j: (i, nj + j)),
            pl.BlockSpec((tm, tn), lambda i, j: (i, 2 * nj + j)),
        ] + ride.in_specs,
        out_specs=[pl.BlockSpec((tm, tn), lambda i, j: (i, j))] + ride.out_specs,
        out_shape=[jax.ShapeDtypeStruct((rows, d), bf16)] + ride.out_shapes,
        compiler_params=_params("arbitrary", "arbitrary"),
        name="merge_branches",
    )(o_a, o_b, o_c, w_branch, w_branch, w_branch, gates, gates, gates, *ride.args)
    (out,), extra = ride.split(res, 1)
    return out, extra


def _row_halves(m):
    return [slice(0, m // 2), slice(m // 2, m)] if m % 32 == 0 else [slice(0, m)]


def _residual_kernel(a_ref, w_ref, x_ref, gate_ref, o_ref):
    if o_ref.shape[1] > 256:
        a = a_ref[...]
        for cs in _col_chunks(o_ref.shape[1]):
            y = jnp.dot(a, w_ref[:, cs], preferred_element_type=f32)
            o_ref[:, cs] = x_ref[:, cs] + gate_ref[:, cs] * y
    else:
        w = w_ref[...]
        for rs in _row_halves(o_ref.shape[0]):
            y = jnp.dot(a_ref[rs, :], w, preferred_element_type=f32)
            o_ref[rs, :] = x_ref[rs, :] + gate_ref[...] * y


def _gated_residual_matmul(a, w, x, mod, gate_idx, rows_per_group, tn, name, a_buffers=1):
    rows, kdim = a.shape
    d = w.shape[1]
    tm = _row_tile(rows_per_group, 1024)
    per = rows_per_group // tm
    nj = d // tn
    single = dict(pipeline_mode=pl.Buffered(1)) if a_buffers == 1 else {}
    return pl.pallas_call(
        _residual_kernel,
        grid=(rows // tm, nj),
        in_specs=[
            pl.BlockSpec((tm, kdim), lambda i, j: (i, 0), **single),
            pl.BlockSpec((kdim, tn), lambda i, j: (0, j)),
            pl.BlockSpec((tm, tn), lambda i, j: (i, j)),
            pl.BlockSpec((None, 1, tn), lambda i, j: (i // per, 0, gate_idx * nj + j)),
        ],
        out_specs=pl.BlockSpec((tm, tn), lambda i, j: (i, j)),
        out_shape=jax.ShapeDtypeStruct((rows, d), f32),
        compiler_params=_params("arbitrary", "arbitrary"),
        name=name,
    )(a, w, x, mod)


def _residual_ksplit_kernel(a0_ref, a1_ref, w0_ref, w1_ref, x_ref, gate_ref, o_ref):
    w0, w1 = w0_ref[...], w1_ref[...]
    for rs in _row_halves(o_ref.shape[0]):
        y = (jnp.dot(a0_ref[rs, :], w0, preferred_element_type=f32)
             + jnp.dot(a1_ref[rs, :], w1, preferred_element_type=f32))
        o_ref[rs, :] = x_ref[rs, :] + gate_ref[...] * y


def _gated_residual_matmul_ksplit(a, w, x, mod, gate_idx, rows_per_group, tn, name):
    rows, kdim = a.shape
    d = w.shape[1]
    kh = kdim // 2
    assert kdim % 2 == 0 and kh % LANES == 0
    tm = _row_tile(rows_per_group, 1024)
    per = rows_per_group // tm
    nj = d // tn
    return pl.pallas_call(
        _residual_ksplit_kernel,
        grid=(rows // tm, nj),
        in_specs=[
            pl.BlockSpec((tm, kh), lambda i, j: (i, 0), pipeline_mode=pl.Buffered(1)),
            pl.BlockSpec((tm, kh), lambda i, j: (i, 1)),
            pl.BlockSpec((kh, tn), lambda i, j: (0, j)),
            pl.BlockSpec((kh, tn), lambda i, j: (1, j)),
            pl.BlockSpec((tm, tn), lambda i, j: (i, j)),
            pl.BlockSpec((None, 1, tn), lambda i, j: (i // per, 0, gate_idx * nj + j)),
        ],
        out_specs=pl.BlockSpec((tm, tn), lambda i, j: (i, j)),
        out_shape=jax.ShapeDtypeStruct((rows, d), f32),
        compiler_params=_params("arbitrary", "arbitrary"),
        name=name,
    )(a, a, w, w, x, mod)


def _swiglu_kernel(h_ref, wg_ref, wu_ref, o_ref):
    wg, wu = wg_ref[...], wu_ref[...]
    for rs in _row_halves(o_ref.shape[0]):
        h = h_ref[rs, :]
        g = jnp.dot(h, wg, preferred_element_type=f32)
        u = jnp.dot(h, wu, preferred_element_type=f32)
        o_ref[rs, :] = (g * _sigmoid_tanh(g) * u).astype(o_ref.dtype)


def _swiglu_up(h, w_gate, w_up, riders=()):
    rows, d = h.shape
    n = w_gate.shape[1]
    tm = _row_tile(rows, 1024)
    tn = FFN_TN
    nj = n // tn
    w_spec = pl.BlockSpec((d, tn), lambda i, j: (0, j))
    ride = _Riders(riders, (rows // tm) * nj, lambda i, j: i * nj + j)
    res = pl.pallas_call(
        ride.wrap(_swiglu_kernel, 3, 1),
        grid=(rows // tm, nj),
        in_specs=[pl.BlockSpec((tm, d), lambda i, j: (i, 0)), w_spec, w_spec] + ride.in_specs,
        out_specs=[pl.BlockSpec((tm, tn), lambda i, j: (i, j))] + ride.out_specs,
        out_shape=[jax.ShapeDtypeStruct((rows, n), bf16)] + ride.out_shapes,
        compiler_params=_params("arbitrary", "arbitrary"),
        name="swiglu_up",
    )(h, w_gate, w_up, *ride.args)
    (out,), extra = ride.split(res, 1)
    return out, extra


def _rope_tables(seq):
    half = C_DH // 2
    pos = jnp.arange(seq)
    inv = ROPE_BASE ** (-jnp.arange(0, half, 2, dtype=f32) / half)
    ang_row = (pos // GRID_W).astype(f32)[:, None] * inv[None, :]
    ang_col = (pos % GRID_W).astype(f32)[:, None] * inv[None, :]

    def one(ang):
        return (jnp.concatenate([jnp.cos(ang), jnp.cos(ang)], axis=-1),
                jnp.concatenate([-jnp.sin(ang), jnp.sin(ang)], axis=-1))

    cr, sr = one(ang_row)
    cc, sc = one(ang_col)
    cos = jnp.concatenate([cr, cc], axis=-1)
    sin = jnp.concatenate([sr, sc], axis=-1)
    reps = LANES // C_DH
    return jnp.tile(cos, (1, reps)), jnp.tile(sin, (1, reps))


def kernel(x, c, ctx, c_ctx, norm1_g, norm2_g, w_mod, b_mod, w_in, hgrn_lb, a_norm_g, na_rpb,
           c_sink, w_branch, w_out, w_ffn_gate, w_ffn_up, w_ffn_down, final_norm_g):
    batch, seq, d = x.shape
    ctx_len = ctx.shape[1]
    depth = w_in.shape[0]
    n_lat, n_ctx = batch * seq, batch * ctx_len

    lb_w = jax.nn.softmax(hgrn_lb.astype(f32), axis=0)
    lower_bounds = jnp.cumsum(lb_w, axis=0) - lb_w[:1]

    mod_rows = 8 * (-(-(batch + 1) // 8))
    c_rows = jnp.zeros((mod_rows, d), f32).at[:batch].set(c).at[batch].set(c_ctx)
    mod_all = _modulation(c_rows, w_mod, b_mod)

    cos_t, sin_t = _rope_tables(seq)
    ones_t = jnp.ones((n_ctx, LANES), f32)
    zeros_t = jnp.zeros((n_ctx, LANES), f32)

    a_hi = 3 * A_QK + 2 * A_WIDTH
    b_hi = a_hi + 3 * B_WIDTH
    c_hi = b_hi + C_WIDTH + 2 * C_KV_WIDTH
    in_width = w_in.shape[2]
    b_scale = jnp.concatenate([jnp.full((B_WIDTH,), B_DH ** -0.5, f32),
                               jnp.ones((2 * B_WIDTH,), f32)]).reshape(1, -1)
    ck_cols = tuple(C_WIDTH + 2 * g * LANES for g in range(C_KV_HEADS))
    cv_cols = tuple(C_WIDTH + C_EXP + 2 * g * LANES for g in range(C_KV_HEADS))
    bk_cols = tuple(B_WIDTH + h * B_DH for h in range(B_HEADS))
    bv_cols = tuple(2 * B_WIDTH + h * B_DH for h in range(B_HEADS))

    x_lat = x.reshape(n_lat, d)
    x_ctx = ctx.reshape(n_ctx, d)

    in_splits = ((0, a_hi), (a_hi, b_hi), (b_hi, c_hi), (c_hi, in_width))
    w_in_cast = _cast_weight(w_in, 0, in_splits)

    for l in range(depth):
        need_ctx = l < depth - 1
        mod_l = mod_all[l, :batch].reshape(batch, 1, N_MOD * d)
        mod_c = mod_all[l, batch:batch + 1].reshape(1, 1, N_MOD * d)
        w_a, w_b, w_c, w_g = w_in_cast
        na_tables, na_types = _na_tables(na_rpb[l], seq // GRID_W)

        h_lat = _norm_modulate(x_lat, norm1_g[l], mod_l, 0, seq)
        h_ctx = _norm_modulate(x_ctx, norm1_g[l], mod_c, 0, n_ctx)

        pa_lat, (wfu,) = _project(h_lat, w_a, f32, 1024, riders=[(w_ffn_up, l, None)], name="proj_hgrn")
        pb_lat, (wbr, wo) = _project(h_lat, w_b, bf16, 1024, col_scale=b_scale,
                                     riders=[(w_branch, l, None), (w_out, l, None)], name="proj_na")
        pc_lat = _project_window(h_lat, w_c, cos_t, sin_t)
        g_lat, (wfg,) = _project(h_lat, w_g, bf16, 1024, sigmoid=True, riders=[(w_ffn_gate, l, None)],
                                 name="proj_gates")
        pa_ctx, _ = _project(h_ctx, w_a, f32, 1024, name="proj_hgrn")
        pb_ctx, _ = _project(h_ctx, w_b, bf16, 512, col_scale=b_scale, name="proj_na")
        pc_ctx = _project_window(h_ctx, w_c, ones_t, zeros_t)

        a_lat, a_ctx = _hgrn_mixer(pa_lat, pa_ctx, lower_bounds[l], a_norm_g[l], seq, ctx_len, need_ctx)
        b_lat = _neighborhood_attention(pb_lat, pb_ctx, na_tables, na_types, seq, ctx_len)
        c_lat = _window_attention(pc_lat, pc_ctx, c_sink[l], seq, ctx_len)

        m_lat, (wfd,) = _merge(a_lat, b_lat, c_lat, g_lat, wbr, riders=[(w_ffn_down, l, None)])
        x_lat = _gated_residual_matmul(m_lat, wo, x_lat, mod_l, 2, seq, 512, "out_proj", a_buffers=2)
        h2 = _norm_modulate(x_lat, norm2_g[l], mod_l, 3, seq)
        next_in = [(w_in, l + 1, in_splits)] if l + 1 < depth else []
        u, nxt = _swiglu_up(h2, wfg, wfu, riders=next_in)
        if nxt:
            w_in_cast = nxt[0]
        x_lat = _gated_residual_matmul_ksplit(u, wfd, x_lat, mod_l, 5, seq, FFN_TN, "ffn_down")

        if need_ctx:
            g_ctx, _ = _project(h_ctx, w_g, bf16, 1024, sigmoid=True, name="proj_gates")
            b_ctx = _context_attention(pb_ctx, ctx_len, B_HEADS, B_DH, bk_cols, bv_cols)
            c_ctx_o = _context_attention(pc_ctx, ctx_len, C_HEADS, C_DH, ck_cols, cv_cols, sink=c_sink[l])
            m_ctx, _ = _merge(a_ctx, b_ctx, c_ctx_o, g_ctx, wbr)
            x_ctx = _gated_residual_matmul(m_ctx, wo, x_ctx, mod_c, 2, n_ctx, 1024, "out_proj")
            h2c = _norm_modulate(x_ctx, norm2_g[l], mod_c, 3, n_ctx)
            uc, _ = _swiglu_up(h2c, wfg, wfu)
            x_ctx = _gated_residual_matmul(uc, wfd, x_ctx, mod_c, 5, n_ctx, FFN_TN, "ffn_down")

    return _final_norm(x_lat, final_norm_g).reshape(batch, seq, d)
```

```python
import functools

import jax
import jax.numpy as jnp
import numpy as np
from jax import lax
from jax.experimental import pallas as pl
from jax.experimental.pallas import tpu as pltpu

GRID_W = 64
EPS = 1e-6
NEG_INF = -1e30
N_MOD = 6
A_HEADS, A_DK, A_DV, A_CHUNK = 16, 128, 128, 32
A_QK = A_HEADS * A_DK
A_WIDTH = A_HEADS * A_DV
B_HEADS, B_DH = 8, 128
B_WIDTH = B_HEADS * B_DH
NA_ROWS, NA_COLS = 8, 16
C_HEADS, C_KV_HEADS, C_DH = 16, 2, 64
C_WIDTH = C_HEADS * C_DH
C_KV_WIDTH = C_KV_HEADS * C_DH
C_WINDOW = 128
C_BLOCK = 128
ROPE_BASE = 10000.0
N_BRANCH = 3

LANES = 128
VMEM_LIMIT_BYTES = 56 * 1024 * 1024

HGRN_BLOCK = 256
HGRN_CHUNK = 2 * A_CHUNK
HGRN_HEADS_PER_STEP = 16
NA_QROWS = 4
NA_UNION = NA_ROWS + NA_QROWS
NA_HEADS_PER_STEP = 8
NORM_ROWS = 512
FFN_TN = 256
CAST_BLOCK_BYTES = 4 * 1024 * 1024

C_PAIRS = C_WIDTH // LANES
C_EXP = 2 * C_KV_HEADS * LANES
C_OUT_WIDTH = C_WIDTH + 2 * C_EXP

_NT = (((1,), (1,)), ((), ()))
_TN = (((0,), (0,)), ((), ()))

bf16 = jnp.bfloat16
f32 = jnp.float32


def _params(*sem):
    return pltpu.CompilerParams(dimension_semantics=sem, vmem_limit_bytes=VMEM_LIMIT_BYTES)


def _row_tile(rows, want):
    t = min(rows, want)
    while rows % t:
        t //= 2
    return t


def _sigmoid(x):
    return 1.0 / (1.0 + jnp.exp(-x))


def _silu(x):
    return x * _sigmoid(x)


def _sigmoid_tanh(x):
    return 0.5 * jnp.tanh(0.5 * x) + 0.5


def _col_chunks(n, width=256):
    width = min(width, n)
    return [slice(c, c + width) for c in range(0, n, width)]


def _mod_kernel(c_ref, w_ref, b_ref, o_ref):
    a = _silu(c_ref[...]).astype(bf16)
    o_ref[...] = jnp.dot(a, w_ref[...].astype(bf16), preferred_element_type=f32) + b_ref[...]


def _modulation(c_rows, w_mod, b_mod):
    depth, d, n = w_mod.shape
    rows = c_rows.shape[0]
    tn = 512
    return pl.pallas_call(
        _mod_kernel,
        grid=(depth, n // tn),
        in_specs=[
            pl.BlockSpec((rows, d), lambda l, j: (0, 0)),
            pl.BlockSpec((None, d, tn), lambda l, j: (l, 0, j)),
            pl.BlockSpec((None, 1, tn), lambda l, j: (l, 0, j)),
        ],
        out_specs=pl.BlockSpec((None, rows, tn), lambda l, j: (l, 0, j)),
        out_shape=jax.ShapeDtypeStruct((depth, rows, n), f32),
        compiler_params=_params("arbitrary", "arbitrary"),
        name="modulation",
    )(c_rows, w_mod, b_mod.reshape(depth, 1, n))


def _norm_mod_kernel(x_ref, g_ref, shift_ref, scale_ref, o_ref):
    x = x_ref[...]
    y = x * lax.rsqrt(jnp.mean(x * x, axis=-1, keepdims=True) + EPS) * g_ref[...]
    o_ref[...] = (y * (1.0 + scale_ref[...]) + shift_ref[...]).astype(o_ref.dtype)


def _norm_kernel(x_ref, g_ref, o_ref):
    x = x_ref[...]
    y = x * lax.rsqrt(jnp.mean(x * x, axis=-1, keepdims=True) + EPS) * g_ref[...]
    o_ref[...] = y.astype(o_ref.dtype)


def _norm_modulate(x, g, mod, shift_idx, rows_per_group):
    rows, d = x.shape
    tm = _row_tile(rows_per_group, NORM_ROWS)
    per = rows_per_group // tm
    return pl.pallas_call(
        _norm_mod_kernel,
        grid=(rows // tm,),
        in_specs=[
            pl.BlockSpec((tm, d), lambda i: (i, 0)),
            pl.BlockSpec((1, d), lambda i: (0, 0)),
            pl.BlockSpec((None, 1, d), lambda i: (i // per, 0, shift_idx)),
            pl.BlockSpec((None, 1, d), lambda i: (i // per, 0, shift_idx + 1)),
        ],
        out_specs=pl.BlockSpec((tm, d), lambda i: (i, 0)),
        out_shape=jax.ShapeDtypeStruct((rows, d), bf16),
        compiler_params=_params("arbitrary"),
        name="norm_modulate",
    )(x, g.reshape(1, d), mod, mod)


def _final_norm(x, g):
    rows, d = x.shape
    tm = _row_tile(rows, NORM_ROWS)
    return pl.pallas_call(
        _norm_kernel,
        grid=(rows // tm,),
        in_specs=[pl.BlockSpec((tm, d), lambda i: (i, 0)), pl.BlockSpec((1, d), lambda i: (0, 0))],
        out_specs=pl.BlockSpec((tm, d), lambda i: (i, 0)),
        out_shape=jax.ShapeDtypeStruct((rows, d), f32),
        compiler_params=_params("arbitrary"),
        name="final_norm",
    )(x, g.reshape(1, d))


def _proj_kernel(h_ref, w_ref, o_ref):
    h = h_ref[...]
    for cs in _col_chunks(o_ref.shape[1]):
        o_ref[:, cs] = jnp.dot(h, w_ref[:, cs], preferred_element_type=f32).astype(o_ref.dtype)


def _proj_scale_kernel(h_ref, w_ref, s_ref, o_ref):
    h = h_ref[...]
    for cs in _col_chunks(o_ref.shape[1]):
        acc = jnp.dot(h, w_ref[:, cs], preferred_element_type=f32)
        o_ref[:, cs] = (acc * s_ref[:, cs]).astype(o_ref.dtype)


def _proj_sigmoid_kernel(h_ref, w_ref, o_ref):
    h = h_ref[...]
    for cs in _col_chunks(o_ref.shape[1]):
        acc = jnp.dot(h, w_ref[:, cs], preferred_element_type=f32)
        o_ref[:, cs] = _sigmoid_tanh(acc).astype(o_ref.dtype)


def _cast_kernel(w_ref, *o_refs, bounds):
    for o_ref, (lo, hi) in zip(o_refs, bounds):
        o_ref[...] = w_ref[:, lo:hi].astype(o_ref.dtype)


def _cast_weight(w, layer, splits=None):
    _, k, n = w.shape
    bounds = tuple(splits) if splits else ((0, n),)
    tk = 1 << ((CAST_BLOCK_BYTES // (4 * n)).bit_length() - 1)
    while k % tk:
        tk //= 2
    outs = pl.pallas_call(
        functools.partial(_cast_kernel, bounds=bounds),
        grid=(k // tk,),
        in_specs=[pl.BlockSpec((None, tk, n), lambda i: (layer, i, 0))],
        out_specs=[pl.BlockSpec((tk, hi - lo), lambda i: (i, 0)) for lo, hi in bounds],
        out_shape=[jax.ShapeDtypeStruct((k, hi - lo), bf16) for lo, hi in bounds],
        compiler_params=_params("arbitrary"),
        name="cast_weight",
    )(w)
    return outs if splits else outs[0]


class _Riders:
    def __init__(self, riders, steps, step_of):
        self.args, self.in_specs, self.out_specs, self.out_shapes, self.bounds = [], [], [], [], []
        self.plan = []
        for w, layer, splits in riders:
            _, k, n = w.shape
            rows = 16
            while rows < k and (k % rows or k // rows > steps):
                rows *= 2
            carried = k % rows == 0 and k // rows <= steps
            self.plan.append((carried, w, layer, splits))
            if not carried:
                continue
            last = k // rows - 1

            def blk(*g, last=last):
                return jnp.minimum(step_of(*g), last)

            bounds = tuple(splits) if splits else ((0, n),)
            self.args.append(w)
            self.in_specs.append(pl.BlockSpec((None, rows, n), lambda *g, b=blk, l=layer: (l, b(*g), 0)))
            for lo, hi in bounds:
                self.out_specs.append(pl.BlockSpec((rows, hi - lo), lambda *g, b=blk: (b(*g), 0)))
                self.out_shapes.append(jax.ShapeDtypeStruct((k, hi - lo), bf16))
            self.bounds.append(bounds)

    def wrap(self, body, n_in, n_out):
        n_src = len(self.bounds)
        n_dst = len(self.out_specs)
        bounds = self.bounds

        def kern(*refs):
            ins, srcs = refs[:n_in], refs[n_in:n_in + n_src]
            o0 = n_in + n_src
            outs, dsts = refs[o0:o0 + n_out], iter(refs[o0 + n_out:o0 + n_out + n_dst])
            body(*ins, *outs, *refs[o0 + n_out + n_dst:])
            for src, bnd in zip(srcs, bounds):
                for lo, hi in bnd:
                    dst = next(dsts)
                    dst[...] = src[:, lo:hi].astype(dst.dtype)

        return kern

    def split(self, results, n_out):
        host, rest = results[:n_out], list(results[n_out:])
        per = []
        for carried, w, layer, splits in self.plan:
            if carried:
                count = len(splits) if splits else 1
                got, rest = rest[:count], rest[count:]
                per.append(got[0] if not splits else tuple(got))
            else:
                got = _cast_weight(w, layer, splits)
                per.append(tuple(got) if splits else got)
        return host, per


def _project(h, w, out_dtype, tn, *, tm_want=1024, h_buffers=2, col_scale=None, sigmoid=False,
             riders=(), name="proj"):
    rows, d = h.shape
    n = w.shape[1]
    nj = n // tn
    tm = _row_tile(rows, tm_want)
    single = dict(pipeline_mode=pl.Buffered(1)) if h_buffers == 1 else {}
    h_spec = pl.BlockSpec((tm, d), lambda i, j: (i, 0), **single)
    w_spec = pl.BlockSpec((d, tn), lambda i, j: (0, j))
    args, specs = [h, w], [h_spec, w_spec]
    if sigmoid:
        kern = _proj_sigmoid_kernel
    elif col_scale is not None:
        kern = _proj_scale_kernel
        args.append(col_scale)
        specs.append(pl.BlockSpec((1, tn), lambda i, j: (0, j)))
    else:
        kern = _proj_kernel
    ride = _Riders(riders, (rows // tm) * nj, lambda i, j: i * nj + j)
    res = pl.pallas_call(
        ride.wrap(kern, len(args), 1),
        grid=(rows // tm, nj),
        in_specs=specs + ride.in_specs,
        out_specs=[pl.BlockSpec((tm, tn), lambda i, j: (i, j))] + ride.out_specs,
        out_shape=[jax.ShapeDtypeStruct((rows, n), out_dtype)] + ride.out_shapes,
        compiler_params=_params("arbitrary", "arbitrary"),
        name=name,
    )(*args, *ride.args)
    (out,), extra = ride.split(res, 1)
    return out, extra


def _rope_rotate(x, cos, sin):
    n = x.shape[-1]
    lane = lax.broadcasted_iota(jnp.int32, x.shape, x.ndim - 1)
    up = pltpu.roll(x, n - 16, x.ndim - 1)
    down = pltpu.roll(x, 16, x.ndim - 1)
    return x * cos + jnp.where((lane % 32) < 16, up, down) * sin


def _proj_window_kernel(h_ref, w_ref, cos_ref, sin_ref, o_ref):
    acc = jnp.dot(h_ref[...], w_ref[...], preferred_element_type=f32)
    cos = cos_ref[...]
    sin = sin_ref[...]
    for j in range(C_PAIRS):
        x = _rope_rotate(acc[:, j * LANES:(j + 1) * LANES], cos, sin) * (C_DH ** -0.5)
        o_ref[:, j * LANES:(j + 1) * LANES] = x.astype(o_ref.dtype)
    k = _rope_rotate(acc[:, C_WIDTH:C_WIDTH + LANES], cos, sin)
    v = acc[:, C_WIDTH + LANES:C_WIDTH + 2 * LANES]
    low = lax.broadcasted_iota(jnp.int32, k.shape, 1) < C_DH
    for idx, x in enumerate((k, v)):
        swapped = pltpu.roll(x, C_DH, 1)
        groups = (jnp.where(low, x, 0.0), jnp.where(low, 0.0, swapped),
                  jnp.where(low, swapped, 0.0), jnp.where(low, 0.0, x))
        base = C_WIDTH + idx * C_EXP
        for c, val in enumerate(groups):
            o_ref[:, base + c * LANES:base + (c + 1) * LANES] = val.astype(o_ref.dtype)


def _project_window(h, w, cos, sin):
    rows, d = h.shape
    n = w.shape[1]
    tm = _row_tile(rows, 1024)
    tok_tiles = cos.shape[0] // tm
    t_spec = pl.BlockSpec((tm, LANES), lambda i: (i % tok_tiles, 0))
    return pl.pallas_call(
        _proj_window_kernel,
        grid=(rows // tm,),
        in_specs=[pl.BlockSpec((tm, d), lambda i: (i, 0)),
                  pl.BlockSpec((d, n), lambda i: (0, 0), pipeline_mode=pl.Buffered(1)),
                  t_spec, t_spec],
        out_specs=pl.BlockSpec((tm, C_OUT_WIDTH), lambda i: (i, 0)),
        out_shape=jax.ShapeDtypeStruct((rows, C_OUT_WIDTH), bf16),
        compiler_params=_params("arbitrary"),
        name="proj_window",
    )(h, w, cos, sin)


def _hgrn_kernel(*refs, rev, has_s0, emit_state, readout, hp):
    it = iter(refs)
    q_ref, v_ref, f_ref, lb_ref = next(it), next(it), next(it), next(it)
    s0_ref = next(it) if has_s0 else None
    if readout:
        g_ref, oprev_ref, ng_ref = next(it), next(it), next(it)
    o_ref = next(it)
    sout_ref = next(it) if emit_state else None
    st_ref = next(it)

    blk = pl.program_id(2)
    nblk = pl.num_programs(2)
    tb = q_ref.shape[0]
    chunk = HGRN_CHUNK
    nchunk = tb // chunk
    width = hp * A_DK
    heads = range(hp)

    def head(a, h):
        return a[:, h * A_DK:(h + 1) * A_DK]

    @pl.when(blk == 0)
    def _():
        if has_s0:
            st_ref[...] = s0_ref[...]
        else:
            st_ref[...] = jnp.zeros_like(st_ref)

    lb = lb_ref[...]
    f = lb + (1.0 - lb) * _sigmoid(f_ref[...])
    logf = jnp.log(f)
    k = 1.0 - f

    row = lax.broadcasted_iota(jnp.int32, (tb, tb), 0)
    col = lax.broadcasted_iota(jnp.int32, (tb, tb), 1)
    same = (row // chunk) == (col // chunk)
    causal = (col >= row) if rev else (col <= row)
    mask = same & causal
    tri = jnp.where(mask, 1.0, 0.0).astype(bf16)

    hi = logf.astype(bf16)
    lo = (logf - hi.astype(f32)).astype(bf16)
    cum2 = jnp.dot(tri, jnp.concatenate([hi, lo], axis=1), preferred_element_type=f32)
    cum = cum2[:, :width] + cum2[:, width:]

    mid_rows, mid_decay = [], []
    for j in range(nchunk):
        mid = j * chunk + (chunk // 2 if rev else chunk // 2 - 1)
        at_mid = cum[mid:mid + 1, :]
        mid_rows.append(jnp.broadcast_to(at_mid, (chunk, width)))
        mid_decay.append(jnp.broadcast_to(jnp.exp(at_mid), (chunk, width)))
    rel = cum - jnp.concatenate(mid_rows, axis=0)
    q_rel = _silu(q_ref[...]) * jnp.exp(rel)
    q_att = q_rel.astype(bf16)
    k_inv = (k * jnp.exp(-rel)).astype(bf16)
    q_dec = (q_rel * jnp.concatenate(mid_decay, axis=0)).astype(bf16)
    v = v_ref[...].astype(bf16)
    att = [lax.dot_general(head(q_att, h), head(k_inv, h), _NT, preferred_element_type=f32)
           for h in heads]
    att = [jnp.where(mask, a, 0.0).astype(bf16) for a in att]
    o_intra = [jnp.dot(att[h], head(v, h), preferred_element_type=f32) for h in heads]

    order = range(nchunk - 1, -1, -1) if rev else range(nchunk)
    k_end, dec = {}, {}
    for j in order:
        sl = slice(j * chunk, (j + 1) * chunk)
        last = j * chunk if rev else (j + 1) * chunk - 1
        tot = cum[last:last + 1, :]
        k_end[j] = (k[sl] * jnp.exp(tot - cum[sl])).astype(bf16)
        dec[j] = jnp.exp(tot)
    zero = jnp.zeros((chunk, A_DK), bf16)
    upd = {}
    for h in heads:
        k_diag = jnp.concatenate(
            [jnp.concatenate([head(k_end[j], h) if c == j else zero for c in range(nchunk)], axis=1)
             for j in range(nchunk)], axis=0)
        u_all = lax.dot_general(head(v, h), k_diag, _TN, preferred_element_type=f32)
        for j in range(nchunk):
            upd[h, j] = u_all[:, j * A_DK:(j + 1) * A_DK]

    before = {}
    final = []
    for h in heads:
        s = st_ref[h]
        for j in order:
            before[h, j] = s.astype(bf16)
            s = s * head(dec[j], h) + upd[h, j]
        st_ref[h] = s
        final.append(s)

    outs = []
    for h in heads:
        o_inter = [lax.dot_general(head(q_dec[j * chunk:(j + 1) * chunk], h), before[h, j], _NT,
                                   preferred_element_type=f32) for j in range(nchunk)]
        outs.append(o_intra[h] + jnp.concatenate(o_inter, axis=0))

    if readout:
        ng = ng_ref[...]
        normed = []
        for h in heads:
            o = outs[h] + head(oprev_ref[...], h)
            normed.append(o * lax.rsqrt(jnp.mean(o * o, axis=-1, keepdims=True) + EPS) * ng)
        o_ref[...] = (jnp.concatenate(normed, axis=1) * _silu(g_ref[...])).astype(o_ref.dtype)
    else:
        o_ref[...] = jnp.concatenate(outs, axis=1)

    if emit_state:
        @pl.when(blk == nblk - 1)
        def _():
            for h in heads:
                sout_ref[h] = final[h]


def _hgrn_scan(proj, lb_dir, seq, direction, *, s0=None, emit_state=False, readout=None):
    rows = proj.shape[0]
    batch = rows // seq
    tb = min(HGRN_BLOCK, seq)
    nblk = seq // tb
    rev = direction == 1
    hp = HGRN_HEADS_PER_STEP
    hblocks = A_HEADS // hp
    width = hp * A_DK

    def tok(b, h, i):
        return b * nblk + (nblk - 1 - i if rev else i)

    def col_spec(group):
        return pl.BlockSpec((tb, width), lambda b, h, i: (tok(b, h, i), group * hblocks + h))

    state_spec = pl.BlockSpec((None, hp, A_DV, A_DK), lambda b, h, i: (b, h, 0, 0))
    args = [proj, proj, proj, lb_dir.reshape(1, A_QK)]
    specs = [col_spec(0), col_spec(1), col_spec(2 + direction),
             pl.BlockSpec((1, width), lambda b, h, i: (0, h))]
    if s0 is not None:
        args.append(s0)
        specs.append(state_spec)
    if readout is not None:
        o_prev, norm_g = readout
        args += [proj, o_prev, norm_g.reshape(1, A_DV)]
        specs += [col_spec(4),
                  pl.BlockSpec((tb, width), lambda b, h, i: (tok(b, h, i), h)),
                  pl.BlockSpec((1, A_DV), lambda b, h, i: (0, 0))]
    out_shape = [jax.ShapeDtypeStruct((rows, A_WIDTH), bf16 if readout is not None else f32)]
    out_specs = [pl.BlockSpec((tb, width), lambda b, h, i: (tok(b, h, i), h))]
    if emit_state:
        out_shape.append(jax.ShapeDtypeStruct((batch, A_HEADS, A_DV, A_DK), f32))
        out_specs.append(state_spec)
    kern = functools.partial(_hgrn_kernel, rev=rev, has_s0=s0 is not None,
                             emit_state=emit_state, readout=readout is not None, hp=hp)
    res = pl.pallas_call(
        kern,
        grid=(batch, hblocks, nblk),
        in_specs=specs,
        out_specs=out_specs,
        out_shape=out_shape,
        scratch_shapes=[pltpu.VMEM((hp, A_DV, A_DK), f32)],
        compiler_params=_params("arbitrary", "arbitrary", "arbitrary"),
        name="hgrn_scan",
    )(*args)
    return res if emit_state else res[0]


def _hgrn_mixer(proj_lat, proj_ctx, lb, norm_g, seq, ctx_len, need_ctx):
    o_c_f, s_f = _hgrn_scan(proj_ctx, lb[0], ctx_len, 0, emit_state=True)
    o_l_f = _hgrn_scan(proj_lat, lb[0], seq, 0, s0=s_f)
    if need_ctx:
        a_ctx, s_b = _hgrn_scan(proj_ctx, lb[1], ctx_len, 1, emit_state=True, readout=(o_c_f, norm_g))
    else:
        _, s_b = _hgrn_scan(proj_ctx, lb[1], ctx_len, 1, emit_state=True)
        a_ctx = None
    a_lat = _hgrn_scan(proj_lat, lb[1], seq, 1, s0=s_b, readout=(o_l_f, norm_g))
    return a_lat, a_ctx


def _attend(parts, sink=None):
    m = parts[0][0].max(axis=-1, keepdims=True)
    for s, _ in parts[1:]:
        m = jnp.maximum(m, s.max(axis=-1, keepdims=True))
    if sink is not None:
        m = jnp.maximum(m, sink)
    den = None
    acc = None
    for s, v in parts:
        e = jnp.exp(s - m)
        d = e.sum(axis=-1, keepdims=True)
        o = jnp.dot(e.astype(bf16), v, preferred_element_type=f32)
        den = d if den is None else den + d
        acc = o if acc is None else acc + o
    if sink is not None:
        den = den + jnp.exp(sink - m)
    return acc / den


def _na_kernel(types_ref, q_ref, k_ref, v_ref, kc_ref, vc_ref, tbl_ref, o_ref, *, grid_rows, hp):
    del types_ref
    step = pl.program_id(2)
    span = NA_UNION * GRID_W
    first = jnp.clip(step * NA_QROWS - NA_ROWS // 2, 0, grid_rows - NA_UNION)
    start = pl.multiple_of(first * GRID_W, GRID_W)
    for h in range(hp):
        hs = slice(h * B_DH, (h + 1) * B_DH)
        q = q_ref[:, hs]
        kn = k_ref[pl.ds(start, span), hs]
        vn = v_ref[pl.ds(start, span), hs]
        s_nb = lax.dot_general(q, kn, _NT, preferred_element_type=f32) + tbl_ref[h]
        s_cx = lax.dot_general(q, kc_ref[:, hs], _NT, preferred_element_type=f32)
        o = _attend([(s_nb, vn), (s_cx, vc_ref[:, hs])])
        o_ref[:, hs] = o.astype(o_ref.dtype)


def _na_tables(rpb, grid_rows):
    assert grid_rows >= NA_UNION and grid_rows % NA_QROWS == 0
    col = np.arange(GRID_W)
    col_off = np.clip(col[None, :] - col[:, None] + NA_COLS - 1, 0, 2 * NA_COLS - 2)
    col_start = np.clip(col - NA_COLS // 2, 0, GRID_W - NA_COLS)
    col_ok = (col[None, :] >= col_start[:, None]) & (col[None, :] < col_start[:, None] + NA_COLS)
    seen, types = {}, []
    for i in range(grid_rows // NA_QROWS):
        first = int(np.clip(i * NA_QROWS - NA_ROWS // 2, 0, grid_rows - NA_UNION))
        key_row = first + np.arange(NA_UNION)[None, :]
        r = i * NA_QROWS + np.arange(NA_QROWS)[:, None]
        row_start = np.clip(r - NA_ROWS // 2, 0, grid_rows - NA_ROWS)
        ok = (key_row >= row_start) & (key_row < row_start + NA_ROWS)
        assert (ok.sum(axis=1) == NA_ROWS).all()
        off = np.where(ok, key_row - r + NA_ROWS - 1, 0)
        sig = (ok.tobytes(), off.tobytes())
        if sig not in seen:
            seen[sig] = (len(seen), ok, off)
        types.append(seen[sig][0])
    toe = jnp.where(col_ok[None, None], rpb.astype(f32)[:, :, col_off], NEG_INF)
    neg = jnp.full((rpb.shape[0], GRID_W, GRID_W), NEG_INF, f32)
    tables = []
    for _, ok, off in sorted(seen.values(), key=lambda t: t[0]):
        rows = [jnp.concatenate([toe[:, off[rl, a]] if ok[rl, a] else neg for a in range(NA_UNION)],
                                axis=-1) for rl in range(NA_QROWS)]
        tables.append(jnp.concatenate(rows, axis=-2))
    return jnp.stack(tables, axis=1), jnp.asarray(np.array(types, np.int32))


def _neighborhood_attention(qkv_lat, qkv_ctx, tables, types, seq, ctx_len):
    rows = qkv_lat.shape[0]
    batch = rows // seq
    tq = NA_QROWS * GRID_W
    nq = seq // tq
    hp = NA_HEADS_PER_STEP
    hb = B_HEADS // hp
    width = hp * B_DH
    kern = functools.partial(_na_kernel, grid_rows=seq // GRID_W, hp=hp)
    grid_spec = pltpu.PrefetchScalarGridSpec(
        num_scalar_prefetch=1,
        grid=(batch, hb, nq),
        in_specs=[
            pl.BlockSpec((tq, width), lambda b, h, i, t: (b * nq + i, h)),
            pl.BlockSpec((seq, width), lambda b, h, i, t: (b, hb + h), pipeline_mode=pl.Buffered(1)),
            pl.BlockSpec((seq, width), lambda b, h, i, t: (b, 2 * hb + h), pipeline_mode=pl.Buffered(1)),
            pl.BlockSpec((ctx_len, width), lambda b, h, i, t: (b, hb + h)),
            pl.BlockSpec((ctx_len, width), lambda b, h, i, t: (b, 2 * hb + h)),
            pl.BlockSpec((hp, None, tq, NA_UNION * GRID_W), lambda b, h, i, t: (h, t[i], 0, 0)),
        ],
        out_specs=pl.BlockSpec((tq, width), lambda b, h, i, t: (b * nq + i, h)),
    )
    return pl.pallas_call(
        kern,
        grid_spec=grid_spec,
        out_shape=jax.ShapeDtypeStruct((rows, B_WIDTH), bf16),
        compiler_params=_params("arbitrary", "arbitrary", "arbitrary"),
        name="neighborhood_attention",
    )(types, qkv_lat, qkv_lat, qkv_lat, qkv_ctx, qkv_ctx, tables)


def _window_kernel(sink_ref, q_ref, k_ref, v_ref, kc_ref, vc_ref, o_ref, *, seq):
    n = pl.program_id(1)
    span = 3 * C_BLOCK
    start = pl.multiple_of(jnp.clip((n - 1) * C_BLOCK, 0, seq - span), C_BLOCK)
    pairs = C_PAIRS // C_KV_HEADS
    rows = pairs * C_BLOCK
    qpos = n * C_BLOCK + lax.broadcasted_iota(jnp.int32, (C_BLOCK, span), 0)
    kpos = start + lax.broadcasted_iota(jnp.int32, (C_BLOCK, span), 1)
    off_window = jnp.where(jnp.abs(qpos - kpos) <= C_WINDOW, 0.0, NEG_INF)
    off_window = jnp.concatenate([off_window] * pairs, axis=0)
    pair_id = lax.broadcasted_iota(jnp.int32, (rows, 1), 0) // C_BLOCK

    chains = [(g, e) for g in range(C_KV_HEADS) for e in range(2)]
    q = {g: jnp.concatenate([q_ref[:, (g * pairs + p) * LANES:(g * pairs + p + 1) * LANES]
                             for p in range(pairs)], axis=0) for g in range(C_KV_HEADS)}
    s_w, s_c, sink = {}, {}, {}
    for g, e in chains:
        cs = slice((2 * g + e) * LANES, (2 * g + e + 1) * LANES)
        s = lax.dot_general(q[g], k_ref[pl.ds(start, span), cs], _NT, preferred_element_type=f32)
        s_w[g, e] = s + off_window
        s_c[g, e] = lax.dot_general(q[g], kc_ref[:, cs], _NT, preferred_element_type=f32)
        col = jnp.zeros((rows, 1), f32)
        for p in range(pairs):
            col = jnp.where(pair_id == p, sink_ref[(g * pairs + p) * 2 + e], col)
        sink[g, e] = col
    out = {}
    for g, e in chains:
        cs = slice((2 * g + e) * LANES, (2 * g + e + 1) * LANES)
        out[g, e] = _attend([(s_w[g, e], v_ref[pl.ds(start, span), cs]), (s_c[g, e], vc_ref[:, cs])],
                            sink=sink[g, e])
    for g in range(C_KV_HEADS):
        o = out[g, 0] + out[g, 1]
        for p in range(pairs):
            o_ref[:, (g * pairs + p) * LANES:(g * pairs + p + 1) * LANES] = (
                o[p * C_BLOCK:(p + 1) * C_BLOCK].astype(o_ref.dtype))


def _window_attention(qkv_lat, qkv_ctx, sink, seq, ctx_len):
    rows = qkv_lat.shape[0]
    batch = rows // seq
    nq = seq // C_BLOCK
    kblk = C_WIDTH // C_EXP
    kern = functools.partial(_window_kernel, seq=seq)
    resident = dict(pipeline_mode=pl.Buffered(1))
    return pl.pallas_call(
        kern,
        grid=(batch, nq),
        in_specs=[
            pl.BlockSpec(memory_space=pltpu.SMEM),
            pl.BlockSpec((C_BLOCK, C_WIDTH), lambda b, i: (b * nq + i, 0)),
            pl.BlockSpec((seq, C_EXP), lambda b, i: (b, kblk), **resident),
            pl.BlockSpec((seq, C_EXP), lambda b, i: (b, kblk + 1), **resident),
            pl.BlockSpec((ctx_len, C_EXP), lambda b, i: (b, kblk)),
            pl.BlockSpec((ctx_len, C_EXP), lambda b, i: (b, kblk + 1)),
        ],
        out_specs=pl.BlockSpec((C_BLOCK, C_WIDTH), lambda b, i: (b * nq + i, 0)),
        out_shape=jax.ShapeDtypeStruct((rows, C_WIDTH), bf16),
        compiler_params=_params("arbitrary", "arbitrary"),
        name="window_attention",
    )(sink.astype(f32), qkv_lat, qkv_lat, qkv_lat, qkv_ctx, qkv_ctx)


def _ctx_attn_kernel(*refs, heads, dh, k_cols, v_cols, has_sink):
    if has_sink:
        sink_ref, qkv_ref, o_ref = refs
    else:
        qkv_ref, o_ref = refs
    group = heads // len(k_cols)
    for kh, (kc, vc) in enumerate(zip(k_cols, v_cols)):
        k = qkv_ref[:, kc:kc + dh]
        v = qkv_ref[:, vc:vc + dh]
        for g in range(group):
            hq = kh * group + g
            q = qkv_ref[:, hq * dh:(hq + 1) * dh]
            s = lax.dot_general(q, k, _NT, preferred_element_type=f32)
            o = _attend([(s, v)], sink=sink_ref[hq] if has_sink else None)
            o_ref[:, hq * dh:(hq + 1) * dh] = o.astype(o_ref.dtype)


def _context_attention(qkv_ctx, ctx_len, heads, dh, k_cols, v_cols, sink=None):
    rows, width = qkv_ctx.shape
    kern = functools.partial(_ctx_attn_kernel, heads=heads, dh=dh, k_cols=k_cols, v_cols=v_cols,
                             has_sink=sink is not None)
    args, specs = [qkv_ctx], [pl.BlockSpec((ctx_len, width), lambda b: (b, 0))]
    if sink is not None:
        args.insert(0, sink.astype(f32))
        specs.insert(0, pl.BlockSpec(memory_space=pltpu.SMEM))
    return pl.pallas_call(
        kern,
        grid=(rows // ctx_len,),
        in_specs=specs,
        out_specs=pl.BlockSpec((ctx_len, heads * dh), lambda b: (b, 0)),
        out_shape=jax.ShapeDtypeStruct((rows, heads * dh), bf16),
        compiler_params=_params("arbitrary"),
        name="context_attention",
    )(*args)


def _merge_kernel(oa_ref, ob_ref, oc_ref, wa_ref, wb_ref, wc_ref, ga_ref, gb_ref, gc_ref, o_ref):
    oa, ob, oc = oa_ref[...], ob_ref[...], oc_ref[...]
    for cs in _col_chunks(o_ref.shape[1]):
        br_a = jnp.dot(oa, wa_ref[:, cs], preferred_element_type=f32)
        br_b = jnp.dot(ob, wb_ref[:, cs], preferred_element_type=f32)
        br_c = jnp.dot(oc, wc_ref[:, cs], preferred_element_type=f32)
        m = ga_ref[:, cs] * br_a + gb_ref[:, cs] * br_b + gc_ref[:, cs] * br_c
        o_ref[:, cs] = m.astype(o_ref.dtype)


def _merge(o_a, o_b, o_c, gates, w_branch, riders=()):
    rows = o_a.shape[0]
    d = w_branch.shape[1]
    tm = _row_tile(rows, 1024)
    tn = 512
    nj = d // tn
    assert A_WIDTH % B_WIDTH == 0 and B_WIDTH == C_WIDTH
    b_blk = A_WIDTH // B_WIDTH
    ride = _Riders(riders, (rows // tm) * nj, lambda i, j: i * nj + j)
    res = pl.pallas_call(
        ride.wrap(_merge_kernel, 9, 1),
        grid=(rows // tm, nj),
        in_specs=[
            pl.BlockSpec((tm, A_WIDTH), lambda i, j: (i, 0)),
            pl.BlockSpec((tm, B_WIDTH), lambda i, j: (i, 0)),
            pl.BlockSpec((tm, C_WIDTH), lambda i, j: (i, 0)),
            pl.BlockSpec((A_WIDTH, tn), lambda i, j: (0, j)),
            pl.BlockSpec((B_WIDTH, tn), lambda i, j: (b_blk, j)),
            pl.BlockSpec((C_WIDTH, tn), lambda i, j: (b_blk + 1, j)),
            pl.BlockSpec((tm, tn), lambda i, j: (i, j)),
            pl.BlockSpec((tm, tn), lambda i, j: (i, nj + j)),
            pl.BlockSpec((tm, tn), lambda i, j: (i, 2 * nj + j)),
        ] + ride.in_specs,
        out_specs=[pl.BlockSpec((tm, tn), lambda i, j: (i, j))] + ride.out_specs,
        out_shape=[jax.ShapeDtypeStruct((rows, d), bf16)] + ride.out_shapes,
        compiler_params=_params("arbitrary", "arbitrary"),
        name="merge_branches",
    )(o_a, o_b, o_c, w_branch, w_branch, w_branch, gates, gates, gates, *ride.args)
    (out,), extra = ride.split(res, 1)
    return out, extra


def _row_halves(m):
    return [slice(0, m // 2), slice(m // 2, m)] if m % 32 == 0 else [slice(0, m)]


def _residual_kernel(a_ref, w_ref, x_ref, gate_ref, o_ref):
    if o_ref.shape[1] > 256:
        a = a_ref[...]
        for cs in _col_chunks(o_ref.shape[1]):
            y = jnp.dot(a, w_ref[:, cs], preferred_element_type=f32)
            o_ref[:, cs] = x_ref[:, cs] + gate_ref[:, cs] * y
    else:
        w = w_ref[...]
        for rs in _row_halves(o_ref.shape[0]):
            y = jnp.dot(a_ref[rs, :], w, preferred_element_type=f32)
            o_ref[rs, :] = x_ref[rs, :] + gate_ref[...] * y


def _gated_residual_matmul(a, w, x, mod, gate_idx, rows_per_group, tn, name, a_buffers=1):
    rows, kdim = a.shape
    d = w.shape[1]
    tm = _row_tile(rows_per_group, 1024)
    per = rows_per_group // tm
    nj = d // tn
    single = dict(pipeline_mode=pl.Buffered(1)) if a_buffers == 1 else {}
    return pl.pallas_call(
        _residual_kernel,
        grid=(rows // tm, nj),
        in_specs=[
            pl.BlockSpec((tm, kdim), lambda i, j: (i, 0), **single),
            pl.BlockSpec((kdim, tn), lambda i, j: (0, j)),
            pl.BlockSpec((tm, tn), lambda i, j: (i, j)),
            pl.BlockSpec((None, 1, tn), lambda i, j: (i // per, 0, gate_idx * nj + j)),
        ],
        out_specs=pl.BlockSpec((tm, tn), lambda i, j: (i, j)),
        out_shape=jax.ShapeDtypeStruct((rows, d), f32),
        compiler_params=_params("arbitrary", "arbitrary"),
        name=name,
    )(a, w, x, mod)


def _residual_ksplit_kernel(a0_ref, a1_ref, w0_ref, w1_ref, x_ref, gate_ref, o_ref):
    w0, w1 = w0_ref[...], w1_ref[...]
    for rs in _row_halves(o_ref.shape[0]):
        y = (jnp.dot(a0_ref[rs, :], w0, preferred_element_type=f32)
             + jnp.dot(a1_ref[rs, :], w1, preferred_element_type=f32))
        o_ref[rs, :] = x_ref[rs, :] + gate_ref[...] * y


def _gated_residual_matmul_ksplit(a, w, x, mod, gate_idx, rows_per_group, tn, name):
    rows, kdim = a.shape
    d = w.shape[1]
    kh = kdim // 2
    assert kdim % 2 == 0 and kh % LANES == 0
    tm = _row_tile(rows_per_group, 1024)
    per = rows_per_group // tm
    nj = d // tn
    return pl.pallas_call(
        _residual_ksplit_kernel,
        grid=(rows // tm, nj),
        in_specs=[
            pl.BlockSpec((tm, kh), lambda i, j: (i, 0), pipeline_mode=pl.Buffered(1)),
            pl.BlockSpec((tm, kh), lambda i, j: (i, 1)),
            pl.BlockSpec((kh, tn), lambda i, j: (0, j)),
            pl.BlockSpec((kh, tn), lambda i, j: (1, j)),
            pl.BlockSpec((tm, tn), lambda i, j: (i, j)),
            pl.BlockSpec((None, 1, tn), lambda i, j: (i // per, 0, gate_idx * nj + j)),
        ],
        out_specs=pl.BlockSpec((tm, tn), lambda i, j: (i, j)),
        out_shape=jax.ShapeDtypeStruct((rows, d), f32),
        compiler_params=_params("arbitrary", "arbitrary"),
        name=name,
    )(a, a, w, w, x, mod)


def _swiglu_kernel(h_ref, wg_ref, wu_ref, o_ref):
    wg, wu = wg_ref[...], wu_ref[...]
    for rs in _row_halves(o_ref.shape[0]):
        h = h_ref[rs, :]
        g = jnp.dot(h, wg, preferred_element_type=f32)
        u = jnp.dot(h, wu, preferred_element_type=f32)
        o_ref[rs, :] = (g * _sigmoid_tanh(g) * u).astype(o_ref.dtype)


def _swiglu_up(h, w_gate, w_up, riders=()):
    rows, d = h.shape
    n = w_gate.shape[1]
    tm = _row_tile(rows, 1024)
    tn = FFN_TN
    nj = n // tn
    w_spec = pl.BlockSpec((d, tn), lambda i, j: (0, j))
    ride = _Riders(riders, (rows // tm) * nj, lambda i, j: i * nj + j)
    res = pl.pallas_call(
        ride.wrap(_swiglu_kernel, 3, 1),
        grid=(rows // tm, nj),
        in_specs=[pl.BlockSpec((tm, d), lambda i, j: (i, 0)), w_spec, w_spec] + ride.in_specs,
        out_specs=[pl.BlockSpec((tm, tn), lambda i, j: (i, j))] + ride.out_specs,
        out_shape=[jax.ShapeDtypeStruct((rows, n), bf16)] + ride.out_shapes,
        compiler_params=_params("arbitrary", "arbitrary"),
        name="swiglu_up",
    )(h, w_gate, w_up, *ride.args)
    (out,), extra = ride.split(res, 1)
    return out, extra


def _rope_tables(seq):
    half = C_DH // 2
    pos = jnp.arange(seq)
    inv = ROPE_BASE ** (-jnp.arange(0, half, 2, dtype=f32) / half)
    ang_row = (pos // GRID_W).astype(f32)[:, None] * inv[None, :]
    ang_col = (pos % GRID_W).astype(f32)[:, None] * inv[None, :]

    def one(ang):
        return (jnp.concatenate([jnp.cos(ang), jnp.cos(ang)], axis=-1),
                jnp.concatenate([-jnp.sin(ang), jnp.sin(ang)], axis=-1))

    cr, sr = one(ang_row)
    cc, sc = one(ang_col)
    cos = jnp.concatenate([cr, cc], axis=-1)
    sin = jnp.concatenate([sr, sc], axis=-1)
    reps = LANES // C_DH
    return jnp.tile(cos, (1, reps)), jnp.tile(sin, (1, reps))


def kernel(x, c, ctx, c_ctx, norm1_g, norm2_g, w_mod, b_mod, w_in, hgrn_lb, a_norm_g, na_rpb,
           c_sink, w_branch, w_out, w_ffn_gate, w_ffn_up, w_ffn_down, final_norm_g):
    batch, seq, d = x.shape
    ctx_len = ctx.shape[1]
    depth = w_in.shape[0]
    n_lat, n_ctx = batch * seq, batch * ctx_len

    lb_w = jax.nn.softmax(hgrn_lb.astype(f32), axis=0)
    lower_bounds = jnp.cumsum(lb_w, axis=0) - lb_w[:1]

    mod_rows = 8 * (-(-(batch + 1) // 8))
    c_rows = jnp.zeros((mod_rows, d), f32).at[:batch].set(c).at[batch].set(c_ctx)
    mod_all = _modulation(c_rows, w_mod, b_mod)

    cos_t, sin_t = _rope_tables(seq)
    ones_t = jnp.ones((n_ctx, LANES), f32)
    zeros_t = jnp.zeros((n_ctx, LANES), f32)

    a_hi = 3 * A_QK + 2 * A_WIDTH
    b_hi = a_hi + 3 * B_WIDTH
    c_hi = b_hi + C_WIDTH + 2 * C_KV_WIDTH
    in_width = w_in.shape[2]
    b_scale = jnp.concatenate([jnp.full((B_WIDTH,), B_DH ** -0.5, f32),
                               jnp.ones((2 * B_WIDTH,), f32)]).reshape(1, -1)
    ck_cols = tuple(C_WIDTH + 2 * g * LANES for g in range(C_KV_HEADS))
    cv_cols = tuple(C_WIDTH + C_EXP + 2 * g * LANES for g in range(C_KV_HEADS))
    bk_cols = tuple(B_WIDTH + h * B_DH for h in range(B_HEADS))
    bv_cols = tuple(2 * B_WIDTH + h * B_DH for h in range(B_HEADS))

    x_lat = x.reshape(n_lat, d)
    x_ctx = ctx.reshape(n_ctx, d)

    in_splits = ((0, a_hi), (a_hi, b_hi), (b_hi, c_hi), (c_hi, in_width))
    w_in_cast = _cast_weight(w_in, 0, in_splits)

    for l in range(depth):
        need_ctx = l < depth - 1
        mod_l = mod_all[l, :batch].reshape(batch, 1, N_MOD * d)
        mod_c = mod_all[l, batch:batch + 1].reshape(1, 1, N_MOD * d)
        w_a, w_b, w_c, w_g = w_in_cast
        na_tables, na_types = _na_tables(na_rpb[l], seq // GRID_W)

        h_lat = _norm_modulate(x_lat, norm1_g[l], mod_l, 0, seq)
        h_ctx = _norm_modulate(x_ctx, norm1_g[l], mod_c, 0, n_ctx)

        pa_lat, (wfu,) = _project(h_lat, w_a, f32, 1024, riders=[(w_ffn_up, l, None)], name="proj_hgrn")
        pb_lat, (wbr, wo) = _project(h_lat, w_b, bf16, 1024, col_scale=b_scale,
                                     riders=[(w_branch, l, None), (w_out, l, None)], name="proj_na")
        pc_lat = _project_window(h_lat, w_c, cos_t, sin_t)
        g_lat, (wfg,) = _project(h_lat, w_g, bf16, 1024, sigmoid=True, riders=[(w_ffn_gate, l, None)],
                                 name="proj_gates")
        pa_ctx, _ = _project(h_ctx, w_a, f32, 1024, name="proj_hgrn")
        pb_ctx, _ = _project(h_ctx, w_b, bf16, 512, col_scale=b_scale, name="proj_na")
        pc_ctx = _project_window(h_ctx, w_c, ones_t, zeros_t)

        a_lat, a_ctx = _hgrn_mixer(pa_lat, pa_ctx, lower_bounds[l], a_norm_g[l], seq, ctx_len, need_ctx)
        b_lat = _neighborhood_attention(pb_lat, pb_ctx, na_tables, na_types, seq, ctx_len)
        c_lat = _window_attention(pc_lat, pc_ctx, c_sink[l], seq, ctx_len)

        m_lat, (wfd,) = _merge(a_lat, b_lat, c_lat, g_lat, wbr, riders=[(w_ffn_down, l, None)])
        x_lat = _gated_residual_matmul(m_lat, wo, x_lat, mod_l, 2, seq, 512, "out_proj", a_buffers=2)
        h2 = _norm_modulate(x_lat, norm2_g[l], mod_l, 3, seq)
        next_in = [(w_in, l + 1, in_splits)] if l + 1 < depth else []
        u, nxt = _swiglu_up(h2, wfg, wfu, riders=next_in)
        if nxt:
            w_in_cast = nxt[0]
        x_lat = _gated_residual_matmul_ksplit(u, wfd, x_lat, mod_l, 5, seq, FFN_TN, "ffn_down")

        if need_ctx:
            g_ctx, _ = _project(h_ctx, w_g, bf16, 1024, sigmoid=True, name="proj_gates")
            b_ctx = _context_attention(pb_ctx, ctx_len, B_HEADS, B_DH, bk_cols, bv_cols)
            c_ctx_o = _context_attention(pc_ctx, ctx_len, C_HEADS, C_DH, ck_cols, cv_cols, sink=c_sink[l])
            m_ctx, _ = _merge(a_ctx, b_ctx, c_ctx_o, g_ctx, wbr)
            x_ctx = _gated_residual_matmul(m_ctx, wo, x_ctx, mod_c, 2, n_ctx, 1024, "out_proj")
            h2c = _norm_modulate(x_ctx, norm2_g[l], mod_c, 3, n_ctx)
            uc, _ = _swiglu_up(h2c, wfg, wfu)
            x_ctx = _gated_residual_matmul(uc, wfd, x_ctx, mod_c, 5, n_ctx, FFN_TN, "ffn_down")

    return _final_norm(x_lat, final_norm_g).reshape(batch, seq, d)
```

```python
import functools

import jax
import jax.numpy as jnp
import numpy as np
from jax import lax
from jax.experimental import pallas as pl
from jax.experimental.pallas import tpu as pltpu

GRID_W = 64
EPS = 1e-6
NEG_INF = -1e30
N_MOD = 6
A_HEADS, A_DK, A_DV, A_CHUNK = 16, 128, 128, 32
A_QK = A_HEADS * A_DK
A_WIDTH = A_HEADS * A_DV
B_HEADS, B_DH = 8, 128
B_WIDTH = B_HEADS * B_DH
NA_ROWS, NA_COLS = 8, 16
C_HEADS, C_KV_HEADS, C_DH = 16, 2, 64
C_WIDTH = C_HEADS * C_DH
C_KV_WIDTH = C_KV_HEADS * C_DH
C_WINDOW = 128
C_BLOCK = 128
ROPE_BASE = 10000.0

LANES = 128
VMEM_LIMIT_BYTES = 56 * 1024 * 1024

HGRN_BLOCK = 256
HGRN_CHUNK = 2 * A_CHUNK
HGRN_HEADS_PER_STEP = 16
NA_QROWS = 4
NA_UNION = NA_ROWS + NA_QROWS
NA_HEADS_PER_STEP = 8
NORM_ROWS = 512
FFN_TN = 256
CAST_BLOCK_BYTES = 4 * 1024 * 1024

C_PAIRS = C_WIDTH // LANES
C_EXP = 2 * C_KV_HEADS * LANES
C_OUT_WIDTH = C_WIDTH + 2 * C_EXP

_NT = (((1,), (1,)), ((), ()))
_TN = (((0,), (0,)), ((), ()))

bf16 = jnp.bfloat16
f32 = jnp.float32


def _params(*sem):
    return pltpu.CompilerParams(dimension_semantics=sem, vmem_limit_bytes=VMEM_LIMIT_BYTES)


def _row_tile(rows, want):
    t = min(rows, want)
    while rows % t:
        t //= 2
    return t


def _sigmoid(x):
    return 1.0 / (1.0 + jnp.exp(-x))


def _silu(x):
    return x * _sigmoid(x)


def _sigmoid_tanh(x):
    return 0.5 * jnp.tanh(0.5 * x) + 0.5


def _col_chunks(n, width=256):
    width = min(width, n)
    return [slice(c, c + width) for c in range(0, n, width)]


def _mod_kernel(c_ref, w_ref, b_ref, o_ref):
    a = _silu(c_ref[...]).astype(bf16)
    o_ref[...] = jnp.dot(a, w_ref[...].astype(bf16), preferred_element_type=f32) + b_ref[...]


def _modulation(c_rows, w_mod, b_mod):
    depth, d, n = w_mod.shape
    rows = c_rows.shape[0]
    tn = 512
    return pl.pallas_call(
        _mod_kernel,
        grid=(depth, n // tn),
        in_specs=[
            pl.BlockSpec((rows, d), lambda l, j: (0, 0)),
            pl.BlockSpec((None, d, tn), lambda l, j: (l, 0, j)),
            pl.BlockSpec((None, 1, tn), lambda l, j: (l, 0, j)),
        ],
        out_specs=pl.BlockSpec((None, rows, tn), lambda l, j: (l, 0, j)),
        out_shape=jax.ShapeDtypeStruct((depth, rows, n), f32),
        compiler_params=_params("arbitrary", "arbitrary"),
        name="modulation",
    )(c_rows, w_mod, b_mod.reshape(depth, 1, n))


def _norm_mod_kernel(x_ref, g_ref, shift_ref, scale_ref, o_ref):
    x = x_ref[...]
    y = x * lax.rsqrt(jnp.mean(x * x, axis=-1, keepdims=True) + EPS) * g_ref[...]
    o_ref[...] = (y * (1.0 + scale_ref[...]) + shift_ref[...]).astype(o_ref.dtype)


def _norm_kernel(x_ref, g_ref, o_ref):
    x = x_ref[...]
    y = x * lax.rsqrt(jnp.mean(x * x, axis=-1, keepdims=True) + EPS) * g_ref[...]
    o_ref[...] = y.astype(o_ref.dtype)


def _norm_modulate(x, g, mod, shift_idx, rows_per_group):
    rows, d = x.shape
    tm = _row_tile(rows_per_group, NORM_ROWS)
    per = rows_per_group // tm
    return pl.pallas_call(
        _norm_mod_kernel,
        grid=(rows // tm,),
        in_specs=[
            pl.BlockSpec((tm, d), lambda i: (i, 0)),
            pl.BlockSpec((1, d), lambda i: (0, 0)),
            pl.BlockSpec((None, 1, d), lambda i: (i // per, 0, shift_idx)),
            pl.BlockSpec((None, 1, d), lambda i: (i // per, 0, shift_idx + 1)),
        ],
        out_specs=pl.BlockSpec((tm, d), lambda i: (i, 0)),
        out_shape=jax.ShapeDtypeStruct((rows, d), bf16),
        compiler_params=_params("arbitrary"),
        name="norm_modulate",
    )(x, g.reshape(1, d), mod, mod)


def _final_norm(x, g):
    rows, d = x.shape
    tm = _row_tile(rows, NORM_ROWS)
    return pl.pallas_call(
        _norm_kernel,
        grid=(rows // tm,),
        in_specs=[pl.BlockSpec((tm, d), lambda i: (i, 0)), pl.BlockSpec((1, d), lambda i: (0, 0))],
        out_specs=pl.BlockSpec((tm, d), lambda i: (i, 0)),
        out_shape=jax.ShapeDtypeStruct((rows, d), f32),
        compiler_params=_params("arbitrary"),
        name="final_norm",
    )(x, g.reshape(1, d))


def _proj_kernel(h_ref, w_ref, o_ref):
    h = h_ref[...]
    for cs in _col_chunks(o_ref.shape[1]):
        o_ref[:, cs] = jnp.dot(h, w_ref[:, cs], preferred_element_type=f32).astype(o_ref.dtype)


def _proj_scale_kernel(h_ref, w_ref, s_ref, o_ref):
    h = h_ref[...]
    for cs in _col_chunks(o_ref.shape[1]):
        acc = jnp.dot(h, w_ref[:, cs], preferred_element_type=f32)
        o_ref[:, cs] = (acc * s_ref[:, cs]).astype(o_ref.dtype)


def _proj_sigmoid_kernel(h_ref, w_ref, o_ref):
    h = h_ref[...]
    for cs in _col_chunks(o_ref.shape[1]):
        acc = jnp.dot(h, w_ref[:, cs], preferred_element_type=f32)
        o_ref[:, cs] = _sigmoid_tanh(acc).astype(o_ref.dtype)


def _cast_kernel(w_ref, *o_refs, bounds):
    for o_ref, (lo, hi) in zip(o_refs, bounds):
        o_ref[...] = w_ref[:, lo:hi].astype(o_ref.dtype)


def _cast_weight(w, layer, splits=None):
    _, k, n = w.shape
    bounds = tuple(splits) if splits else ((0, n),)
    tk = 1 << ((CAST_BLOCK_BYTES // (4 * n)).bit_length() - 1)
    while k % tk:
        tk //= 2
    outs = pl.pallas_call(
        functools.partial(_cast_kernel, bounds=bounds),
        grid=(k // tk,),
        in_specs=[pl.BlockSpec((None, tk, n), lambda i: (layer, i, 0))],
        out_specs=[pl.BlockSpec((tk, hi - lo), lambda i: (i, 0)) for lo, hi in bounds],
        out_shape=[jax.ShapeDtypeStruct((k, hi - lo), bf16) for lo, hi in bounds],
        compiler_params=_params("arbitrary"),
        name="cast_weight",
    )(w)
    return outs if splits else outs[0]


class _Riders:
    def __init__(self, riders, steps, step_of):
        self.args, self.in_specs, self.out_specs, self.out_shapes, self.bounds = [], [], [], [], []
        self.plan = []
        for w, layer, splits in riders:
            _, k, n = w.shape
            rows = 16
            while rows < k and (k % rows or k // rows > steps):
                rows *= 2
            carried = k % rows == 0 and k // rows <= steps
            self.plan.append((carried, w, layer, splits))
            if not carried:
                continue
            last = k // rows - 1

            def blk(*g, last=last):
                return jnp.minimum(step_of(*g), last)

            bounds = tuple(splits) if splits else ((0, n),)
            self.args.append(w)
            self.in_specs.append(pl.BlockSpec((None, rows, n), lambda *g, b=blk, l=layer: (l, b(*g), 0)))
            for lo, hi in bounds:
                self.out_specs.append(pl.BlockSpec((rows, hi - lo), lambda *g, b=blk: (b(*g), 0)))
                self.out_shapes.append(jax.ShapeDtypeStruct((k, hi - lo), bf16))
            self.bounds.append(bounds)

    def wrap(self, body, n_in, n_out):
        n_src = len(self.bounds)
        n_dst = len(self.out_specs)
        bounds = self.bounds

        def kern(*refs):
            ins, srcs = refs[:n_in], refs[n_in:n_in + n_src]
            o0 = n_in + n_src
            outs, dsts = refs[o0:o0 + n_out], iter(refs[o0 + n_out:o0 + n_out + n_dst])
            body(*ins, *outs, *refs[o0 + n_out + n_dst:])
            for src, bnd in zip(srcs, bounds):
                for lo, hi in bnd:
                    dst = next(dsts)
                    dst[...] = src[:, lo:hi].astype(dst.dtype)

        return kern

    def split(self, results, n_out):
        host, rest = results[:n_out], list(results[n_out:])
        per = []
        for carried, w, layer, splits in self.plan:
            if carried:
                count = len(splits) if splits else 1
                got, rest = rest[:count], rest[count:]
                per.append(got[0] if not splits else tuple(got))
            else:
                got = _cast_weight(w, layer, splits)
                per.append(tuple(got) if splits else got)
        return host, per


def _project(h, w, out_dtype, tn, *, tm_want=1024, h_buffers=2, col_scale=None, sigmoid=False,
             riders=(), name="proj"):
    rows, d = h.shape
    n = w.shape[1]
    nj = n // tn
    tm = _row_tile(rows, tm_want)
    single = dict(pipeline_mode=pl.Buffered(1)) if h_buffers == 1 else {}
    h_spec = pl.BlockSpec((tm, d), lambda i, j: (i, 0), **single)
    w_spec = pl.BlockSpec((d, tn), lambda i, j: (0, j))
    args, specs = [h, w], [h_spec, w_spec]
    if sigmoid:
        kern = _proj_sigmoid_kernel
    elif col_scale is not None:
        kern = _proj_scale_kernel
        args.append(col_scale)
        specs.append(pl.BlockSpec((1, tn), lambda i, j: (0, j)))
    else:
        kern = _proj_kernel
    ride = _Riders(riders, (rows // tm) * nj, lambda i, j: i * nj + j)
    res = pl.pallas_call(
        ride.wrap(kern, len(args), 1),
        grid=(rows // tm, nj),
        in_specs=specs + ride.in_specs,
        out_specs=[pl.BlockSpec((tm, tn), lambda i, j: (i, j))] + ride.out_specs,
        out_shape=[jax.ShapeDtypeStruct((rows, n), out_dtype)] + ride.out_shapes,
        compiler_params=_params("arbitrary", "arbitrary"),
        name=name,
    )(*args, *ride.args)
    (out,), extra = ride.split(res, 1)
    return out, extra


def _rope_rotate(x, cos, sin):
    n = x.shape[-1]
    lane = lax.broadcasted_iota(jnp.int32, x.shape, x.ndim - 1)
    up = pltpu.roll(x, n - 16, x.ndim - 1)
    down = pltpu.roll(x, 16, x.ndim - 1)
    return x * cos + jnp.where((lane % 32) < 16, up, down) * sin


def _proj_window_kernel(h_ref, w_ref, cos_ref, sin_ref, o_ref):
    acc = jnp.dot(h_ref[...], w_ref[...], preferred_element_type=f32)
    cos = cos_ref[...]
    sin = sin_ref[...]
    for j in range(C_PAIRS):
        x = _rope_rotate(acc[:, j * LANES:(j + 1) * LANES], cos, sin) * (C_DH ** -0.5)
        o_ref[:, j * LANES:(j + 1) * LANES] = x.astype(o_ref.dtype)
    k = _rope_rotate(acc[:, C_WIDTH:C_WIDTH + LANES], cos, sin)
    v = acc[:, C_WIDTH + LANES:C_WIDTH + 2 * LANES]
    low = lax.broadcasted_iota(jnp.int32, k.shape, 1) < C_DH
    for idx, x in enumerate((k, v)):
        swapped = pltpu.roll(x, C_DH, 1)
        groups = (jnp.where(low, x, 0.0), jnp.where(low, 0.0, swapped),
                  jnp.where(low, swapped, 0.0), jnp.where(low, 0.0, x))
        base = C_WIDTH + idx * C_EXP
        for c, val in enumerate(groups):
            o_ref[:, base + c * LANES:base + (c + 1) * LANES] = val.astype(o_ref.dtype)


def _project_window(h, w, cos, sin):
    rows, d = h.shape
    n = w.shape[1]
    tm = _row_tile(rows, 1024)
    tok_tiles = cos.shape[0] // tm
    t_spec = pl.BlockSpec((tm, LANES), lambda i: (i % tok_tiles, 0))
    return pl.pallas_call(
        _proj_window_kernel,
        grid=(rows // tm,),
        in_specs=[pl.BlockSpec((tm, d), lambda i: (i, 0)),
                  pl.BlockSpec((d, n), lambda i: (0, 0), pipeline_mode=pl.Buffered(1)),
                  t_spec, t_spec],
        out_specs=pl.BlockSpec((tm, C_OUT_WIDTH), lambda i: (i, 0)),
        out_shape=jax.ShapeDtypeStruct((rows, C_OUT_WIDTH), bf16),
        compiler_params=_params("arbitrary"),
        name="proj_window",
    )(h, w, cos, sin)


def _hgrn_kernel(*refs, rev, has_s0, emit_state, readout, hp):
    it = iter(refs)
    q_ref, v_ref, f_ref, lb_ref = next(it), next(it), next(it), next(it)
    s0_ref = next(it) if has_s0 else None
    if readout:
        g_ref, oprev_ref, ng_ref = next(it), next(it), next(it)
    o_ref = next(it)
    sout_ref = next(it) if emit_state else None
    st_ref = next(it)

    blk = pl.program_id(2)
    nblk = pl.num_programs(2)
    tb = q_ref.shape[0]
    chunk = HGRN_CHUNK
    nchunk = tb // chunk
    width = hp * A_DK
    heads = range(hp)

    def head(a, h):
        return a[:, h * A_DK:(h + 1) * A_DK]

    @pl.when(blk == 0)
    def _():
        if has_s0:
            st_ref[...] = s0_ref[...]
        else:
            st_ref[...] = jnp.zeros_like(st_ref)

    lb = lb_ref[...]
    f = lb + (1.0 - lb) * _sigmoid(f_ref[...])
    logf = jnp.log(f)
    k = 1.0 - f

    row = lax.broadcasted_iota(jnp.int32, (tb, tb), 0)
    col = lax.broadcasted_iota(jnp.int32, (tb, tb), 1)
    same = (row // chunk) == (col // chunk)
    causal = (col >= row) if rev else (col <= row)
    mask = same & causal
    tri = jnp.where(mask, 1.0, 0.0).astype(bf16)

    hi = logf.astype(bf16)
    lo = (logf - hi.astype(f32)).astype(bf16)
    cum2 = jnp.dot(tri, jnp.concatenate([hi, lo], axis=1), preferred_element_type=f32)
    cum = cum2[:, :width] + cum2[:, width:]

    mid_rows, mid_decay = [], []
    for j in range(nchunk):
        mid = j * chunk + (chunk // 2 if rev else chunk // 2 - 1)
        at_mid = cum[mid:mid + 1, :]
        mid_rows.append(jnp.broadcast_to(at_mid, (chunk, width)))
        mid_decay.append(jnp.broadcast_to(jnp.exp(at_mid), (chunk, width)))
    rel = cum - jnp.concatenate(mid_rows, axis=0)
    q_rel = _silu(q_ref[...]) * jnp.exp(rel)
    q_att = q_rel.astype(bf16)
    k_inv = (k * jnp.exp(-rel)).astype(bf16)
    q_dec = (q_rel * jnp.concatenate(mid_decay, axis=0)).astype(bf16)
    v = v_ref[...].astype(bf16)
    att = [lax.dot_general(head(q_att, h), head(k_inv, h), _NT, preferred_element_type=f32)
           for h in heads]
    att = [jnp.where(mask, a, 0.0).astype(bf16) for a in att]
    o_intra = [jnp.dot(att[h], head(v, h), preferred_element_type=f32) for h in heads]

    order = range(nchunk - 1, -1, -1) if rev else range(nchunk)
    k_end, dec = {}, {}
    for j in order:
        sl = slice(j * chunk, (j + 1) * chunk)
        last = j * chunk if rev else (j + 1) * chunk - 1
        tot = cum[last:last + 1, :]
        k_end[j] = (k[sl] * jnp.exp(tot - cum[sl])).astype(bf16)
        dec[j] = jnp.exp(tot)
    zero = jnp.zeros((chunk, A_DK), bf16)
    upd = {}
    for h in heads:
        k_diag = jnp.concatenate(
            [jnp.concatenate([head(k_end[j], h) if c == j else zero for c in range(nchunk)], axis=1)
             for j in range(nchunk)], axis=0)
        u_all = lax.dot_general(head(v, h), k_diag, _TN, preferred_element_type=f32)
        for j in range(nchunk):
            upd[h, j] = u_all[:, j * A_DK:(j + 1) * A_DK]

    before = {}
    final = []
    for h in heads:
        s = st_ref[h]
        for j in order:
            before[h, j] = s.astype(bf16)
            s = s * head(dec[j], h) + upd[h, j]
        st_ref[h] = s
        final.append(s)

    outs = []
    for h in heads:
        o_inter = [lax.dot_general(head(q_dec[j * chunk:(j + 1) * chunk], h), before[h, j], _NT,
                                   preferred_element_type=f32) for j in range(nchunk)]
        outs.append(o_intra[h] + jnp.concatenate(o_inter, axis=0))

    if readout:
        ng = ng_ref[...]
        normed = []
        for h in heads:
            o = outs[h] + head(oprev_ref[...], h)
            normed.append(o * lax.rsqrt(jnp.mean(o * o, axis=-1, keepdims=True) + EPS) * ng)
        o_ref[...] = (jnp.concatenate(normed, axis=1) * _silu(g_ref[...])).astype(o_ref.dtype)
    else:
        o_ref[...] = jnp.concatenate(outs, axis=1)

    if emit_state:
        @pl.when(blk == nblk - 1)
        def _():
            for h in heads:
                sout_ref[h] = final[h]


def _hgrn_scan(proj, lb_dir, seq, direction, *, s0=None, emit_state=False, readout=None):
    rows = proj.shape[0]
    batch = rows // seq
    tb = min(HGRN_BLOCK, seq)
    nblk = seq // tb
    rev = direction == 1
    hp = HGRN_HEADS_PER_STEP
    hblocks = A_HEADS // hp
    width = hp * A_DK

    def tok(b, h, i):
        return b * nblk + (nblk - 1 - i if rev else i)

    def col_spec(group):
        return pl.BlockSpec((tb, width), lambda b, h, i: (tok(b, h, i), group * hblocks + h))

    state_spec = pl.BlockSpec((None, hp, A_DV, A_DK), lambda b, h, i: (b, h, 0, 0))
    args = [proj, proj, proj, lb_dir.reshape(1, A_QK)]
    specs = [col_spec(0), col_spec(1), col_spec(2 + direction),
             pl.BlockSpec((1, width), lambda b, h, i: (0, h))]
    if s0 is not None:
        args.append(s0)
        specs.append(state_spec)
    if readout is not None:
        o_prev, norm_g = readout
        args += [proj, o_prev, norm_g.reshape(1, A_DV)]
        specs += [col_spec(4),
                  pl.BlockSpec((tb, width), lambda b, h, i: (tok(b, h, i), h)),
                  pl.BlockSpec((1, A_DV), lambda b, h, i: (0, 0))]
    out_shape = [jax.ShapeDtypeStruct((rows, A_WIDTH), bf16 if readout is not None else f32)]
    out_specs = [pl.BlockSpec((tb, width), lambda b, h, i: (tok(b, h, i), h))]
    if emit_state:
        out_shape.append(jax.ShapeDtypeStruct((batch, A_HEADS, A_DV, A_DK), f32))
        out_specs.append(state_spec)
    kern = functools.partial(_hgrn_kernel, rev=rev, has_s0=s0 is not None,
                             emit_state=emit_state, readout=readout is not None, hp=hp)
    res = pl.pallas_call(
        kern,
        grid=(batch, hblocks, nblk),
        in_specs=specs,
        out_specs=out_specs,
        out_shape=out_shape,
        scratch_shapes=[pltpu.VMEM((hp, A_DV, A_DK), f32)],
        compiler_params=_params("arbitrary", "arbitrary", "arbitrary"),
        name="hgrn_scan",
    )(*args)
    return res if emit_state else res[0]


def _hgrn_mixer(proj_lat, proj_ctx, lb, norm_g, seq, ctx_len, need_ctx):
    o_c_f, s_f = _hgrn_scan(proj_ctx, lb[0], ctx_len, 0, emit_state=True)
    o_l_f = _hgrn_scan(proj_lat, lb[0], seq, 0, s0=s_f)
    if need_ctx:
        a_ctx, s_b = _hgrn_scan(proj_ctx, lb[1], ctx_len, 1, emit_state=True, readout=(o_c_f, norm_g))
    else:
        _, s_b = _hgrn_scan(proj_ctx, lb[1], ctx_len, 1, emit_state=True)
        a_ctx = None
    a_lat = _hgrn_scan(proj_lat, lb[1], seq, 1, s0=s_b, readout=(o_l_f, norm_g))
    return a_lat, a_ctx


def _attend(parts, sink=None):
    m = parts[0][0].max(axis=-1, keepdims=True)
    for s, _ in parts[1:]:
        m = jnp.maximum(m, s.max(axis=-1, keepdims=True))
    if sink is not None:
        m = jnp.maximum(m, sink)
    den = None
    acc = None
    for s, v in parts:
        e = jnp.exp(s - m)
        d = e.sum(axis=-1, keepdims=True)
        o = jnp.dot(e.astype(bf16), v, preferred_element_type=f32)
        den = d if den is None else den + d
        acc = o if acc is None else acc + o
    if sink is not None:
        den = den + jnp.exp(sink - m)
    return acc / den


def _na_kernel(types_ref, q_ref, k_ref, v_ref, kc_ref, vc_ref, tbl_ref, o_ref, *, grid_rows, hp):
    del types_ref
    step = pl.program_id(2)
    span = NA_UNION * GRID_W
    first = jnp.clip(step * NA_QROWS - NA_ROWS // 2, 0, grid_rows - NA_UNION)
    start = pl.multiple_of(first * GRID_W, GRID_W)
    for h in range(hp):
        hs = slice(h * B_DH, (h + 1) * B_DH)
        q = q_ref[:, hs]
        kn = k_ref[pl.ds(start, span), hs]
        vn = v_ref[pl.ds(start, span), hs]
        s_nb = lax.dot_general(q, kn, _NT, preferred_element_type=f32) + tbl_ref[h]
        s_cx = lax.dot_general(q, kc_ref[:, hs], _NT, preferred_element_type=f32)
        o = _attend([(s_nb, vn), (s_cx, vc_ref[:, hs])])
        o_ref[:, hs] = o.astype(o_ref.dtype)


def _na_tables(rpb, grid_rows):
    assert grid_rows >= NA_UNION and grid_rows % NA_QROWS == 0
    col = np.arange(GRID_W)
    col_off = np.clip(col[None, :] - col[:, None] + NA_COLS - 1, 0, 2 * NA_COLS - 2)
    col_start = np.clip(col - NA_COLS // 2, 0, GRID_W - NA_COLS)
    col_ok = (col[None, :] >= col_start[:, None]) & (col[None, :] < col_start[:, None] + NA_COLS)
    seen, types = {}, []
    for i in range(grid_rows // NA_QROWS):
        first = int(np.clip(i * NA_QROWS - NA_ROWS // 2, 0, grid_rows - NA_UNION))
        key_row = first + np.arange(NA_UNION)[None, :]
        r = i * NA_QROWS + np.arange(NA_QROWS)[:, None]
        row_start = np.clip(r - NA_ROWS // 2, 0, grid_rows - NA_ROWS)
        ok = (key_row >= row_start) & (key_row < row_start + NA_ROWS)
        assert (ok.sum(axis=1) == NA_ROWS).all()
        off = np.where(ok, key_row - r + NA_ROWS - 1, 0)
        sig = (ok.tobytes(), off.tobytes())
        if sig not in seen:
            seen[sig] = (len(seen), ok, off)
        types.append(seen[sig][0])
    onehot = jnp.asarray((col_off[None] == np.arange(2 * NA_COLS - 1)[:, None, None]).astype(np.float32))
    toe = jnp.einsum("hoc,cqk->hoqk", rpb.astype(f32), onehot, precision=lax.Precision.HIGHEST)
    toe = jnp.where(col_ok[None, None], toe, NEG_INF)
    neg = jnp.full((rpb.shape[0], GRID_W, GRID_W), NEG_INF, f32)
    tables = []
    for _, ok, off in sorted(seen.values(), key=lambda t: t[0]):
        rows = [jnp.concatenate([toe[:, off[rl, a]] if ok[rl, a] else neg for a in range(NA_UNION)],
                                axis=-1) for rl in range(NA_QROWS)]
        tables.append(jnp.concatenate(rows, axis=-2))
    return jnp.stack(tables, axis=1), jnp.asarray(np.array(types, np.int32))


def _neighborhood_attention(qkv_lat, qkv_ctx, tables, types, seq, ctx_len):
    rows = qkv_lat.shape[0]
    batch = rows // seq
    tq = NA_QROWS * GRID_W
    nq = seq // tq
    hp = NA_HEADS_PER_STEP
    hb = B_HEADS // hp
    width = hp * B_DH
    kern = functools.partial(_na_kernel, grid_rows=seq // GRID_W, hp=hp)
    grid_spec = pltpu.PrefetchScalarGridSpec(
        num_scalar_prefetch=1,
        grid=(batch, hb, nq),
        in_specs=[
            pl.BlockSpec((tq, width), lambda b, h, i, t: (b * nq + i, h)),
            pl.BlockSpec((seq, width), lambda b, h, i, t: (b, hb + h), pipeline_mode=pl.Buffered(1)),
            pl.BlockSpec((seq, width), lambda b, h, i, t: (b, 2 * hb + h), pipeline_mode=pl.Buffered(1)),
            pl.BlockSpec((ctx_len, width), lambda b, h, i, t: (b, hb + h)),
            pl.BlockSpec((ctx_len, width), lambda b, h, i, t: (b, 2 * hb + h)),
            pl.BlockSpec((hp, None, tq, NA_UNION * GRID_W), lambda b, h, i, t: (h, t[i], 0, 0)),
        ],
        out_specs=pl.BlockSpec((tq, width), lambda b, h, i, t: (b * nq + i, h)),
    )
    return pl.pallas_call(
        kern,
        grid_spec=grid_spec,
        out_shape=jax.ShapeDtypeStruct((rows, B_WIDTH), bf16),
        compiler_params=_params("arbitrary", "arbitrary", "arbitrary"),
        name="neighborhood_attention",
    )(types, qkv_lat, qkv_lat, qkv_lat, qkv_ctx, qkv_ctx, tables)


def _window_kernel(sink_ref, q_ref, k_ref, v_ref, kc_ref, vc_ref, o_ref, *, seq):
    n = pl.program_id(1)
    span = 3 * C_BLOCK
    start = pl.multiple_of(jnp.clip((n - 1) * C_BLOCK, 0, seq - span), C_BLOCK)
    pairs = C_PAIRS // C_KV_HEADS
    rows = pairs * C_BLOCK
    qpos = n * C_BLOCK + lax.broadcasted_iota(jnp.int32, (C_BLOCK, span), 0)
    kpos = start + lax.broadcasted_iota(jnp.int32, (C_BLOCK, span), 1)
    off_window = jnp.where(jnp.abs(qpos - kpos) <= C_WINDOW, 0.0, NEG_INF)
    off_window = jnp.concatenate([off_window] * pairs, axis=0)
    pair_id = lax.broadcasted_iota(jnp.int32, (rows, 1), 0) // C_BLOCK

    chains = [(g, e) for g in range(C_KV_HEADS) for e in range(2)]
    q = {g: jnp.concatenate([q_ref[:, (g * pairs + p) * LANES:(g * pairs + p + 1) * LANES]
                             for p in range(pairs)], axis=0) for g in range(C_KV_HEADS)}
    s_w, s_c, sink = {}, {}, {}
    for g, e in chains:
        cs = slice((2 * g + e) * LANES, (2 * g + e + 1) * LANES)
        s = lax.dot_general(q[g], k_ref[pl.ds(start, span), cs], _NT, preferred_element_type=f32)
        s_w[g, e] = s + off_window
        s_c[g, e] = lax.dot_general(q[g], kc_ref[:, cs], _NT, preferred_element_type=f32)
        col = jnp.zeros((rows, 1), f32)
        for p in range(pairs):
            col = jnp.where(pair_id == p, sink_ref[(g * pairs + p) * 2 + e], col)
        sink[g, e] = col
    out = {}
    for g, e in chains:
        cs = slice((2 * g + e) * LANES, (2 * g + e + 1) * LANES)
        out[g, e] = _attend([(s_w[g, e], v_ref[pl.ds(start, span), cs]), (s_c[g, e], vc_ref[:, cs])],
                            sink=sink[g, e])
    for g in range(C_KV_HEADS):
        o = out[g, 0] + out[g, 1]
        for p in range(pairs):
            o_ref[:, (g * pairs + p) * LANES:(g * pairs + p + 1) * LANES] = (
                o[p * C_BLOCK:(p + 1) * C_BLOCK].astype(o_ref.dtype))


def _window_attention(qkv_lat, qkv_ctx, sink, seq, ctx_len):
    rows = qkv_lat.shape[0]
    batch = rows // seq
    nq = seq // C_BLOCK
    kblk = C_WIDTH // C_EXP
    kern = functools.partial(_window_kernel, seq=seq)
    resident = dict(pipeline_mode=pl.Buffered(1))
    return pl.pallas_call(
        kern,
        grid=(batch, nq),
        in_specs=[
            pl.BlockSpec(memory_space=pltpu.SMEM),
            pl.BlockSpec((C_BLOCK, C_WIDTH), lambda b, i: (b * nq + i, 0)),
            pl.BlockSpec((seq, C_EXP), lambda b, i: (b, kblk), **resident),
            pl.BlockSpec((seq, C_EXP), lambda b, i: (b, kblk + 1), **resident),
            pl.BlockSpec((ctx_len, C_EXP), lambda b, i: (b, kblk)),
            pl.BlockSpec((ctx_len, C_EXP), lambda b, i: (b, kblk + 1)),
        ],
        out_specs=pl.BlockSpec((C_BLOCK, C_WIDTH), lambda b, i: (b * nq + i, 0)),
        out_shape=jax.ShapeDtypeStruct((rows, C_WIDTH), bf16),
        compiler_params=_params("arbitrary", "arbitrary"),
        name="window_attention",
    )(sink.astype(f32), qkv_lat, qkv_lat, qkv_lat, qkv_ctx, qkv_ctx)


def _ctx_attn_kernel(*refs, heads, dh, k_cols, v_cols, has_sink):
    if has_sink:
        sink_ref, qkv_ref, o_ref = refs
    else:
        qkv_ref, o_ref = refs
    group = heads // len(k_cols)
    for kh, (kc, vc) in enumerate(zip(k_cols, v_cols)):
        k = qkv_ref[:, kc:kc + dh]
        v = qkv_ref[:, vc:vc + dh]
        for g in range(group):
            hq = kh * group + g
            q = qkv_ref[:, hq * dh:(hq + 1) * dh]
            s = lax.dot_general(q, k, _NT, preferred_element_type=f32)
            o = _attend([(s, v)], sink=sink_ref[hq] if has_sink else None)
            o_ref[:, hq * dh:(hq + 1) * dh] = o.astype(o_ref.dtype)


def _context_attention(qkv_ctx, ctx_len, heads, dh, k_cols, v_cols, sink=None):
    rows, width = qkv_ctx.shape
    kern = functools.partial(_ctx_attn_kernel, heads=heads, dh=dh, k_cols=k_cols, v_cols=v_cols,
                             has_sink=sink is not None)
    args, specs = [qkv_ctx], [pl.BlockSpec((ctx_len, width), lambda b: (b, 0))]
    if sink is not None:
        args.insert(0, sink.astype(f32))
        specs.insert(0, pl.BlockSpec(memory_space=pltpu.SMEM))
    return pl.pallas_call(
        kern,
        grid=(rows // ctx_len,),
        in_specs=specs,
        out_specs=pl.BlockSpec((ctx_len, heads * dh), lambda b: (b, 0)),
        out_shape=jax.ShapeDtypeStruct((rows, heads * dh), bf16),
        compiler_params=_params("arbitrary"),
        name="context_attention",
    )(*args)


def _merge_kernel(oa_ref, ob_ref, oc_ref, wa_ref, wb_ref, wc_ref, ga_ref, gb_ref, gc_ref, o_ref):
    oa, ob, oc = oa_ref[...], ob_ref[...], oc_ref[...]
    for cs in _col_chunks(o_ref.shape[1]):
        br_a = jnp.dot(oa, wa_ref[:, cs], preferred_element_type=f32)
        br_b = jnp.dot(ob, wb_ref[:, cs], preferred_element_type=f32)
        br_c = jnp.dot(oc, wc_ref[:, cs], preferred_element_type=f32)
        m = ga_ref[:, cs] * br_a + gb_ref[:, cs] * br_b + gc_ref[:, cs] * br_c
        o_ref[:, cs] = m.astype(o_ref.dtype)


def _merge(o_a, o_b, o_c, gates, w_branch, riders=()):
    rows = o_a.shape[0]
    d = w_branch.shape[1]
    tm = _row_tile(rows, 1024)
    tn = 512
    nj = d // tn
    assert A_WIDTH % B_WIDTH == 0 and B_WIDTH == C_WIDTH
    b_blk = A_WIDTH // B_WIDTH
    ride = _Riders(riders, (rows // tm) * nj, lambda i, j: i * nj + j)
    res = pl.pallas_call(
        ride.wrap(_merge_kernel, 9, 1),
        grid=(rows // tm, nj),
        in_specs=[
            pl.BlockSpec((tm, A_WIDTH), lambda i, j: (i, 0)),
            pl.BlockSpec((tm, B_WIDTH), lambda i, j: (i, 0)),
            pl.BlockSpec((tm, C_WIDTH), lambda i, j: (i, 0)),
            pl.BlockSpec((A_WIDTH, tn), lambda i, j: (0, j)),
            pl.BlockSpec((B_WIDTH, tn), lambda i, j: (b_blk, j)),
            pl.BlockSpec((C_WIDTH, tn), lambda i, j: (b_blk + 1, j)),
            pl.BlockSpec((tm, tn), lambda i, j: (i, j)),
            pl.BlockSpec((tm, tn), lambda i, j: (i, nj + j)),
            pl.BlockSpec((tm, tn), lambda i, j: (i, 2 * nj + j)),
        ] + ride.in_specs,
        out_specs=[pl.BlockSpec((tm, tn), lambda i, j: (i, j))] + ride.out_specs,
        out_shape=[jax.ShapeDtypeStruct((rows, d), bf16)] + ride.out_shapes,
        compiler_params=_params("arbitrary", "arbitrary"),
        name="merge_branches",
    )(o_a, o_b, o_c, w_branch, w_branch, w_branch, gates, gates, gates, *ride.args)
    (out,), extra = ride.split(res, 1)
    return out, extra


def _row_halves(m):
    return [slice(0, m // 2), slice(m // 2, m)] if m % 32 == 0 else [slice(0, m)]


def _residual_kernel(a_ref, w_ref, x_ref, gate_ref, o_ref):
    if o_ref.shape[1] > 256:
        a = a_ref[...]
        for cs in _col_chunks(o_ref.shape[1]):
            y = jnp.dot(a, w_ref[:, cs], preferred_element_type=f32)
            o_ref[:, cs] = x_ref[:, cs] + gate_ref[:, cs] * y
    else:
        w = w_ref[...]
        for rs in _row_halves(o_ref.shape[0]):
            y = jnp.dot(a_ref[rs, :], w, preferred_element_type=f32)
            o_ref[rs, :] = x_ref[rs, :] + gate_ref[...] * y


def _gated_residual_matmul(a, w, x, mod, gate_idx, rows_per_group, tn, name, a_buffers=1):
    rows, kdim = a.shape
    d = w.shape[1]
    tm = _row_tile(rows_per_group, 1024)
    per = rows_per_group // tm
    nj = d // tn
    single = dict(pipeline_mode=pl.Buffered(1)) if a_buffers == 1 else {}
    return pl.pallas_call(
        _residual_kernel,
        grid=(rows // tm, nj),
        in_specs=[
            pl.BlockSpec((tm, kdim), lambda i, j: (i, 0), **single),
            pl.BlockSpec((kdim, tn), lambda i, j: (0, j)),
            pl.BlockSpec((tm, tn), lambda i, j: (i, j)),
            pl.BlockSpec((None, 1, tn), lambda i, j: (i // per, 0, gate_idx * nj + j)),
        ],
        out_specs=pl.BlockSpec((tm, tn), lambda i, j: (i, j)),
        out_shape=jax.ShapeDtypeStruct((rows, d), f32),
        compiler_params=_params("arbitrary", "arbitrary"),
        name=name,
    )(a, w, x, mod)


def _residual_ksplit_kernel(a0_ref, a1_ref, w0_ref, w1_ref, x_ref, gate_ref, o_ref):
    w0, w1 = w0_ref[...], w1_ref[...]
    for rs in _row_halves(o_ref.shape[0]):
        y = (jnp.dot(a0_ref[rs, :], w0, preferred_element_type=f32)
             + jnp.dot(a1_ref[rs, :], w1, preferred_element_type=f32))
        o_ref[rs, :] = x_ref[rs, :] + gate_ref[...] * y


def _gated_residual_matmul_ksplit(a, w, x, mod, gate_idx, rows_per_group, tn, name):
    rows, kdim = a.shape
    d = w.shape[1]
    kh = kdim // 2
    assert kdim % 2 == 0 and kh % LANES == 0
    tm = _row_tile(rows_per_group, 1024)
    per = rows_per_group // tm
    nj = d // tn
    return pl.pallas_call(
        _residual_ksplit_kernel,
        grid=(rows // tm, nj),
        in_specs=[
            pl.BlockSpec((tm, kh), lambda i, j: (i, 0), pipeline_mode=pl.Buffered(1)),
            pl.BlockSpec((tm, kh), lambda i, j: (i, 1)),
            pl.BlockSpec((kh, tn), lambda i, j: (0, j)),
            pl.BlockSpec((kh, tn), lambda i, j: (1, j)),
            pl.BlockSpec((tm, tn), lambda i, j: (i, j)),
            pl.BlockSpec((None, 1, tn), lambda i, j: (i // per, 0, gate_idx * nj + j)),
        ],
        out_specs=pl.BlockSpec((tm, tn), lambda i, j: (i, j)),
        out_shape=jax.ShapeDtypeStruct((rows, d), f32),
        compiler_params=_params("arbitrary", "arbitrary"),
        name=name,
    )(a, a, w, w, x, mod)


def _swiglu_kernel(h_ref, wg_ref, wu_ref, o_ref):
    wg, wu = wg_ref[...], wu_ref[...]
    for rs in _row_halves(o_ref.shape[0]):
        h = h_ref[rs, :]
        g = jnp.dot(h, wg, preferred_element_type=f32)
        u = jnp.dot(h, wu, preferred_element_type=f32)
        o_ref[rs, :] = (g * _sigmoid_tanh(g) * u).astype(o_ref.dtype)


def _swiglu_up(h, w_gate, w_up, riders=()):
    rows, d = h.shape
    n = w_gate.shape[1]
    tm = _row_tile(rows, 1024)
    tn = FFN_TN
    nj = n // tn
    w_spec = pl.BlockSpec((d, tn), lambda i, j: (0, j))
    ride = _Riders(riders, (rows // tm) * nj, lambda i, j: i * nj + j)
    res = pl.pallas_call(
        ride.wrap(_swiglu_kernel, 3, 1),
        grid=(rows // tm, nj),
        in_specs=[pl.BlockSpec((tm, d), lambda i, j: (i, 0)), w_spec, w_spec] + ride.in_specs,
        out_specs=[pl.BlockSpec((tm, tn), lambda i, j: (i, j))] + ride.out_specs,
        out_shape=[jax.ShapeDtypeStruct((rows, n), bf16)] + ride.out_shapes,
        compiler_params=_params("arbitrary", "arbitrary"),
        name="swiglu_up",
    )(h, w_gate, w_up, *ride.args)
    (out,), extra = ride.split(res, 1)
    return out, extra


def _rope_tables(seq):
    half = C_DH // 2
    pos = jnp.arange(seq)
    inv = ROPE_BASE ** (-jnp.arange(0, half, 2, dtype=f32) / half)
    ang_row = (pos // GRID_W).astype(f32)[:, None] * inv[None, :]
    ang_col = (pos % GRID_W).astype(f32)[:, None] * inv[None, :]

    def one(ang):
        return (jnp.concatenate([jnp.cos(ang), jnp.cos(ang)], axis=-1),
                jnp.concatenate([-jnp.sin(ang), jnp.sin(ang)], axis=-1))

    cr, sr = one(ang_row)
    cc, sc = one(ang_col)
    cos = jnp.concatenate([cr, cc], axis=-1)
    sin = jnp.concatenate([sr, sc], axis=-1)
    reps = LANES // C_DH
    return jnp.tile(cos, (1, reps)), jnp.tile(sin, (1, reps))


def kernel(x, c, ctx, c_ctx, norm1_g, norm2_g, w_mod, b_mod, w_in, hgrn_lb, a_norm_g, na_rpb,
           c_sink, w_branch, w_out, w_ffn_gate, w_ffn_up, w_ffn_down, final_norm_g):
    batch, seq, d = x.shape
    ctx_len = ctx.shape[1]
    depth = w_in.shape[0]
    n_lat, n_ctx = batch * seq, batch * ctx_len

    lb_w = jax.nn.softmax(hgrn_lb.astype(f32), axis=0)
    lower_bounds = jnp.cumsum(lb_w, axis=0) - lb_w[:1]

    mod_rows = 8 * (-(-(batch + 1) // 8))
    c_rows = jnp.zeros((mod_rows, d), f32).at[:batch].set(c).at[batch].set(c_ctx)
    mod_all = _modulation(c_rows, w_mod, b_mod)

    cos_t, sin_t = _rope_tables(seq)
    ones_t = jnp.ones((n_ctx, LANES), f32)
    zeros_t = jnp.zeros((n_ctx, LANES), f32)

    a_hi = 3 * A_QK + 2 * A_WIDTH
    b_hi = a_hi + 3 * B_WIDTH
    c_hi = b_hi + C_WIDTH + 2 * C_KV_WIDTH
    in_width = w_in.shape[2]
    b_scale = jnp.concatenate([jnp.full((B_WIDTH,), B_DH ** -0.5, f32),
                               jnp.ones((2 * B_WIDTH,), f32)]).reshape(1, -1)
    ck_cols = tuple(C_WIDTH + 2 * g * LANES for g in range(C_KV_HEADS))
    cv_cols = tuple(C_WIDTH + C_EXP + 2 * g * LANES for g in range(C_KV_HEADS))
    bk_cols = tuple(B_WIDTH + h * B_DH for h in range(B_HEADS))
    bv_cols = tuple(2 * B_WIDTH + h * B_DH for h in range(B_HEADS))

    x_lat = x.reshape(n_lat, d)
    x_ctx = ctx.reshape(n_ctx, d)

    in_splits = ((0, a_hi), (a_hi, b_hi), (b_hi, c_hi), (c_hi, in_width))
    w_in_cast = _cast_weight(w_in, 0, in_splits)

    for l in range(depth):
        need_ctx = l < depth - 1
        mod_l = mod_all[l, :batch].reshape(batch, 1, N_MOD * d)
        mod_c = mod_all[l, batch:batch + 1].reshape(1, 1, N_MOD * d)
        w_a, w_b, w_c, w_g = w_in_cast
        na_tables, na_types = _na_tables(na_rpb[l], seq // GRID_W)

        h_lat = _norm_modulate(x_lat, norm1_g[l], mod_l, 0, seq)
        h_ctx = _norm_modulate(x_ctx, norm1_g[l], mod_c, 0, n_ctx)

        pa_lat, (wfu,) = _project(h_lat, w_a, f32, 1024, riders=[(w_ffn_up, l, None)], name="proj_hgrn")
        pb_lat, (wbr, wo) = _project(h_lat, w_b, bf16, 1024, col_scale=b_scale,
                                     riders=[(w_branch, l, None), (w_out, l, None)], name="proj_na")
        pc_lat = _project_window(h_lat, w_c, cos_t, sin_t)
        g_lat, (wfg,) = _project(h_lat, w_g, bf16, 1024, sigmoid=True, riders=[(w_ffn_gate, l, None)],
                                 name="proj_gates")
        pa_ctx, _ = _project(h_ctx, w_a, f32, 1024, name="proj_hgrn")
        pb_ctx, _ = _project(h_ctx, w_b, bf16, 512, col_scale=b_scale, name="proj_na")
        pc_ctx = _project_window(h_ctx, w_c, ones_t, zeros_t)

        a_lat, a_ctx = _hgrn_mixer(pa_lat, pa_ctx, lower_bounds[l], a_norm_g[l], seq, ctx_len, need_ctx)
        b_lat = _neighborhood_attention(pb_lat, pb_ctx, na_tables, na_types, seq, ctx_len)
        c_lat = _window_attention(pc_lat, pc_ctx, c_sink[l], seq, ctx_len)

        m_lat, (wfd,) = _merge(a_lat, b_lat, c_lat, g_lat, wbr, riders=[(w_ffn_down, l, None)])
        x_lat = _gated_residual_matmul(m_lat, wo, x_lat, mod_l, 2, seq, 512, "out_proj", a_buffers=2)
        h2 = _norm_modulate(x_lat, norm2_g[l], mod_l, 3, seq)
        next_in = [(w_in, l + 1, in_splits)] if l + 1 < depth else []
        u, nxt = _swiglu_up(h2, wfg, wfu, riders=next_in)
        if nxt:
            w_in_cast = nxt[0]
        x_lat = _gated_residual_matmul_ksplit(u, wfd, x_lat, mod_l, 5, seq, FFN_TN, "ffn_down")

        if need_ctx:
            g_ctx, _ = _project(h_ctx, w_g, bf16, 1024, sigmoid=True, name="proj_gates")
            b_ctx = _context_attention(pb_ctx, ctx_len, B_HEADS, B_DH, bk_cols, bv_cols)
            c_ctx_o = _context_attention(pc_ctx, ctx_len, C_HEADS, C_DH, ck_cols, cv_cols, sink=c_sink[l])
            m_ctx, _ = _merge(a_ctx, b_ctx, c_ctx_o, g_ctx, wbr)
            x_ctx = _gated_residual_matmul(m_ctx, wo, x_ctx, mod_c, 2, n_ctx, 1024, "out_proj")
            h2c = _norm_modulate(x_ctx, norm2_g[l], mod_c, 3, n_ctx)
            uc, _ = _swiglu_up(h2c, wfg, wfu)
            x_ctx = _gated_residual_matmul(uc, wfd, x_ctx, mod_c, 5, n_ctx, FFN_TN, "ffn_down")

    return _final_norm(x_lat, final_norm_g).reshape(batch, seq, d)
```

```python
import functools

import jax
import jax.numpy as jnp
import numpy as np
from jax import lax
from jax.experimental import pallas as pl
from jax.experimental.pallas import tpu as pltpu

GRID_W = 64
EPS = 1e-6
NEG_INF = -1e30
N_MOD = 6
A_HEADS, A_DK, A_DV, A_CHUNK = 16, 128, 128, 32
A_QK = A_HEADS * A_DK
A_WIDTH = A_HEADS * A_DV
B_HEADS, B_DH = 8, 128
B_WIDTH = B_HEADS * B_DH
NA_ROWS, NA_COLS = 8, 16
C_HEADS, C_KV_HEADS, C_DH = 16, 2, 64
C_WIDTH = C_HEADS * C_DH
C_KV_WIDTH = C_KV_HEADS * C_DH
C_WINDOW = 128
C_BLOCK = 128
ROPE_BASE = 10000.0

LANES = 128
VMEM_LIMIT_BYTES = 56 * 1024 * 1024

HGRN_BLOCK = 256
HGRN_CHUNK = 2 * A_CHUNK
HGRN_HEADS_PER_STEP = 16
NA_QROWS = 4
NA_UNION = NA_ROWS + NA_QROWS
NA_HEADS_PER_STEP = 8
NORM_ROWS = 512
FFN_TN = 256
CAST_BLOCK_BYTES = 4 * 1024 * 1024

C_PAIRS = C_WIDTH // LANES
C_EXP = 2 * C_KV_HEADS * LANES
C_OUT_WIDTH = C_WIDTH + 2 * C_EXP

_NT = (((1,), (1,)), ((), ()))
_TN = (((0,), (0,)), ((), ()))

bf16 = jnp.bfloat16
f32 = jnp.float32


def _params(*sem):
    return pltpu.CompilerParams(dimension_semantics=sem, vmem_limit_bytes=VMEM_LIMIT_BYTES)


def _row_tile(rows, want):
    t = min(rows, want)
    while rows % t:
        t //= 2
    return t


def _sigmoid(x):
    return 1.0 / (1.0 + jnp.exp(-x))


def _silu(x):
    return x * _sigmoid(x)


def _sigmoid_tanh(x):
    return 0.5 * jnp.tanh(0.5 * x) + 0.5


def _col_chunks(n, width=256):
    width = min(width, n)
    return [slice(c, c + width) for c in range(0, n, width)]


def _mod_kernel(c_ref, w_ref, b_ref, o_ref):
    a = _silu(c_ref[...]).astype(bf16)
    o_ref[...] = jnp.dot(a, w_ref[...].astype(bf16), preferred_element_type=f32) + b_ref[...]


def _modulation(c_rows, w_mod, b_mod):
    depth, d, n = w_mod.shape
    rows = c_rows.shape[0]
    tn = 512
    return pl.pallas_call(
        _mod_kernel,
        grid=(depth, n // tn),
        in_specs=[
            pl.BlockSpec((rows, d), lambda l, j: (0, 0)),
            pl.BlockSpec((None, d, tn), lambda l, j: (l, 0, j)),
            pl.BlockSpec((None, 1, tn), lambda l, j: (l, 0, j)),
        ],
        out_specs=pl.BlockSpec((None, rows, tn), lambda l, j: (l, 0, j)),
        out_shape=jax.ShapeDtypeStruct((depth, rows, n), f32),
        compiler_params=_params("arbitrary", "arbitrary"),
        name="modulation",
    )(c_rows, w_mod, b_mod.reshape(depth, 1, n))


def _norm_mod_kernel(x_ref, g_ref, shift_ref, scale_ref, o_ref):
    x = x_ref[...]
    y = x * lax.rsqrt(jnp.mean(x * x, axis=-1, keepdims=True) + EPS) * g_ref[...]
    o_ref[...] = (y * (1.0 + scale_ref[...]) + shift_ref[...]).astype(o_ref.dtype)


def _norm_kernel(x_ref, g_ref, o_ref):
    x = x_ref[...]
    y = x * lax.rsqrt(jnp.mean(x * x, axis=-1, keepdims=True) + EPS) * g_ref[...]
    o_ref[...] = y.astype(o_ref.dtype)


def _norm_modulate(x, g, mod, shift_idx, rows_per_group):
    rows, d = x.shape
    tm = _row_tile(rows_per_group, NORM_ROWS)
    per = rows_per_group // tm
    return pl.pallas_call(
        _norm_mod_kernel,
        grid=(rows // tm,),
        in_specs=[
            pl.BlockSpec((tm, d), lambda i: (i, 0)),
            pl.BlockSpec((1, d), lambda i: (0, 0)),
            pl.BlockSpec((None, 1, d), lambda i: (i // per, 0, shift_idx)),
            pl.BlockSpec((None, 1, d), lambda i: (i // per, 0, shift_idx + 1)),
        ],
        out_specs=pl.BlockSpec((tm, d), lambda i: (i, 0)),
        out_shape=jax.ShapeDtypeStruct((rows, d), bf16),
        compiler_params=_params("arbitrary"),
        name="norm_modulate",
    )(x, g.reshape(1, d), mod, mod)


def _final_norm(x, g):
    rows, d = x.shape
    tm = _row_tile(rows, NORM_ROWS)
    return pl.pallas_call(
        _norm_kernel,
        grid=(rows // tm,),
        in_specs=[pl.BlockSpec((tm, d), lambda i: (i, 0)), pl.BlockSpec((1, d), lambda i: (0, 0))],
        out_specs=pl.BlockSpec((tm, d), lambda i: (i, 0)),
        out_shape=jax.ShapeDtypeStruct((rows, d), f32),
        compiler_params=_params("arbitrary"),
        name="final_norm",
    )(x, g.reshape(1, d))


def _proj_kernel(h_ref, w_ref, o_ref):
    h = h_ref[...]
    for cs in _col_chunks(o_ref.shape[1]):
        o_ref[:, cs] = jnp.dot(h, w_ref[:, cs], preferred_element_type=f32).astype(o_ref.dtype)


def _proj_scale_kernel(h_ref, w_ref, s_ref, o_ref):
    h = h_ref[...]
    for cs in _col_chunks(o_ref.shape[1]):
        acc = jnp.dot(h, w_ref[:, cs], preferred_element_type=f32)
        o_ref[:, cs] = (acc * s_ref[:, cs]).astype(o_ref.dtype)


def _proj_sigmoid_kernel(h_ref, w_ref, o_ref):
    h = h_ref[...]
    for cs in _col_chunks(o_ref.shape[1]):
        acc = jnp.dot(h, w_ref[:, cs], preferred_element_type=f32)
        o_ref[:, cs] = _sigmoid_tanh(acc).astype(o_ref.dtype)


def _cast_kernel(w_ref, *o_refs, bounds):
    for o_ref, (lo, hi) in zip(o_refs, bounds):
        o_ref[...] = w_ref[:, lo:hi].astype(o_ref.dtype)


def _cast_weight(w, layer, splits=None):
    _, k, n = w.shape
    bounds = tuple(splits) if splits else ((0, n),)
    tk = 1 << ((CAST_BLOCK_BYTES // (4 * n)).bit_length() - 1)
    while k % tk:
        tk //= 2
    outs = pl.pallas_call(
        functools.partial(_cast_kernel, bounds=bounds),
        grid=(k // tk,),
        in_specs=[pl.BlockSpec((None, tk, n), lambda i: (layer, i, 0))],
        out_specs=[pl.BlockSpec((tk, hi - lo), lambda i: (i, 0)) for lo, hi in bounds],
        out_shape=[jax.ShapeDtypeStruct((k, hi - lo), bf16) for lo, hi in bounds],
        compiler_params=_params("arbitrary"),
        name="cast_weight",
    )(w)
    return outs if splits else outs[0]


class _Riders:
    def __init__(self, riders, steps, step_of):
        self.args, self.in_specs, self.out_specs, self.out_shapes, self.bounds = [], [], [], [], []
        self.plan = []
        for w, layer, splits in riders:
            _, k, n = w.shape
            rows = 16
            while rows < k and (k % rows or k // rows > steps):
                rows *= 2
            carried = k % rows == 0 and k // rows <= steps
            self.plan.append((carried, w, layer, splits))
            if not carried:
                continue
            last = k // rows - 1

            def blk(*g, last=last):
                return jnp.minimum(step_of(*g), last)

            bounds = tuple(splits) if splits else ((0, n),)
            self.args.append(w)
            self.in_specs.append(pl.BlockSpec((None, rows, n), lambda *g, b=blk, l=layer: (l, b(*g), 0)))
            for lo, hi in bounds:
                self.out_specs.append(pl.BlockSpec((rows, hi - lo), lambda *g, b=blk: (b(*g), 0)))
                self.out_shapes.append(jax.ShapeDtypeStruct((k, hi - lo), bf16))
            self.bounds.append(bounds)

    def wrap(self, body, n_in, n_out):
        n_src = len(self.bounds)
        n_dst = len(self.out_specs)
        bounds = self.bounds

        def kern(*refs):
            ins, srcs = refs[:n_in], refs[n_in:n_in + n_src]
            o0 = n_in + n_src
            outs, dsts = refs[o0:o0 + n_out], iter(refs[o0 + n_out:o0 + n_out + n_dst])
            body(*ins, *outs, *refs[o0 + n_out + n_dst:])
            for src, bnd in zip(srcs, bounds):
                for lo, hi in bnd:
                    dst = next(dsts)
                    dst[...] = src[:, lo:hi].astype(dst.dtype)

        return kern

    def split(self, results, n_out):
        host, rest = results[:n_out], list(results[n_out:])
        per = []
        for carried, w, layer, splits in self.plan:
            if carried:
                count = len(splits) if splits else 1
                got, rest = rest[:count], rest[count:]
                per.append(got[0] if not splits else tuple(got))
            else:
                got = _cast_weight(w, layer, splits)
                per.append(tuple(got) if splits else got)
        return host, per


def _project(h, w, out_dtype, tn, *, tm_want=1024, h_buffers=2, col_scale=None, sigmoid=False,
             riders=(), name="proj"):
    rows, d = h.shape
    n = w.shape[1]
    nj = n // tn
    tm = _row_tile(rows, tm_want)
    single = dict(pipeline_mode=pl.Buffered(1)) if h_buffers == 1 else {}
    h_spec = pl.BlockSpec((tm, d), lambda i, j: (i, 0), **single)
    w_spec = pl.BlockSpec((d, tn), lambda i, j: (0, j))
    args, specs = [h, w], [h_spec, w_spec]
    if sigmoid:
        kern = _proj_sigmoid_kernel
    elif col_scale is not None:
        kern = _proj_scale_kernel
        args.append(col_scale)
        specs.append(pl.BlockSpec((1, tn), lambda i, j: (0, j)))
    else:
        kern = _proj_kernel
    ride = _Riders(riders, (rows // tm) * nj, lambda i, j: i * nj + j)
    res = pl.pallas_call(
        ride.wrap(kern, len(args), 1),
        grid=(rows // tm, nj),
        in_specs=specs + ride.in_specs,
        out_specs=[pl.BlockSpec((tm, tn), lambda i, j: (i, j))] + ride.out_specs,
        out_shape=[jax.ShapeDtypeStruct((rows, n), out_dtype)] + ride.out_shapes,
        compiler_params=_params("arbitrary", "arbitrary"),
        name=name,
    )(*args, *ride.args)
    (out,), extra = ride.split(res, 1)
    return out, extra


def _rope_rotate(x, cos, sin):
    n = x.shape[-1]
    lane = lax.broadcasted_iota(jnp.int32, x.shape, x.ndim - 1)
    up = pltpu.roll(x, n - 16, x.ndim - 1)
    down = pltpu.roll(x, 16, x.ndim - 1)
    return x * cos + jnp.where((lane % 32) < 16, up, down) * sin


def _proj_window_kernel(h_ref, w_ref, cos_ref, sin_ref, o_ref):
    h = h_ref[...]
    cos = cos_ref[...]
    sin = sin_ref[...]
    for cs in _col_chunks(C_WIDTH):
        acc = jnp.dot(h, w_ref[:, cs], preferred_element_type=f32)
        for j in range(acc.shape[1] // LANES):
            x = _rope_rotate(acc[:, j * LANES:(j + 1) * LANES], cos, sin) * (C_DH ** -0.5)
            o_ref[:, cs.start + j * LANES:cs.start + (j + 1) * LANES] = x.astype(o_ref.dtype)
    kv = jnp.dot(h, w_ref[:, C_WIDTH:C_WIDTH + 2 * LANES], preferred_element_type=f32)
    k = _rope_rotate(kv[:, :LANES], cos, sin)
    v = kv[:, LANES:]
    low = lax.broadcasted_iota(jnp.int32, k.shape, 1) < C_DH
    for idx, x in enumerate((k, v)):
        swapped = pltpu.roll(x, C_DH, 1)
        groups = (jnp.where(low, x, 0.0), jnp.where(low, 0.0, swapped),
                  jnp.where(low, swapped, 0.0), jnp.where(low, 0.0, x))
        base = C_WIDTH + idx * C_EXP
        for c, val in enumerate(groups):
            o_ref[:, base + c * LANES:base + (c + 1) * LANES] = val.astype(o_ref.dtype)


def _project_window(h, w, cos, sin):
    rows, d = h.shape
    n = w.shape[1]
    tm = _row_tile(rows, 1024)
    tok_tiles = cos.shape[0] // tm
    t_spec = pl.BlockSpec((tm, LANES), lambda i: (i % tok_tiles, 0))
    return pl.pallas_call(
        _proj_window_kernel,
        grid=(rows // tm,),
        in_specs=[pl.BlockSpec((tm, d), lambda i: (i, 0)),
                  pl.BlockSpec((d, n), lambda i: (0, 0), pipeline_mode=pl.Buffered(1)),
                  t_spec, t_spec],
        out_specs=pl.BlockSpec((tm, C_OUT_WIDTH), lambda i: (i, 0)),
        out_shape=jax.ShapeDtypeStruct((rows, C_OUT_WIDTH), bf16),
        compiler_params=_params("arbitrary"),
        name="proj_window",
    )(h, w, cos, sin)


def _hgrn_kernel(*refs, rev, has_s0, emit_state, readout, hp):
    it = iter(refs)
    q_ref, v_ref, f_ref, lb_ref = next(it), next(it), next(it), next(it)
    s0_ref = next(it) if has_s0 else None
    if readout:
        g_ref, oprev_ref, ng_ref = next(it), next(it), next(it)
    o_ref = next(it)
    sout_ref = next(it) if emit_state else None
    st_ref = next(it)

    blk = pl.program_id(2)
    nblk = pl.num_programs(2)
    tb = q_ref.shape[0]
    chunk = HGRN_CHUNK
    nchunk = tb // chunk
    width = hp * A_DK
    heads = range(hp)

    def head(a, h):
        return a[:, h * A_DK:(h + 1) * A_DK]

    @pl.when(blk == 0)
    def _():
        if has_s0:
            st_ref[...] = s0_ref[...]
        else:
            st_ref[...] = jnp.zeros_like(st_ref)

    lb = lb_ref[...]
    f = lb + (1.0 - lb) * _sigmoid(f_ref[...])
    logf = jnp.log(f)
    k = 1.0 - f

    row = lax.broadcasted_iota(jnp.int32, (tb, tb), 0)
    col = lax.broadcasted_iota(jnp.int32, (tb, tb), 1)
    same = (row // chunk) == (col // chunk)
    causal = (col >= row) if rev else (col <= row)
    mask = same & causal
    tri = jnp.where(mask, 1.0, 0.0).astype(bf16)

    hi = logf.astype(bf16)
    lo = (logf - hi.astype(f32)).astype(bf16)
    cum2 = jnp.dot(tri, jnp.concatenate([hi, lo], axis=1), preferred_element_type=f32)
    cum = cum2[:, :width] + cum2[:, width:]

    mid_rows, mid_decay = [], []
    for j in range(nchunk):
        mid = j * chunk + (chunk // 2 if rev else chunk // 2 - 1)
        at_mid = cum[mid:mid + 1, :]
        mid_rows.append(jnp.broadcast_to(at_mid, (chunk, width)))
        mid_decay.append(jnp.broadcast_to(jnp.exp(at_mid), (chunk, width)))
    rel = cum - jnp.concatenate(mid_rows, axis=0)
    q_rel = _silu(q_ref[...]) * jnp.exp(rel)
    q_att = q_rel.astype(bf16)
    k_inv = (k * jnp.exp(-rel)).astype(bf16)
    q_dec = (q_rel * jnp.concatenate(mid_decay, axis=0)).astype(bf16)
    v = v_ref[...].astype(bf16)
    att = [lax.dot_general(head(q_att, h), head(k_inv, h), _NT, preferred_element_type=f32)
           for h in heads]
    att = [jnp.where(mask, a, 0.0).astype(bf16) for a in att]
    o_intra = [jnp.dot(att[h], head(v, h), preferred_element_type=f32) for h in heads]

    order = range(nchunk - 1, -1, -1) if rev else range(nchunk)
    k_end, dec = {}, {}
    for j in order:
        sl = slice(j * chunk, (j + 1) * chunk)
        last = j * chunk if rev else (j + 1) * chunk - 1
        tot = cum[last:last + 1, :]
        k_end[j] = (k[sl] * jnp.exp(tot - cum[sl])).astype(bf16)
        dec[j] = jnp.exp(tot)
    zero = jnp.zeros((chunk, A_DK), bf16)
    upd = {}
    for h in heads:
        k_diag = jnp.concatenate(
            [jnp.concatenate([head(k_end[j], h) if c == j else zero for c in range(nchunk)], axis=1)
             for j in range(nchunk)], axis=0)
        u_all = lax.dot_general(head(v, h), k_diag, _TN, preferred_element_type=f32)
        for j in range(nchunk):
            upd[h, j] = u_all[:, j * A_DK:(j + 1) * A_DK]

    before = {}
    final = []
    for h in heads:
        s = st_ref[h]
        for j in order:
            before[h, j] = s.astype(bf16)
            s = s * head(dec[j], h) + upd[h, j]
        st_ref[h] = s
        final.append(s)

    outs = []
    for h in heads:
        o_inter = [lax.dot_general(head(q_dec[j * chunk:(j + 1) * chunk], h), before[h, j], _NT,
                                   preferred_element_type=f32) for j in range(nchunk)]
        outs.append(o_intra[h] + jnp.concatenate(o_inter, axis=0))

    if readout:
        ng = ng_ref[...]
        normed = []
        for h in heads:
            o = outs[h] + head(oprev_ref[...], h)
            normed.append(o * lax.rsqrt(jnp.mean(o * o, axis=-1, keepdims=True) + EPS) * ng)
        o_ref[...] = (jnp.concatenate(normed, axis=1) * _silu(g_ref[...])).astype(o_ref.dtype)
    else:
        o_ref[...] = jnp.concatenate(outs, axis=1)

    if emit_state:
        @pl.when(blk == nblk - 1)
        def _():
            for h in heads:
                sout_ref[h] = final[h]


def _hgrn_scan(proj, lb_dir, seq, direction, *, s0=None, emit_state=False, readout=None):
    rows = proj.shape[0]
    batch = rows // seq
    tb = min(HGRN_BLOCK, seq)
    nblk = seq // tb
    rev = direction == 1
    hp = HGRN_HEADS_PER_STEP
    hblocks = A_HEADS // hp
    width = hp * A_DK

    def tok(b, h, i):
        return b * nblk + (nblk - 1 - i if rev else i)

    def col_spec(group):
        return pl.BlockSpec((tb, width), lambda b, h, i: (tok(b, h, i), group * hblocks + h))

    state_spec = pl.BlockSpec((None, hp, A_DV, A_DK), lambda b, h, i: (b, h, 0, 0))
    args = [proj, proj, proj, lb_dir.reshape(1, A_QK)]
    specs = [col_spec(0), col_spec(1), col_spec(2 + direction),
             pl.BlockSpec((1, width), lambda b, h, i: (0, h))]
    if s0 is not None:
        args.append(s0)
        specs.append(state_spec)
    if readout is not None:
        o_prev, norm_g = readout
        args += [proj, o_prev, norm_g.reshape(1, A_DV)]
        specs += [col_spec(4),
                  pl.BlockSpec((tb, width), lambda b, h, i: (tok(b, h, i), h)),
                  pl.BlockSpec((1, A_DV), lambda b, h, i: (0, 0))]
    out_shape = [jax.ShapeDtypeStruct((rows, A_WIDTH), bf16 if readout is not None else f32)]
    out_specs = [pl.BlockSpec((tb, width), lambda b, h, i: (tok(b, h, i), h))]
    if emit_state:
        out_shape.append(jax.ShapeDtypeStruct((batch, A_HEADS, A_DV, A_DK), f32))
        out_specs.append(state_spec)
    kern = functools.partial(_hgrn_kernel, rev=rev, has_s0=s0 is not None,
                             emit_state=emit_state, readout=readout is not None, hp=hp)
    res = pl.pallas_call(
        kern,
        grid=(batch, hblocks, nblk),
        in_specs=specs,
        out_specs=out_specs,
        out_shape=out_shape,
        scratch_shapes=[pltpu.VMEM((hp, A_DV, A_DK), f32)],
        compiler_params=_params("arbitrary", "arbitrary", "arbitrary"),
        name="hgrn_scan",
    )(*args)
    return res if emit_state else res[0]


def _hgrn_mixer(proj_lat, proj_ctx, lb, norm_g, seq, ctx_len, need_ctx):
    o_c_f, s_f = _hgrn_scan(proj_ctx, lb[0], ctx_len, 0, emit_state=True)
    o_l_f = _hgrn_scan(proj_lat, lb[0], seq, 0, s0=s_f)
    if need_ctx:
        a_ctx, s_b = _hgrn_scan(proj_ctx, lb[1], ctx_len, 1, emit_state=True, readout=(o_c_f, norm_g))
    else:
        _, s_b = _hgrn_scan(proj_ctx, lb[1], ctx_len, 1, emit_state=True)
        a_ctx = None
    a_lat = _hgrn_scan(proj_lat, lb[1], seq, 1, s0=s_b, readout=(o_l_f, norm_g))
    return a_lat, a_ctx


def _attend(parts, sink=None):
    m = parts[0][0].max(axis=-1, keepdims=True)
    for s, _ in parts[1:]:
        m = jnp.maximum(m, s.max(axis=-1, keepdims=True))
    if sink is not None:
        m = jnp.maximum(m, sink)
    den = None
    acc = None
    for s, v in parts:
        e = jnp.exp(s - m)
        d = e.sum(axis=-1, keepdims=True)
        o = jnp.dot(e.astype(bf16), v, preferred_element_type=f32)
        den = d if den is None else den + d
        acc = o if acc is None else acc + o
    if sink is not None:
        den = den + jnp.exp(sink - m)
    return acc / den


def _na_kernel(types_ref, q_ref, k_ref, v_ref, kc_ref, vc_ref, tbl_ref, o_ref, *, grid_rows, hp):
    del types_ref
    step = pl.program_id(2)
    span = NA_UNION * GRID_W
    first = jnp.clip(step * NA_QROWS - NA_ROWS // 2, 0, grid_rows - NA_UNION)
    start = pl.multiple_of(first * GRID_W, GRID_W)
    for h in range(hp):
        hs = slice(h * B_DH, (h + 1) * B_DH)
        q = q_ref[:, hs]
        kn = k_ref[pl.ds(start, span), hs]
        vn = v_ref[pl.ds(start, span), hs]
        s_nb = lax.dot_general(q, kn, _NT, preferred_element_type=f32) + tbl_ref[h]
        s_cx = lax.dot_general(q, kc_ref[:, hs], _NT, preferred_element_type=f32)
        o = _attend([(s_nb, vn), (s_cx, vc_ref[:, hs])])
        o_ref[:, hs] = o.astype(o_ref.dtype)


def _na_tables(rpb, grid_rows):
    assert grid_rows >= NA_UNION and grid_rows % NA_QROWS == 0
    col = np.arange(GRID_W)
    col_off = np.clip(col[None, :] - col[:, None] + NA_COLS - 1, 0, 2 * NA_COLS - 2)
    col_start = np.clip(col - NA_COLS // 2, 0, GRID_W - NA_COLS)
    col_ok = (col[None, :] >= col_start[:, None]) & (col[None, :] < col_start[:, None] + NA_COLS)
    seen, types = {}, []
    for i in range(grid_rows // NA_QROWS):
        first = int(np.clip(i * NA_QROWS - NA_ROWS // 2, 0, grid_rows - NA_UNION))
        key_row = first + np.arange(NA_UNION)[None, :]
        r = i * NA_QROWS + np.arange(NA_QROWS)[:, None]
        row_start = np.clip(r - NA_ROWS // 2, 0, grid_rows - NA_ROWS)
        ok = (key_row >= row_start) & (key_row < row_start + NA_ROWS)
        assert (ok.sum(axis=1) == NA_ROWS).all()
        off = np.where(ok, key_row - r + NA_ROWS - 1, 0)
        sig = (ok.tobytes(), off.tobytes())
        if sig not in seen:
            seen[sig] = (len(seen), ok, off)
        types.append(seen[sig][0])
    onehot = jnp.asarray((col_off[None] == np.arange(2 * NA_COLS - 1)[:, None, None]).astype(np.float32))
    toe = jnp.einsum("hoc,cqk->hoqk", rpb.astype(f32), onehot, precision=lax.Precision.HIGHEST)
    toe = jnp.where(col_ok[None, None], toe, NEG_INF)
    neg = jnp.full((rpb.shape[0], GRID_W, GRID_W), NEG_INF, f32)
    tables = []
    for _, ok, off in sorted(seen.values(), key=lambda t: t[0]):
        rows = [jnp.concatenate([toe[:, off[rl, a]] if ok[rl, a] else neg for a in range(NA_UNION)],
                                axis=-1) for rl in range(NA_QROWS)]
        tables.append(jnp.concatenate(rows, axis=-2))
    return jnp.stack(tables, axis=1), jnp.asarray(np.array(types, np.int32))


def _neighborhood_attention(qkv_lat, qkv_ctx, tables, types, seq, ctx_len):
    rows = qkv_lat.shape[0]
    batch = rows // seq
    tq = NA_QROWS * GRID_W
    nq = seq // tq
    hp = NA_HEADS_PER_STEP
    hb = B_HEADS // hp
    width = hp * B_DH
    kern = functools.partial(_na_kernel, grid_rows=seq // GRID_W, hp=hp)
    grid_spec = pltpu.PrefetchScalarGridSpec(
        num_scalar_prefetch=1,
        grid=(batch, hb, nq),
        in_specs=[
            pl.BlockSpec((tq, width), lambda b, h, i, t: (b * nq + i, h)),
            pl.BlockSpec((seq, width), lambda b, h, i, t: (b, hb + h), pipeline_mode=pl.Buffered(1)),
            pl.BlockSpec((seq, width), lambda b, h, i, t: (b, 2 * hb + h), pipeline_mode=pl.Buffered(1)),
            pl.BlockSpec((ctx_len, width), lambda b, h, i, t: (b, hb + h)),
            pl.BlockSpec((ctx_len, width), lambda b, h, i, t: (b, 2 * hb + h)),
            pl.BlockSpec((hp, None, tq, NA_UNION * GRID_W), lambda b, h, i, t: (h, t[i], 0, 0)),
        ],
        out_specs=pl.BlockSpec((tq, width), lambda b, h, i, t: (b * nq + i, h)),
    )
    return pl.pallas_call(
        kern,
        grid_spec=grid_spec,
        out_shape=jax.ShapeDtypeStruct((rows, B_WIDTH), bf16),
        compiler_params=_params("arbitrary", "arbitrary", "arbitrary"),
        name="neighborhood_attention",
    )(types, qkv_lat, qkv_lat, qkv_lat, qkv_ctx, qkv_ctx, tables)


def _window_kernel(sink_ref, q_ref, k_ref, v_ref, kc_ref, vc_ref, o_ref, *, seq):
    n = pl.program_id(1)
    span = 3 * C_BLOCK
    start = pl.multiple_of(jnp.clip((n - 1) * C_BLOCK, 0, seq - span), C_BLOCK)
    pairs = C_PAIRS // C_KV_HEADS
    rows = pairs * C_BLOCK
    qpos = n * C_BLOCK + lax.broadcasted_iota(jnp.int32, (C_BLOCK, span), 0)
    kpos = start + lax.broadcasted_iota(jnp.int32, (C_BLOCK, span), 1)
    off_window = jnp.where(jnp.abs(qpos - kpos) <= C_WINDOW, 0.0, NEG_INF)
    off_window = jnp.concatenate([off_window] * pairs, axis=0)
    pair_id = lax.broadcasted_iota(jnp.int32, (rows, 1), 0) // C_BLOCK

    chains = [(g, e) for g in range(C_KV_HEADS) for e in range(2)]
    q = {g: jnp.concatenate([q_ref[:, (g * pairs + p) * LANES:(g * pairs + p + 1) * LANES]
                             for p in range(pairs)], axis=0) for g in range(C_KV_HEADS)}
    s_w, s_c, sink = {}, {}, {}
    for g, e in chains:
        cs = slice((2 * g + e) * LANES, (2 * g + e + 1) * LANES)
        s = lax.dot_general(q[g], k_ref[pl.ds(start, span), cs], _NT, preferred_element_type=f32)
        s_w[g, e] = s + off_window
        s_c[g, e] = lax.dot_general(q[g], kc_ref[:, cs], _NT, preferred_element_type=f32)
        col = jnp.zeros((rows, 1), f32)
        for p in range(pairs):
            col = jnp.where(pair_id == p, sink_ref[(g * pairs + p) * 2 + e], col)
        sink[g, e] = col
    out = {}
    for g, e in chains:
        cs = slice((2 * g + e) * LANES, (2 * g + e + 1) * LANES)
        out[g, e] = _attend([(s_w[g, e], v_ref[pl.ds(start, span), cs]), (s_c[g, e], vc_ref[:, cs])],
                            sink=sink[g, e])
    for g in range(C_KV_HEADS):
        o = out[g, 0] + out[g, 1]
        for p in range(pairs):
            o_ref[:, (g * pairs + p) * LANES:(g * pairs + p + 1) * LANES] = (
                o[p * C_BLOCK:(p + 1) * C_BLOCK].astype(o_ref.dtype))


def _window_attention(qkv_lat, qkv_ctx, sink, seq, ctx_len):
    rows = qkv_lat.shape[0]
    batch = rows // seq
    nq = seq // C_BLOCK
    kblk = C_WIDTH // C_EXP
    kern = functools.partial(_window_kernel, seq=seq)
    resident = dict(pipeline_mode=pl.Buffered(1))
    return pl.pallas_call(
        kern,
        grid=(batch, nq),
        in_specs=[
            pl.BlockSpec(memory_space=pltpu.SMEM),
            pl.BlockSpec((C_BLOCK, C_WIDTH), lambda b, i: (b * nq + i, 0)),
            pl.BlockSpec((seq, C_EXP), lambda b, i: (b, kblk), **resident),
            pl.BlockSpec((seq, C_EXP), lambda b, i: (b, kblk + 1), **resident),
            pl.BlockSpec((ctx_len, C_EXP), lambda b, i: (b, kblk)),
            pl.BlockSpec((ctx_len, C_EXP), lambda b, i: (b, kblk + 1)),
        ],
        out_specs=pl.BlockSpec((C_BLOCK, C_WIDTH), lambda b, i: (b * nq + i, 0)),
        out_shape=jax.ShapeDtypeStruct((rows, C_WIDTH), bf16),
        compiler_params=_params("arbitrary", "arbitrary"),
        name="window_attention",
    )(sink.astype(f32), qkv_lat, qkv_lat, qkv_lat, qkv_ctx, qkv_ctx)


def _ctx_attn_kernel(*refs, heads, dh, k_cols, v_cols, has_sink):
    if has_sink:
        sink_ref, qkv_ref, o_ref = refs
    else:
        qkv_ref, o_ref = refs
    group = heads // len(k_cols)
    for kh, (kc, vc) in enumerate(zip(k_cols, v_cols)):
        k = qkv_ref[:, kc:kc + dh]
        v = qkv_ref[:, vc:vc + dh]
        for g in range(group):
            hq = kh * group + g
            q = qkv_ref[:, hq * dh:(hq + 1) * dh]
            s = lax.dot_general(q, k, _NT, preferred_element_type=f32)
            o = _attend([(s, v)], sink=sink_ref[hq] if has_sink else None)
            o_ref[:, hq * dh:(hq + 1) * dh] = o.astype(o_ref.dtype)


def _context_attention(qkv_ctx, ctx_len, heads, dh, k_cols, v_cols, sink=None):
    rows, width = qkv_ctx.shape
    kern = functools.partial(_ctx_attn_kernel, heads=heads, dh=dh, k_cols=k_cols, v_cols=v_cols,
                             has_sink=sink is not None)
    args, specs = [qkv_ctx], [pl.BlockSpec((ctx_len, width), lambda b: (b, 0))]
    if sink is not None:
        args.insert(0, sink.astype(f32))
        specs.insert(0, pl.BlockSpec(memory_space=pltpu.SMEM))
    return pl.pallas_call(
        kern,
        grid=(rows // ctx_len,),
        in_specs=specs,
        out_specs=pl.BlockSpec((ctx_len, heads * dh), lambda b: (b, 0)),
        out_shape=jax.ShapeDtypeStruct((rows, heads * dh), bf16),
        compiler_params=_params("arbitrary"),
        name="context_attention",
    )(*args)


def _merge_kernel(oa_ref, ob_ref, oc_ref, wa_ref, wb_ref, wc_ref, ga_ref, gb_ref, gc_ref, o_ref):
    oa, ob, oc = oa_ref[...], ob_ref[...], oc_ref[...]
    for cs in _col_chunks(o_ref.shape[1]):
        br_a = jnp.dot(oa, wa_ref[:, cs], preferred_element_type=f32)
        br_b = jnp.dot(ob, wb_ref[:, cs], preferred_element_type=f32)
        br_c = jnp.dot(oc, wc_ref[:, cs], preferred_element_type=f32)
        m = ga_ref[:, cs] * br_a + gb_ref[:, cs] * br_b + gc_ref[:, cs] * br_c
        o_ref[:, cs] = m.astype(o_ref.dtype)


def _merge(o_a, o_b, o_c, gates, w_branch, riders=()):
    rows = o_a.shape[0]
    d = w_branch.shape[1]
    tm = _row_tile(rows, 1024)
    tn = 512
    nj = d // tn
    assert A_WIDTH % B_WIDTH == 0 and B_WIDTH == C_WIDTH
    b_blk = A_WIDTH // B_WIDTH
    ride = _Riders(riders, (rows // tm) * nj, lambda i, j: i * nj + j)
    res = pl.pallas_call(
        ride.wrap(_merge_kernel, 9, 1),
        grid=(rows // tm, nj),
        in_specs=[
            pl.BlockSpec((tm, A_WIDTH), lambda i, j: (i, 0)),
            pl.BlockSpec((tm, B_WIDTH), lambda i, j: (i, 0)),
            pl.BlockSpec((tm, C_WIDTH), lambda i, j: (i, 0)),
            pl.BlockSpec((A_WIDTH, tn), lambda i, j: (0, j)),
            pl.BlockSpec((B_WIDTH, tn), lambda i, j: (b_blk, j)),
            pl.BlockSpec((C_WIDTH, tn), lambda i, j: (b_blk + 1, j)),
            pl.BlockSpec((tm, tn), lambda i, j: (i, j)),
            pl.BlockSpec((tm, tn), lambda i, j: (i, nj + j)),
            pl.BlockSpec((tm, tn), lambda i, j: (i, 2 * nj + j)),
        ] + ride.in_specs,
        out_specs=[pl.BlockSpec((tm, tn), lambda i, j: (i, j))] + ride.out_specs,
        out_shape=[jax.ShapeDtypeStruct((rows, d), bf16)] + ride.out_shapes,
        compiler_params=_params("arbitrary", "arbitrary"),
        name="merge_branches",
    )(o_a, o_b, o_c, w_branch, w_branch, w_branch, gates, gates, gates, *ride.args)
    (out,), extra = ride.split(res, 1)
    return out, extra


def _row_halves(m):
    return [slice(0, m // 2), slice(m // 2, m)] if m % 32 == 0 else [slice(0, m)]


def _residual_kernel(a_ref, w_ref, x_ref, gate_ref, o_ref):
    if o_ref.shape[1] > 256:
        a = a_ref[...]
        for cs in _col_chunks(o_ref.shape[1]):
            y = jnp.dot(a, w_ref[:, cs], preferred_element_type=f32)
            o_ref[:, cs] = x_ref[:, cs] + gate_ref[:, cs] * y
    else:
        w = w_ref[...]
        for rs in _row_halves(o_ref.shape[0]):
            y = jnp.dot(a_ref[rs, :], w, preferred_element_type=f32)
            o_ref[rs, :] = x_ref[rs, :] + gate_ref[...] * y


def _gated_residual_matmul(a, w, x, mod, gate_idx, rows_per_group, tn, name, a_buffers=1):
    rows, kdim = a.shape
    d = w.shape[1]
    tm = _row_tile(rows_per_group, 1024)
    per = rows_per_group // tm
    nj = d // tn
    single = dict(pipeline_mode=pl.Buffered(1)) if a_buffers == 1 else {}
    return pl.pallas_call(
        _residual_kernel,
        grid=(rows // tm, nj),
        in_specs=[
            pl.BlockSpec((tm, kdim), lambda i, j: (i, 0), **single),
            pl.BlockSpec((kdim, tn), lambda i, j: (0, j)),
            pl.BlockSpec((tm, tn), lambda i, j: (i, j)),
            pl.BlockSpec((None, 1, tn), lambda i, j: (i // per, 0, gate_idx * nj + j)),
        ],
        out_specs=pl.BlockSpec((tm, tn), lambda i, j: (i, j)),
        out_shape=jax.ShapeDtypeStruct((rows, d), f32),
        compiler_params=_params("arbitrary", "arbitrary"),
        name=name,
    )(a, w, x, mod)


def _residual_ksplit_kernel(a0_ref, a1_ref, w0_ref, w1_ref, x_ref, gate_ref, o_ref):
    w0, w1 = w0_ref[...], w1_ref[...]
    for rs in _row_halves(o_ref.shape[0]):
        y = (jnp.dot(a0_ref[rs, :], w0, preferred_element_type=f32)
             + jnp.dot(a1_ref[rs, :], w1, preferred_element_type=f32))
        o_ref[rs, :] = x_ref[rs, :] + gate_ref[...] * y


def _gated_residual_matmul_ksplit(a, w, x, mod, gate_idx, rows_per_group, tn, name):
    rows, kdim = a.shape
    d = w.shape[1]
    kh = kdim // 2
    assert kdim % 2 == 0 and kh % LANES == 0
    tm = _row_tile(rows_per_group, 1024)
    per = rows_per_group // tm
    nj = d // tn
    return pl.pallas_call(
        _residual_ksplit_kernel,
        grid=(rows // tm, nj),
        in_specs=[
            pl.BlockSpec((tm, kh), lambda i, j: (i, 0), pipeline_mode=pl.Buffered(1)),
            pl.BlockSpec((tm, kh), lambda i, j: (i, 1)),
            pl.BlockSpec((kh, tn), lambda i, j: (0, j)),
            pl.BlockSpec((kh, tn), lambda i, j: (1, j)),
            pl.BlockSpec((tm, tn), lambda i, j: (i, j)),
            pl.BlockSpec((None, 1, tn), lambda i, j: (i // per, 0, gate_idx * nj + j)),
        ],
        out_specs=pl.BlockSpec((tm, tn), lambda i, j: (i, j)),
        out_shape=jax.ShapeDtypeStruct((rows, d), f32),
        compiler_params=_params("arbitrary", "arbitrary"),
        name=name,
    )(a, a, w, w, x, mod)


def _swiglu_kernel(h_ref, wg_ref, wu_ref, o_ref):
    wg, wu = wg_ref[...], wu_ref[...]
    for rs in _row_halves(o_ref.shape[0]):
        h = h_ref[rs, :]
        g = jnp.dot(h, wg, preferred_element_type=f32)
        u = jnp.dot(h, wu, preferred_element_type=f32)
        o_ref[rs, :] = (g * _sigmoid_tanh(g) * u).astype(o_ref.dtype)


def _swiglu_up(h, w_gate, w_up, riders=()):
    rows, d = h.shape
    n = w_gate.shape[1]
    tm = _row_tile(rows, 1024)
    tn = FFN_TN
    nj = n // tn
    w_spec = pl.BlockSpec((d, tn), lambda i, j: (0, j))
    ride = _Riders(riders, (rows // tm) * nj, lambda i, j: i * nj + j)
    res = pl.pallas_call(
        ride.wrap(_swiglu_kernel, 3, 1),
        grid=(rows // tm, nj),
        in_specs=[pl.BlockSpec((tm, d), lambda i, j: (i, 0)), w_spec, w_spec] + ride.in_specs,
        out_specs=[pl.BlockSpec((tm, tn), lambda i, j: (i, j))] + ride.out_specs,
        out_shape=[jax.ShapeDtypeStruct((rows, n), bf16)] + ride.out_shapes,
        compiler_params=_params("arbitrary", "arbitrary"),
        name="swiglu_up",
    )(h, w_gate, w_up, *ride.args)
    (out,), extra = ride.split(res, 1)
    return out, extra


def _rope_tables(seq):
    half = C_DH // 2
    pos = jnp.arange(seq)
    inv = ROPE_BASE ** (-jnp.arange(0, half, 2, dtype=f32) / half)
    ang_row = (pos // GRID_W).astype(f32)[:, None] * inv[None, :]
    ang_col = (pos % GRID_W).astype(f32)[:, None] * inv[None, :]

    def one(ang):
        return (jnp.concatenate([jnp.cos(ang), jnp.cos(ang)], axis=-1),
                jnp.concatenate([-jnp.sin(ang), jnp.sin(ang)], axis=-1))

    cr, sr = one(ang_row)
    cc, sc = one(ang_col)
    cos = jnp.concatenate([cr, cc], axis=-1)
    sin = jnp.concatenate([sr, sc], axis=-1)
    reps = LANES // C_DH
    return jnp.tile(cos, (1, reps)), jnp.tile(sin, (1, reps))


def kernel(x, c, ctx, c_ctx, norm1_g, norm2_g, w_mod, b_mod, w_in, hgrn_lb, a_norm_g, na_rpb,
           c_sink, w_branch, w_out, w_ffn_gate, w_ffn_up, w_ffn_down, final_norm_g):
    batch, seq, d = x.shape
    ctx_len = ctx.shape[1]
    depth = w_in.shape[0]
    n_lat, n_ctx = batch * seq, batch * ctx_len

    lb_w = jax.nn.softmax(hgrn_lb.astype(f32), axis=0)
    lower_bounds = jnp.cumsum(lb_w, axis=0) - lb_w[:1]

    mod_rows = 8 * (-(-(batch + 1) // 8))
    c_rows = jnp.zeros((mod_rows, d), f32).at[:batch].set(c).at[batch].set(c_ctx)
    mod_all = _modulation(c_rows, w_mod, b_mod)

    cos_t, sin_t = _rope_tables(seq)
    ones_t = jnp.ones((n_ctx, LANES), f32)
    zeros_t = jnp.zeros((n_ctx, LANES), f32)

    a_hi = 3 * A_QK + 2 * A_WIDTH
    b_hi = a_hi + 3 * B_WIDTH
    c_hi = b_hi + C_WIDTH + 2 * C_KV_WIDTH
    in_width = w_in.shape[2]
    b_scale = jnp.concatenate([jnp.full((B_WIDTH,), B_DH ** -0.5, f32),
                               jnp.ones((2 * B_WIDTH,), f32)]).reshape(1, -1)
    ck_cols = tuple(C_WIDTH + 2 * g * LANES for g in range(C_KV_HEADS))
    cv_cols = tuple(C_WIDTH + C_EXP + 2 * g * LANES for g in range(C_KV_HEADS))
    bk_cols = tuple(B_WIDTH + h * B_DH for h in range(B_HEADS))
    bv_cols = tuple(2 * B_WIDTH + h * B_DH for h in range(B_HEADS))

    x_lat = x.reshape(n_lat, d)
    x_ctx = ctx.reshape(n_ctx, d)

    in_splits = ((0, a_hi), (a_hi, b_hi), (b_hi, c_hi), (c_hi, in_width))
    w_in_cast = _cast_weight(w_in, 0, in_splits)

    for l in range(depth):
        need_ctx = l < depth - 1
        mod_l = mod_all[l, :batch].reshape(batch, 1, N_MOD * d)
        mod_c = mod_all[l, batch:batch + 1].reshape(1, 1, N_MOD * d)
        w_a, w_b, w_c, w_g = w_in_cast
        na_tables, na_types = _na_tables(na_rpb[l], seq // GRID_W)

        h_lat = _norm_modulate(x_lat, norm1_g[l], mod_l, 0, seq)
        h_ctx = _norm_modulate(x_ctx, norm1_g[l], mod_c, 0, n_ctx)

        pa_lat, (wfu,) = _project(h_lat, w_a, f32, 1024, riders=[(w_ffn_up, l, None)], name="proj_hgrn")
        pb_lat, (wbr, wo) = _project(h_lat, w_b, bf16, 1024, col_scale=b_scale,
                                     riders=[(w_branch, l, None), (w_out, l, None)], name="proj_na")
        pc_lat = _project_window(h_lat, w_c, cos_t, sin_t)
        g_lat, (wfg, wfd) = _project(h_lat, w_g, bf16, 1024, sigmoid=True,
                                     riders=[(w_ffn_gate, l, None), (w_ffn_down, l, None)], name="proj_gates")
        pa_ctx, _ = _project(h_ctx, w_a, f32, 1024, name="proj_hgrn")
        pb_ctx, _ = _project(h_ctx, w_b, bf16, 512, col_scale=b_scale, name="proj_na")
        pc_ctx = _project_window(h_ctx, w_c, ones_t, zeros_t)

        a_lat, a_ctx = _hgrn_mixer(pa_lat, pa_ctx, lower_bounds[l], a_norm_g[l], seq, ctx_len, need_ctx)
        b_lat = _neighborhood_attention(pb_lat, pb_ctx, na_tables, na_types, seq, ctx_len)
        c_lat = _window_attention(pc_lat, pc_ctx, c_sink[l], seq, ctx_len)

        m_lat, _ = _merge(a_lat, b_lat, c_lat, g_lat, wbr)
        x_lat = _gated_residual_matmul(m_lat, wo, x_lat, mod_l, 2, seq, 512, "out_proj", a_buffers=2)
        h2 = _norm_modulate(x_lat, norm2_g[l], mod_l, 3, seq)
        next_in = [(w_in, l + 1, in_splits)] if l + 1 < depth else []
        u, nxt = _swiglu_up(h2, wfg, wfu, riders=next_in)
        if nxt:
            w_in_cast = nxt[0]
        x_lat = _gated_residual_matmul_ksplit(u, wfd, x_lat, mod_l, 5, seq, FFN_TN, "ffn_down")

        if need_ctx:
            g_ctx, _ = _project(h_ctx, w_g, bf16, 1024, sigmoid=True, name="proj_gates")
            b_ctx = _context_attention(pb_ctx, ctx_len, B_HEADS, B_DH, bk_cols, bv_cols)
            c_ctx_o = _context_attention(pc_ctx, ctx_len, C_HEADS, C_DH, ck_cols, cv_cols, sink=c_sink[l])
            m_ctx, _ = _merge(a_ctx, b_ctx, c_ctx_o, g_ctx, wbr)
            x_ctx = _gated_residual_matmul(m_ctx, wo, x_ctx, mod_c, 2, n_ctx, 1024, "out_proj")
            h2c = _norm_modulate(x_ctx, norm2_g[l], mod_c, 3, n_ctx)
            uc, _ = _swiglu_up(h2c, wfg, wfu)
            x_ctx = _gated_residual_matmul(uc, wfd, x_ctx, mod_c, 5, n_ctx, FFN_TN, "ffn_down")

    return _final_norm(x_lat, final_norm_g).reshape(batch, seq, d)
```

```python
import functools

import jax
import jax.numpy as jnp
import numpy as np
from jax import lax
from jax.experimental import pallas as pl
from jax.experimental.pallas import tpu as pltpu

GRID_W = 64
EPS = 1e-6
NEG_INF = -1e30
N_MOD = 6
A_HEADS, A_DK, A_DV, A_CHUNK = 16, 128, 128, 32
A_QK = A_HEADS * A_DK
A_WIDTH = A_HEADS * A_DV
B_HEADS, B_DH = 8, 128
B_WIDTH = B_HEADS * B_DH
NA_ROWS, NA_COLS = 8, 16
C_HEADS, C_KV_HEADS, C_DH = 16, 2, 64
C_WIDTH = C_HEADS * C_DH
C_KV_WIDTH = C_KV_HEADS * C_DH
C_WINDOW = 128
C_BLOCK = 128
ROPE_BASE = 10000.0

LANES = 128
VMEM_LIMIT_BYTES = 56 * 1024 * 1024

HGRN_BLOCK = 256
HGRN_CHUNK = 2 * A_CHUNK
HGRN_HEADS_PER_STEP = 16
NA_QROWS = 4
NA_UNION = NA_ROWS + NA_QROWS
NA_HEADS_PER_STEP = 8
NORM_ROWS = 512
FFN_TN = 256
CAST_BLOCK_BYTES = 4 * 1024 * 1024

C_PAIRS = C_WIDTH // LANES
C_EXP = 2 * C_KV_HEADS * LANES
C_OUT_WIDTH = C_WIDTH + 2 * C_EXP

_NT = (((1,), (1,)), ((), ()))
_TN = (((0,), (0,)), ((), ()))

bf16 = jnp.bfloat16
f32 = jnp.float32


def _params(*sem):
    return pltpu.CompilerParams(dimension_semantics=sem, vmem_limit_bytes=VMEM_LIMIT_BYTES)


def _row_tile(rows, want):
    t = min(rows, want)
    while rows % t:
        t //= 2
    return t


def _sigmoid(x):
    return 1.0 / (1.0 + jnp.exp(-x))


def _silu(x):
    return x * _sigmoid(x)


def _sigmoid_tanh(x):
    return 0.5 * jnp.tanh(0.5 * x) + 0.5


def _col_chunks(n, width=256):
    width = min(width, n)
    return [slice(c, c + width) for c in range(0, n, width)]


def _mod_kernel(c_ref, w_ref, b_ref, o_ref):
    a = _silu(c_ref[...]).astype(bf16)
    o_ref[...] = jnp.dot(a, w_ref[...].astype(bf16), preferred_element_type=f32) + b_ref[...]


def _modulation(c_rows, w_mod, b_mod):
    depth, d, n = w_mod.shape
    rows = c_rows.shape[0]
    tn = 512
    return pl.pallas_call(
        _mod_kernel,
        grid=(depth, n // tn),
        in_specs=[
            pl.BlockSpec((rows, d), lambda l, j: (0, 0)),
            pl.BlockSpec((None, d, tn), lambda l, j: (l, 0, j)),
            pl.BlockSpec((None, 1, tn), lambda l, j: (l, 0, j)),
        ],
        out_specs=pl.BlockSpec((None, rows, tn), lambda l, j: (l, 0, j)),
        out_shape=jax.ShapeDtypeStruct((depth, rows, n), f32),
        compiler_params=_params("arbitrary", "arbitrary"),
        name="modulation",
    )(c_rows, w_mod, b_mod.reshape(depth, 1, n))


def _norm_mod_kernel(x_ref, g_ref, shift_ref, scale_ref, o_ref):
    x = x_ref[...]
    y = x * lax.rsqrt(jnp.mean(x * x, axis=-1, keepdims=True) + EPS) * g_ref[...]
    o_ref[...] = (y * (1.0 + scale_ref[...]) + shift_ref[...]).astype(o_ref.dtype)


def _norm_kernel(x_ref, g_ref, o_ref):
    x = x_ref[...]
    y = x * lax.rsqrt(jnp.mean(x * x, axis=-1, keepdims=True) + EPS) * g_ref[...]
    o_ref[...] = y.astype(o_ref.dtype)


def _norm_modulate(x, g, mod, shift_idx, rows_per_group):
    rows, d = x.shape
    tm = _row_tile(rows_per_group, NORM_ROWS)
    per = rows_per_group // tm
    return pl.pallas_call(
        _norm_mod_kernel,
        grid=(rows // tm,),
        in_specs=[
            pl.BlockSpec((tm, d), lambda i: (i, 0)),
            pl.BlockSpec((1, d), lambda i: (0, 0)),
            pl.BlockSpec((None, 1, d), lambda i: (i // per, 0, shift_idx)),
            pl.BlockSpec((None, 1, d), lambda i: (i // per, 0, shift_idx + 1)),
        ],
        out_specs=pl.BlockSpec((tm, d), lambda i: (i, 0)),
        out_shape=jax.ShapeDtypeStruct((rows, d), bf16),
        compiler_params=_params("arbitrary"),
        name="norm_modulate",
    )(x, g.reshape(1, d), mod, mod)


def _final_norm(x, g):
    rows, d = x.shape
    tm = _row_tile(rows, NORM_ROWS)
    return pl.pallas_call(
        _norm_kernel,
        grid=(rows // tm,),
        in_specs=[pl.BlockSpec((tm, d), lambda i: (i, 0)), pl.BlockSpec((1, d), lambda i: (0, 0))],
        out_specs=pl.BlockSpec((tm, d), lambda i: (i, 0)),
        out_shape=jax.ShapeDtypeStruct((rows, d), f32),
        compiler_params=_params("arbitrary"),
        name="final_norm",
    )(x, g.reshape(1, d))


def _proj_kernel(h_ref, w_ref, o_ref):
    h = h_ref[...]
    for cs in _col_chunks(o_ref.shape[1]):
        o_ref[:, cs] = jnp.dot(h, w_ref[:, cs], preferred_element_type=f32).astype(o_ref.dtype)


def _proj_scale_kernel(h_ref, w_ref, s_ref, o_ref):
    h = h_ref[...]
    for cs in _col_chunks(o_ref.shape[1]):
        acc = jnp.dot(h, w_ref[:, cs], preferred_element_type=f32)
        o_ref[:, cs] = (acc * s_ref[:, cs]).astype(o_ref.dtype)


def _proj_sigmoid_kernel(h_ref, w_ref, o_ref):
    h = h_ref[...]
    for cs in _col_chunks(o_ref.shape[1]):
        acc = jnp.dot(h, w_ref[:, cs], preferred_element_type=f32)
        o_ref[:, cs] = _sigmoid_tanh(acc).astype(o_ref.dtype)


def _cast_kernel(w_ref, *o_refs, bounds):
    for o_ref, (lo, hi) in zip(o_refs, bounds):
        o_ref[...] = w_ref[:, lo:hi].astype(o_ref.dtype)


def _cast_weight(w, layer, splits=None):
    _, k, n = w.shape
    bounds = tuple(splits) if splits else ((0, n),)
    tk = 1 << ((CAST_BLOCK_BYTES // (4 * n)).bit_length() - 1)
    while k % tk:
        tk //= 2
    outs = pl.pallas_call(
        functools.partial(_cast_kernel, bounds=bounds),
        grid=(k // tk,),
        in_specs=[pl.BlockSpec((None, tk, n), lambda i: (layer, i, 0))],
        out_specs=[pl.BlockSpec((tk, hi - lo), lambda i: (i, 0)) for lo, hi in bounds],
        out_shape=[jax.ShapeDtypeStruct((k, hi - lo), bf16) for lo, hi in bounds],
        compiler_params=_params("arbitrary"),
        name="cast_weight",
    )(w)
    return outs if splits else outs[0]


class _Riders:
    def __init__(self, riders, steps, step_of):
        self.args, self.in_specs, self.out_specs, self.out_shapes, self.bounds = [], [], [], [], []
        self.plan = []
        for w, layer, splits in riders:
            _, k, n = w.shape
            rows = 16
            while rows < k and (k % rows or k // rows > steps):
                rows *= 2
            carried = k % rows == 0 and k // rows <= steps
            self.plan.append((carried, w, layer, splits))
            if not carried:
                continue
            last = k // rows - 1

            def blk(*g, last=last):
                return jnp.minimum(step_of(*g), last)

            bounds = tuple(splits) if splits else ((0, n),)
            self.args.append(w)
            self.in_specs.append(pl.BlockSpec((None, rows, n), lambda *g, b=blk, l=layer: (l, b(*g), 0)))
            for lo, hi in bounds:
                self.out_specs.append(pl.BlockSpec((rows, hi - lo), lambda *g, b=blk: (b(*g), 0)))
                self.out_shapes.append(jax.ShapeDtypeStruct((k, hi - lo), bf16))
            self.bounds.append(bounds)

    def wrap(self, body, n_in, n_out):
        n_src = len(self.bounds)
        n_dst = len(self.out_specs)
        bounds = self.bounds

        def kern(*refs):
            ins, srcs = refs[:n_in], refs[n_in:n_in + n_src]
            o0 = n_in + n_src
            outs, dsts = refs[o0:o0 + n_out], iter(refs[o0 + n_out:o0 + n_out + n_dst])
            body(*ins, *outs, *refs[o0 + n_out + n_dst:])
            for src, bnd in zip(srcs, bounds):
                for lo, hi in bnd:
                    dst = next(dsts)
                    dst[...] = src[:, lo:hi].astype(dst.dtype)

        return kern

    def split(self, results, n_out):
        host, rest = results[:n_out], list(results[n_out:])
        per = []
        for carried, w, layer, splits in self.plan:
            if carried:
                count = len(splits) if splits else 1
                got, rest = rest[:count], rest[count:]
                per.append(got[0] if not splits else tuple(got))
            else:
                got = _cast_weight(w, layer, splits)
                per.append(tuple(got) if splits else got)
        return host, per


def _project(h, w, out_dtype, tn, *, tm_want=1024, h_buffers=2, col_scale=None, sigmoid=False,
             riders=(), name="proj"):
    rows, d = h.shape
    n = w.shape[1]
    nj = n // tn
    tm = _row_tile(rows, tm_want)
    single = dict(pipeline_mode=pl.Buffered(1)) if h_buffers == 1 else {}
    h_spec = pl.BlockSpec((tm, d), lambda i, j: (i, 0), **single)
    w_spec = pl.BlockSpec((d, tn), lambda i, j: (0, j))
    args, specs = [h, w], [h_spec, w_spec]
    if sigmoid:
        kern = _proj_sigmoid_kernel
    elif col_scale is not None:
        kern = _proj_scale_kernel
        args.append(col_scale)
        specs.append(pl.BlockSpec((1, tn), lambda i, j: (0, j)))
    else:
        kern = _proj_kernel
    ride = _Riders(riders, (rows // tm) * nj, lambda i, j: i * nj + j)
    res = pl.pallas_call(
        ride.wrap(kern, len(args), 1),
        grid=(rows // tm, nj),
        in_specs=specs + ride.in_specs,
        out_specs=[pl.BlockSpec((tm, tn), lambda i, j: (i, j))] + ride.out_specs,
        out_shape=[jax.ShapeDtypeStruct((rows, n), out_dtype)] + ride.out_shapes,
        compiler_params=_params("arbitrary", "arbitrary"),
        name=name,
    )(*args, *ride.args)
    (out,), extra = ride.split(res, 1)
    return out, extra


def _rope_rotate(x, cos, sin):
    n = x.shape[-1]
    lane = lax.broadcasted_iota(jnp.int32, x.shape, x.ndim - 1)
    up = pltpu.roll(x, n - 16, x.ndim - 1)
    down = pltpu.roll(x, 16, x.ndim - 1)
    return x * cos + jnp.where((lane % 32) < 16, up, down) * sin


def _proj_window_kernel(h_ref, w_ref, cos_ref, sin_ref, o_ref):
    acc = jnp.dot(h_ref[...], w_ref[...], preferred_element_type=f32)
    cos = cos_ref[...]
    sin = sin_ref[...]
    for j in range(C_PAIRS):
        x = _rope_rotate(acc[:, j * LANES:(j + 1) * LANES], cos, sin) * (C_DH ** -0.5)
        o_ref[:, j * LANES:(j + 1) * LANES] = x.astype(o_ref.dtype)
    k = _rope_rotate(acc[:, C_WIDTH:C_WIDTH + LANES], cos, sin)
    v = acc[:, C_WIDTH + LANES:C_WIDTH + 2 * LANES]
    low = lax.broadcasted_iota(jnp.int32, k.shape, 1) < C_DH
    for idx, x in enumerate((k, v)):
        swapped = pltpu.roll(x, C_DH, 1)
        groups = (jnp.where(low, x, 0.0), jnp.where(low, 0.0, swapped),
                  jnp.where(low, swapped, 0.0), jnp.where(low, 0.0, x))
        base = C_WIDTH + idx * C_EXP
        for c, val in enumerate(groups):
            o_ref[:, base + c * LANES:base + (c + 1) * LANES] = val.astype(o_ref.dtype)


def _project_window(h, w, cos, sin):
    rows, d = h.shape
    n = w.shape[1]
    tm = _row_tile(rows, 1024)
    tok_tiles = cos.shape[0] // tm
    t_spec = pl.BlockSpec((tm, LANES), lambda i: (i % tok_tiles, 0))
    return pl.pallas_call(
        _proj_window_kernel,
        grid=(rows // tm,),
        in_specs=[pl.BlockSpec((tm, d), lambda i: (i, 0)),
                  pl.BlockSpec((d, n), lambda i: (0, 0), pipeline_mode=pl.Buffered(1)),
                  t_spec, t_spec],
        out_specs=pl.BlockSpec((tm, C_OUT_WIDTH), lambda i: (i, 0)),
        out_shape=jax.ShapeDtypeStruct((rows, C_OUT_WIDTH), bf16),
        compiler_params=_params("arbitrary"),
        name="proj_window",
    )(h, w, cos, sin)


def _hgrn_kernel(*refs, rev, has_s0, emit_state, readout, hp):
    it = iter(refs)
    q_ref, v_ref, f_ref, lb_ref = next(it), next(it), next(it), next(it)
    s0_ref = next(it) if has_s0 else None
    if readout:
        g_ref, oprev_ref, ng_ref = next(it), next(it), next(it)
    o_ref = next(it)
    sout_ref = next(it) if emit_state else None
    st_ref = next(it)
    pat_ref, tri_ref = next(it), next(it)

    blk = pl.program_id(2)
    nblk = pl.num_programs(2)
    tb = q_ref.shape[0]
    chunk = HGRN_CHUNK
    nchunk = tb // chunk
    width = hp * A_DK
    heads = range(hp)

    def head(a, h):
        return a[:, h * A_DK:(h + 1) * A_DK]

    @pl.when(blk == 0)
    def _():
        if has_s0:
            st_ref[...] = s0_ref[...]
        else:
            st_ref[...] = jnp.zeros_like(st_ref)

    lb = lb_ref[...]
    f = lb + (1.0 - lb) * _sigmoid(f_ref[...])
    logf = jnp.log(f)
    k = 1.0 - f

    @pl.when((pl.program_id(0) == 0) & (pl.program_id(1) == 0) & (blk == 0))
    def _():
        row = lax.broadcasted_iota(jnp.int32, (tb, tb), 0)
        col = lax.broadcasted_iota(jnp.int32, (tb, tb), 1)
        same = (row // chunk) == (col // chunk)
        causal = (col >= row) if rev else (col <= row)
        ones = jnp.where(same & causal, 1.0, 0.0)
        pat_ref[...] = ones
        tri_ref[...] = ones.astype(bf16)

    mask = pat_ref[...] > 0.5
    tri = tri_ref[...]

    hi = logf.astype(bf16)
    lo = (logf - hi.astype(f32)).astype(bf16)
    cum2 = jnp.dot(tri, jnp.concatenate([hi, lo], axis=1), preferred_element_type=f32)
    cum = cum2[:, :width] + cum2[:, width:]

    mid_rows, mid_decay = [], []
    for j in range(nchunk):
        mid = j * chunk + (chunk // 2 if rev else chunk // 2 - 1)
        at_mid = cum[mid:mid + 1, :]
        mid_rows.append(jnp.broadcast_to(at_mid, (chunk, width)))
        mid_decay.append(jnp.broadcast_to(jnp.exp(at_mid), (chunk, width)))
    rel = cum - jnp.concatenate(mid_rows, axis=0)
    q_rel = _silu(q_ref[...]) * jnp.exp(rel)
    q_att = q_rel.astype(bf16)
    k_inv = (k * jnp.exp(-rel)).astype(bf16)
    q_dec = (q_rel * jnp.concatenate(mid_decay, axis=0)).astype(bf16)
    v = v_ref[...].astype(bf16)
    att = [lax.dot_general(head(q_att, h), head(k_inv, h), _NT, preferred_element_type=f32)
           for h in heads]
    att = [jnp.where(mask, a, 0.0).astype(bf16) for a in att]
    o_intra = [jnp.dot(att[h], head(v, h), preferred_element_type=f32) for h in heads]

    order = range(nchunk - 1, -1, -1) if rev else range(nchunk)
    k_end, dec = {}, {}
    for j in order:
        sl = slice(j * chunk, (j + 1) * chunk)
        last = j * chunk if rev else (j + 1) * chunk - 1
        tot = cum[last:last + 1, :]
        k_end[j] = (k[sl] * jnp.exp(tot - cum[sl])).astype(bf16)
        dec[j] = jnp.exp(tot)
    zero = jnp.zeros((chunk, A_DK), bf16)
    upd = {}
    for h in heads:
        k_diag = jnp.concatenate(
            [jnp.concatenate([head(k_end[j], h) if c == j else zero for c in range(nchunk)], axis=1)
             for j in range(nchunk)], axis=0)
        u_all = lax.dot_general(head(v, h), k_diag, _TN, preferred_element_type=f32)
        for j in range(nchunk):
            upd[h, j] = u_all[:, j * A_DK:(j + 1) * A_DK]

    before = {}
    final = []
    for h in heads:
        s = st_ref[h]
        for j in order:
            before[h, j] = s.astype(bf16)
            s = s * head(dec[j], h) + upd[h, j]
        st_ref[h] = s
        final.append(s)

    outs = []
    for h in heads:
        o_inter = [lax.dot_general(head(q_dec[j * chunk:(j + 1) * chunk], h), before[h, j], _NT,
                                   preferred_element_type=f32) for j in range(nchunk)]
        outs.append(o_intra[h] + jnp.concatenate(o_inter, axis=0))

    if readout:
        ng = ng_ref[...]
        normed = []
        for h in heads:
            o = outs[h] + head(oprev_ref[...], h)
            normed.append(o * lax.rsqrt(jnp.mean(o * o, axis=-1, keepdims=True) + EPS) * ng)
        o_ref[...] = (jnp.concatenate(normed, axis=1) * _silu(g_ref[...])).astype(o_ref.dtype)
    else:
        o_ref[...] = jnp.concatenate(outs, axis=1)

    if emit_state:
        @pl.when(blk == nblk - 1)
        def _():
            for h in heads:
                sout_ref[h] = final[h]


def _hgrn_scan(proj, lb_dir, seq, direction, *, s0=None, emit_state=False, readout=None):
    rows = proj.shape[0]
    batch = rows // seq
    tb = min(HGRN_BLOCK, seq)
    nblk = seq // tb
    rev = direction == 1
    hp = HGRN_HEADS_PER_STEP
    hblocks = A_HEADS // hp
    width = hp * A_DK

    def tok(b, h, i):
        return b * nblk + (nblk - 1 - i if rev else i)

    def col_spec(group):
        return pl.BlockSpec((tb, width), lambda b, h, i: (tok(b, h, i), group * hblocks + h))

    state_spec = pl.BlockSpec((None, hp, A_DV, A_DK), lambda b, h, i: (b, h, 0, 0))
    args = [proj, proj, proj, lb_dir.reshape(1, A_QK)]
    specs = [col_spec(0), col_spec(1), col_spec(2 + direction),
             pl.BlockSpec((1, width), lambda b, h, i: (0, h))]
    if s0 is not None:
        args.append(s0)
        specs.append(state_spec)
    if readout is not None:
        o_prev, norm_g = readout
        args += [proj, o_prev, norm_g.reshape(1, A_DV)]
        specs += [col_spec(4),
                  pl.BlockSpec((tb, width), lambda b, h, i: (tok(b, h, i), h)),
                  pl.BlockSpec((1, A_DV), lambda b, h, i: (0, 0))]
    out_shape = [jax.ShapeDtypeStruct((rows, A_WIDTH), bf16 if readout is not None else f32)]
    out_specs = [pl.BlockSpec((tb, width), lambda b, h, i: (tok(b, h, i), h))]
    if emit_state:
        out_shape.append(jax.ShapeDtypeStruct((batch, A_HEADS, A_DV, A_DK), f32))
        out_specs.append(state_spec)
    kern = functools.partial(_hgrn_kernel, rev=rev, has_s0=s0 is not None,
                             emit_state=emit_state, readout=readout is not None, hp=hp)
    res = pl.pallas_call(
        kern,
        grid=(batch, hblocks, nblk),
        in_specs=specs,
        out_specs=out_specs,
        out_shape=out_shape,
        scratch_shapes=[pltpu.VMEM((hp, A_DV, A_DK), f32), pltpu.VMEM((tb, tb), f32),
                        pltpu.VMEM((tb, tb), bf16)],
        compiler_params=_params("arbitrary", "arbitrary", "arbitrary"),
        name="hgrn_scan",
    )(*args)
    return res if emit_state else res[0]


def _hgrn_mixer(proj_lat, proj_ctx, lb, norm_g, seq, ctx_len, need_ctx):
    o_c_f, s_f = _hgrn_scan(proj_ctx, lb[0], ctx_len, 0, emit_state=True)
    o_l_f = _hgrn_scan(proj_lat, lb[0], seq, 0, s0=s_f)
    if need_ctx:
        a_ctx, s_b = _hgrn_scan(proj_ctx, lb[1], ctx_len, 1, emit_state=True, readout=(o_c_f, norm_g))
    else:
        _, s_b = _hgrn_scan(proj_ctx, lb[1], ctx_len, 1, emit_state=True)
        a_ctx = None
    a_lat = _hgrn_scan(proj_lat, lb[1], seq, 1, s0=s_b, readout=(o_l_f, norm_g))
    return a_lat, a_ctx


def _attend(parts, sink=None):
    m = parts[0][0].max(axis=-1, keepdims=True)
    for s, _ in parts[1:]:
        m = jnp.maximum(m, s.max(axis=-1, keepdims=True))
    if sink is not None:
        m = jnp.maximum(m, sink)
    den = None
    acc = None
    for s, v in parts:
        e = jnp.exp(s - m)
        d = e.sum(axis=-1, keepdims=True)
        o = jnp.dot(e.astype(bf16), v, preferred_element_type=f32)
        den = d if den is None else den + d
        acc = o if acc is None else acc + o
    if sink is not None:
        den = den + jnp.exp(sink - m)
    return acc / den


def _na_kernel(types_ref, q_ref, k_ref, v_ref, kc_ref, vc_ref, tbl_ref, o_ref, *, grid_rows, hp):
    del types_ref
    step = pl.program_id(2)
    span = NA_UNION * GRID_W
    first = jnp.clip(step * NA_QROWS - NA_ROWS // 2, 0, grid_rows - NA_UNION)
    start = pl.multiple_of(first * GRID_W, GRID_W)
    for h in range(hp):
        hs = slice(h * B_DH, (h + 1) * B_DH)
        q = q_ref[:, hs]
        kn = k_ref[pl.ds(start, span), hs]
        vn = v_ref[pl.ds(start, span), hs]
        s_nb = lax.dot_general(q, kn, _NT, preferred_element_type=f32) + tbl_ref[h]
        s_cx = lax.dot_general(q, kc_ref[:, hs], _NT, preferred_element_type=f32)
        o = _attend([(s_nb, vn), (s_cx, vc_ref[:, hs])])
        o_ref[:, hs] = o.astype(o_ref.dtype)


def _na_tables(rpb, grid_rows):
    assert grid_rows >= NA_UNION and grid_rows % NA_QROWS == 0
    col = np.arange(GRID_W)
    col_off = np.clip(col[None, :] - col[:, None] + NA_COLS - 1, 0, 2 * NA_COLS - 2)
    col_start = np.clip(col - NA_COLS // 2, 0, GRID_W - NA_COLS)
    col_ok = (col[None, :] >= col_start[:, None]) & (col[None, :] < col_start[:, None] + NA_COLS)
    seen, types = {}, []
    for i in range(grid_rows // NA_QROWS):
        first = int(np.clip(i * NA_QROWS - NA_ROWS // 2, 0, grid_rows - NA_UNION))
        key_row = first + np.arange(NA_UNION)[None, :]
        r = i * NA_QROWS + np.arange(NA_QROWS)[:, None]
        row_start = np.clip(r - NA_ROWS // 2, 0, grid_rows - NA_ROWS)
        ok = (key_row >= row_start) & (key_row < row_start + NA_ROWS)
        assert (ok.sum(axis=1) == NA_ROWS).all()
        off = np.where(ok, key_row - r + NA_ROWS - 1, 0)
        sig = (ok.tobytes(), off.tobytes())
        if sig not in seen:
            seen[sig] = (len(seen), ok, off)
        types.append(seen[sig][0])
    onehot = jnp.asarray((col_off[None] == np.arange(2 * NA_COLS - 1)[:, None, None]).astype(np.float32))
    toe = jnp.einsum("hoc,cqk->hoqk", rpb.astype(f32), onehot, precision=lax.Precision.HIGHEST)
    toe = jnp.where(col_ok[None, None], toe, NEG_INF)
    neg = jnp.full((rpb.shape[0], GRID_W, GRID_W), NEG_INF, f32)
    tables = []
    for _, ok, off in sorted(seen.values(), key=lambda t: t[0]):
        rows = [jnp.concatenate([toe[:, off[rl, a]] if ok[rl, a] else neg for a in range(NA_UNION)],
                                axis=-1) for rl in range(NA_QROWS)]
        tables.append(jnp.concatenate(rows, axis=-2))
    return jnp.stack(tables, axis=1), jnp.asarray(np.array(types, np.int32))


def _neighborhood_attention(qkv_lat, qkv_ctx, tables, types, seq, ctx_len):
    rows = qkv_lat.shape[0]
    batch = rows // seq
    tq = NA_QROWS * GRID_W
    nq = seq // tq
    hp = NA_HEADS_PER_STEP
    hb = B_HEADS // hp
    width = hp * B_DH
    kern = functools.partial(_na_kernel, grid_rows=seq // GRID_W, hp=hp)
    grid_spec = pltpu.PrefetchScalarGridSpec(
        num_scalar_prefetch=1,
        grid=(batch, hb, nq),
        in_specs=[
            pl.BlockSpec((tq, width), lambda b, h, i, t: (b * nq + i, h)),
            pl.BlockSpec((seq, width), lambda b, h, i, t: (b, hb + h), pipeline_mode=pl.Buffered(1)),
            pl.BlockSpec((seq, width), lambda b, h, i, t: (b, 2 * hb + h), pipeline_mode=pl.Buffered(1)),
            pl.BlockSpec((ctx_len, width), lambda b, h, i, t: (b, hb + h)),
            pl.BlockSpec((ctx_len, width), lambda b, h, i, t: (b, 2 * hb + h)),
            pl.BlockSpec((hp, None, tq, NA_UNION * GRID_W), lambda b, h, i, t: (h, t[i], 0, 0)),
        ],
        out_specs=pl.BlockSpec((tq, width), lambda b, h, i, t: (b * nq + i, h)),
    )
    return pl.pallas_call(
        kern,
        grid_spec=grid_spec,
        out_shape=jax.ShapeDtypeStruct((rows, B_WIDTH), bf16),
        compiler_params=_params("arbitrary", "arbitrary", "arbitrary"),
        name="neighborhood_attention",
    )(types, qkv_lat, qkv_lat, qkv_lat, qkv_ctx, qkv_ctx, tables)


def _window_kernel(sink_ref, q_ref, k_ref, v_ref, kc_ref, vc_ref, o_ref, *, seq):
    n = pl.program_id(1)
    span = 3 * C_BLOCK
    start = pl.multiple_of(jnp.clip((n - 1) * C_BLOCK, 0, seq - span), C_BLOCK)
    pairs = C_PAIRS // C_KV_HEADS
    rows = pairs * C_BLOCK
    qpos = n * C_BLOCK + lax.broadcasted_iota(jnp.int32, (C_BLOCK, span), 0)
    kpos = start + lax.broadcasted_iota(jnp.int32, (C_BLOCK, span), 1)
    off_window = jnp.where(jnp.abs(qpos - kpos) <= C_WINDOW, 0.0, NEG_INF)
    off_window = jnp.concatenate([off_window] * pairs, axis=0)
    pair_id = lax.broadcasted_iota(jnp.int32, (rows, 1), 0) // C_BLOCK

    chains = [(g, e) for g in range(C_KV_HEADS) for e in range(2)]
    q = {g: jnp.concatenate([q_ref[:, (g * pairs + p) * LANES:(g * pairs + p + 1) * LANES]
                             for p in range(pairs)], axis=0) for g in range(C_KV_HEADS)}
    s_w, s_c, sink = {}, {}, {}
    for g, e in chains:
        cs = slice((2 * g + e) * LANES, (2 * g + e + 1) * LANES)
        s = lax.dot_general(q[g], k_ref[pl.ds(start, span), cs], _NT, preferred_element_type=f32)
        s_w[g, e] = s + off_window
        s_c[g, e] = lax.dot_general(q[g], kc_ref[:, cs], _NT, preferred_element_type=f32)
        col = jnp.zeros((rows, 1), f32)
        for p in range(pairs):
            col = jnp.where(pair_id == p, sink_ref[(g * pairs + p) * 2 + e], col)
        sink[g, e] = col
    out = {}
    for g, e in chains:
        cs = slice((2 * g + e) * LANES, (2 * g + e + 1) * LANES)
        out[g, e] = _attend([(s_w[g, e], v_ref[pl.ds(start, span), cs]), (s_c[g, e], vc_ref[:, cs])],
                            sink=sink[g, e])
    for g in range(C_KV_HEADS):
        o = out[g, 0] + out[g, 1]
        for p in range(pairs):
            o_ref[:, (g * pairs + p) * LANES:(g * pairs + p + 1) * LANES] = (
                o[p * C_BLOCK:(p + 1) * C_BLOCK].astype(o_ref.dtype))


def _window_attention(qkv_lat, qkv_ctx, sink, seq, ctx_len):
    rows = qkv_lat.shape[0]
    batch = rows // seq
    nq = seq // C_BLOCK
    kblk = C_WIDTH // C_EXP
    kern = functools.partial(_window_kernel, seq=seq)
    resident = dict(pipeline_mode=pl.Buffered(1))
    return pl.pallas_call(
        kern,
        grid=(batch, nq),
        in_specs=[
            pl.BlockSpec(memory_space=pltpu.SMEM),
            pl.BlockSpec((C_BLOCK, C_WIDTH), lambda b, i: (b * nq + i, 0)),
            pl.BlockSpec((seq, C_EXP), lambda b, i: (b, kblk), **resident),
            pl.BlockSpec((seq, C_EXP), lambda b, i: (b, kblk + 1), **resident),
            pl.BlockSpec((ctx_len, C_EXP), lambda b, i: (b, kblk)),
            pl.BlockSpec((ctx_len, C_EXP), lambda b, i: (b, kblk + 1)),
        ],
        out_specs=pl.BlockSpec((C_BLOCK, C_WIDTH), lambda b, i: (b * nq + i, 0)),
        out_shape=jax.ShapeDtypeStruct((rows, C_WIDTH), bf16),
        compiler_params=_params("arbitrary", "arbitrary"),
        name="window_attention",
    )(sink.astype(f32), qkv_lat, qkv_lat, qkv_lat, qkv_ctx, qkv_ctx)


def _ctx_attn_kernel(*refs, heads, dh, k_cols, v_cols, has_sink):
    if has_sink:
        sink_ref, qkv_ref, o_ref = refs
    else:
        qkv_ref, o_ref = refs
    group = heads // len(k_cols)
    for kh, (kc, vc) in enumerate(zip(k_cols, v_cols)):
        k = qkv_ref[:, kc:kc + dh]
        v = qkv_ref[:, vc:vc + dh]
        for g in range(group):
            hq = kh * group + g
            q = qkv_ref[:, hq * dh:(hq + 1) * dh]
            s = lax.dot_general(q, k, _NT, preferred_element_type=f32)
            o = _attend([(s, v)], sink=sink_ref[hq] if has_sink else None)
            o_ref[:, hq * dh:(hq + 1) * dh] = o.astype(o_ref.dtype)


def _context_attention(qkv_ctx, ctx_len, heads, dh, k_cols, v_cols, sink=None):
    rows, width = qkv_ctx.shape
    kern = functools.partial(_ctx_attn_kernel, heads=heads, dh=dh, k_cols=k_cols, v_cols=v_cols,
                             has_sink=sink is not None)
    args, specs = [qkv_ctx], [pl.BlockSpec((ctx_len, width), lambda b: (b, 0))]
    if sink is not None:
        args.insert(0, sink.astype(f32))
        specs.insert(0, pl.BlockSpec(memory_space=pltpu.SMEM))
    return pl.pallas_call(
        kern,
        grid=(rows // ctx_len,),
        in_specs=specs,
        out_specs=pl.BlockSpec((ctx_len, heads * dh), lambda b: (b, 0)),
        out_shape=jax.ShapeDtypeStruct((rows, heads * dh), bf16),
        compiler_params=_params("arbitrary"),
        name="context_attention",
    )(*args)


def _merge_kernel(oa_ref, ob_ref, oc_ref, wa_ref, wb_ref, wc_ref, ga_ref, gb_ref, gc_ref, o_ref):
    oa, ob, oc = oa_ref[...], ob_ref[...], oc_ref[...]
    for cs in _col_chunks(o_ref.shape[1]):
        br_a = jnp.dot(oa, wa_ref[:, cs], preferred_element_type=f32)
        br_b = jnp.dot(ob, wb_ref[:, cs], preferred_element_type=f32)
        br_c = jnp.dot(oc, wc_ref[:, cs], preferred_element_type=f32)
        m = ga_ref[:, cs] * br_a + gb_ref[:, cs] * br_b + gc_ref[:, cs] * br_c
        o_ref[:, cs] = m.astype(o_ref.dtype)


def _merge(o_a, o_b, o_c, gates, w_branch, riders=()):
    rows = o_a.shape[0]
    d = w_branch.shape[1]
    tm = _row_tile(rows, 1024)
    tn = 512
    nj = d // tn
    assert A_WIDTH % B_WIDTH == 0 and B_WIDTH == C_WIDTH
    b_blk = A_WIDTH // B_WIDTH
    ride = _Riders(riders, (rows // tm) * nj, lambda i, j: i * nj + j)
    res = pl.pallas_call(
        ride.wrap(_merge_kernel, 9, 1),
        grid=(rows // tm, nj),
        in_specs=[
            pl.BlockSpec((tm, A_WIDTH), lambda i, j: (i, 0)),
            pl.BlockSpec((tm, B_WIDTH), lambda i, j: (i, 0)),
            pl.BlockSpec((tm, C_WIDTH), lambda i, j: (i, 0)),
            pl.BlockSpec((A_WIDTH, tn), lambda i, j: (0, j)),
            pl.BlockSpec((B_WIDTH, tn), lambda i, j: (b_blk, j)),
            pl.BlockSpec((C_WIDTH, tn), lambda i, j: (b_blk + 1, j)),
            pl.BlockSpec((tm, tn), lambda i, j: (i, j)),
            pl.BlockSpec((tm, tn), lambda i, j: (i, nj + j)),
            pl.BlockSpec((tm, tn), lambda i, j: (i, 2 * nj + j)),
        ] + ride.in_specs,
        out_specs=[pl.BlockSpec((tm, tn), lambda i, j: (i, j))] + ride.out_specs,
        out_shape=[jax.ShapeDtypeStruct((rows, d), bf16)] + ride.out_shapes,
        compiler_params=_params("arbitrary", "arbitrary"),
        name="merge_branches",
    )(o_a, o_b, o_c, w_branch, w_branch, w_branch, gates, gates, gates, *ride.args)
    (out,), extra = ride.split(res, 1)
    return out, extra


def _row_halves(m):
    return [slice(0, m // 2), slice(m // 2, m)] if m % 32 == 0 else [slice(0, m)]


def _residual_kernel(a_ref, w_ref, x_ref, gate_ref, o_ref):
    if o_ref.shape[1] > 256:
        a = a_ref[...]
        for cs in _col_chunks(o_ref.shape[1]):
            y = jnp.dot(a, w_ref[:, cs], preferred_element_type=f32)
            o_ref[:, cs] = x_ref[:, cs] + gate_ref[:, cs] * y
    else:
        w = w_ref[...]
        for rs in _row_halves(o_ref.shape[0]):
            y = jnp.dot(a_ref[rs, :], w, preferred_element_type=f32)
            o_ref[rs, :] = x_ref[rs, :] + gate_ref[...] * y


def _gated_residual_matmul(a, w, x, mod, gate_idx, rows_per_group, tn, name, a_buffers=1):
    rows, kdim = a.shape
    d = w.shape[1]
    tm = _row_tile(rows_per_group, 1024)
    per = rows_per_group // tm
    nj = d // tn
    single = dict(pipeline_mode=pl.Buffered(1)) if a_buffers == 1 else {}
    return pl.pallas_call(
        _residual_kernel,
        grid=(rows // tm, nj),
        in_specs=[
            pl.BlockSpec((tm, kdim), lambda i, j: (i, 0), **single),
            pl.BlockSpec((kdim, tn), lambda i, j: (0, j)),
            pl.BlockSpec((tm, tn), lambda i, j: (i, j)),
            pl.BlockSpec((None, 1, tn), lambda i, j: (i // per, 0, gate_idx * nj + j)),
        ],
        out_specs=pl.BlockSpec((tm, tn), lambda i, j: (i, j)),
        out_shape=jax.ShapeDtypeStruct((rows, d), f32),
        compiler_params=_params("arbitrary", "arbitrary"),
        name=name,
    )(a, w, x, mod)


def _residual_ksplit_kernel(a0_ref, a1_ref, w0_ref, w1_ref, x_ref, gate_ref, o_ref):
    w0, w1 = w0_ref[...], w1_ref[...]
    for rs in _row_halves(o_ref.shape[0]):
        y = (jnp.dot(a0_ref[rs, :], w0, preferred_element_type=f32)
             + jnp.dot(a1_ref[rs, :], w1, preferred_element_type=f32))
        o_ref[rs, :] = x_ref[rs, :] + gate_ref[...] * y


def _gated_residual_matmul_ksplit(a, w, x, mod, gate_idx, rows_per_group, tn, name):
    rows, kdim = a.shape
    d = w.shape[1]
    kh = kdim // 2
    assert kdim % 2 == 0 and kh % LANES == 0
    tm = _row_tile(rows_per_group, 1024)
    per = rows_per_group // tm
    nj = d // tn
    return pl.pallas_call(
        _residual_ksplit_kernel,
        grid=(rows // tm, nj),
        in_specs=[
            pl.BlockSpec((tm, kh), lambda i, j: (i, 0), pipeline_mode=pl.Buffered(1)),
            pl.BlockSpec((tm, kh), lambda i, j: (i, 1)),
            pl.BlockSpec((kh, tn), lambda i, j: (0, j)),
            pl.BlockSpec((kh, tn), lambda i, j: (1, j)),
            pl.BlockSpec((tm, tn), lambda i, j: (i, j)),
            pl.BlockSpec((None, 1, tn), lambda i, j: (i // per, 0, gate_idx * nj + j)),
        ],
        out_specs=pl.BlockSpec((tm, tn), lambda i, j: (i, j)),
        out_shape=jax.ShapeDtypeStruct((rows, d), f32),
        compiler_params=_params("arbitrary", "arbitrary"),
        name=name,
    )(a, a, w, w, x, mod)


def _swiglu_kernel(h_ref, wg_ref, wu_ref, o_ref):
    wg, wu = wg_ref[...], wu_ref[...]
    for rs in _row_halves(o_ref.shape[0]):
        h = h_ref[rs, :]
        g = jnp.dot(h, wg, preferred_element_type=f32)
        u = jnp.dot(h, wu, preferred_element_type=f32)
        o_ref[rs, :] = (g * _sigmoid_tanh(g) * u).astype(o_ref.dtype)


def _swiglu_up(h, w_gate, w_up, riders=()):
    rows, d = h.shape
    n = w_gate.shape[1]
    tm = _row_tile(rows, 1024)
    tn = FFN_TN
    nj = n // tn
    w_spec = pl.BlockSpec((d, tn), lambda i, j: (0, j))
    ride = _Riders(riders, (rows // tm) * nj, lambda i, j: i * nj + j)
    res = pl.pallas_call(
        ride.wrap(_swiglu_kernel, 3, 1),
        grid=(rows // tm, nj),
        in_specs=[pl.BlockSpec((tm, d), lambda i, j: (i, 0)), w_spec, w_spec] + ride.in_specs,
        out_specs=[pl.BlockSpec((tm, tn), lambda i, j: (i, j))] + ride.out_specs,
        out_shape=[jax.ShapeDtypeStruct((rows, n), bf16)] + ride.out_shapes,
        compiler_params=_params("arbitrary", "arbitrary"),
        name="swiglu_up",
    )(h, w_gate, w_up, *ride.args)
    (out,), extra = ride.split(res, 1)
    return out, extra


def _rope_tables(seq):
    half = C_DH // 2
    pos = jnp.arange(seq)
    inv = ROPE_BASE ** (-jnp.arange(0, half, 2, dtype=f32) / half)
    ang_row = (pos // GRID_W).astype(f32)[:, None] * inv[None, :]
    ang_col = (pos % GRID_W).astype(f32)[:, None] * inv[None, :]

    def one(ang):
        return (jnp.concatenate([jnp.cos(ang), jnp.cos(ang)], axis=-1),
                jnp.concatenate([-jnp.sin(ang), jnp.sin(ang)], axis=-1))

    cr, sr = one(ang_row)
    cc, sc = one(ang_col)
    cos = jnp.concatenate([cr, cc], axis=-1)
    sin = jnp.concatenate([sr, sc], axis=-1)
    reps = LANES // C_DH
    return jnp.tile(cos, (1, reps)), jnp.tile(sin, (1, reps))


def kernel(x, c, ctx, c_ctx, norm1_g, norm2_g, w_mod, b_mod, w_in, hgrn_lb, a_norm_g, na_rpb,
           c_sink, w_branch, w_out, w_ffn_gate, w_ffn_up, w_ffn_down, final_norm_g):
    batch, seq, d = x.shape
    ctx_len = ctx.shape[1]
    depth = w_in.shape[0]
    n_lat, n_ctx = batch * seq, batch * ctx_len

    lb_w = jax.nn.softmax(hgrn_lb.astype(f32), axis=0)
    lower_bounds = jnp.cumsum(lb_w, axis=0) - lb_w[:1]

    mod_rows = 8 * (-(-(batch + 1) // 8))
    c_rows = jnp.zeros((mod_rows, d), f32).at[:batch].set(c).at[batch].set(c_ctx)
    mod_all = _modulation(c_rows, w_mod, b_mod)

    cos_t, sin_t = _rope_tables(seq)
    ones_t = jnp.ones((n_ctx, LANES), f32)
    zeros_t = jnp.zeros((n_ctx, LANES), f32)

    a_hi = 3 * A_QK + 2 * A_WIDTH
    b_hi = a_hi + 3 * B_WIDTH
    c_hi = b_hi + C_WIDTH + 2 * C_KV_WIDTH
    in_width = w_in.shape[2]
    b_scale = jnp.concatenate([jnp.full((B_WIDTH,), B_DH ** -0.5, f32),
                               jnp.ones((2 * B_WIDTH,), f32)]).reshape(1, -1)
    ck_cols = tuple(C_WIDTH + 2 * g * LANES for g in range(C_KV_HEADS))
    cv_cols = tuple(C_WIDTH + C_EXP + 2 * g * LANES for g in range(C_KV_HEADS))
    bk_cols = tuple(B_WIDTH + h * B_DH for h in range(B_HEADS))
    bv_cols = tuple(2 * B_WIDTH + h * B_DH for h in range(B_HEADS))

    x_lat = x.reshape(n_lat, d)
    x_ctx = ctx.reshape(n_ctx, d)

    in_splits = ((0, a_hi), (a_hi, b_hi), (b_hi, c_hi), (c_hi, in_width))
    w_in_cast = _cast_weight(w_in, 0, in_splits)

    for l in range(depth):
        need_ctx = l < depth - 1
        mod_l = mod_all[l, :batch].reshape(batch, 1, N_MOD * d)
        mod_c = mod_all[l, batch:batch + 1].reshape(1, 1, N_MOD * d)
        w_a, w_b, w_c, w_g = w_in_cast
        na_tables, na_types = _na_tables(na_rpb[l], seq // GRID_W)

        h_lat = _norm_modulate(x_lat, norm1_g[l], mod_l, 0, seq)
        h_ctx = _norm_modulate(x_ctx, norm1_g[l], mod_c, 0, n_ctx)

        pa_lat, (wfu,) = _project(h_lat, w_a, f32, 1024, riders=[(w_ffn_up, l, None)], name="proj_hgrn")
        pb_lat, (wbr, wo) = _project(h_lat, w_b, bf16, 1024, col_scale=b_scale,
                                     riders=[(w_branch, l, None), (w_out, l, None)], name="proj_na")
        pc_lat = _project_window(h_lat, w_c, cos_t, sin_t)
        g_lat, (wfg, wfd) = _project(h_lat, w_g, bf16, 1024, sigmoid=True,
                                     riders=[(w_ffn_gate, l, None), (w_ffn_down, l, None)], name="proj_gates")
        pa_ctx, _ = _project(h_ctx, w_a, f32, 1024, name="proj_hgrn")
        pb_ctx, _ = _project(h_ctx, w_b, bf16, 512, col_scale=b_scale, name="proj_na")
        pc_ctx = _project_window(h_ctx, w_c, ones_t, zeros_t)

        a_lat, a_ctx = _hgrn_mixer(pa_lat, pa_ctx, lower_bounds[l], a_norm_g[l], seq, ctx_len, need_ctx)
        b_lat = _neighborhood_attention(pb_lat, pb_ctx, na_tables, na_types, seq, ctx_len)
        c_lat = _window_attention(pc_lat, pc_ctx, c_sink[l], seq, ctx_len)

        m_lat, _ = _merge(a_lat, b_lat, c_lat, g_lat, wbr)
        x_lat = _gated_residual_matmul(m_lat, wo, x_lat, mod_l, 2, seq, 512, "out_proj", a_buffers=2)
        h2 = _norm_modulate(x_lat, norm2_g[l], mod_l, 3, seq)
        next_in = [(w_in, l + 1, in_splits)] if l + 1 < depth else []
        u, nxt = _swiglu_up(h2, wfg, wfu, riders=next_in)
        if nxt:
            w_in_cast = nxt[0]
        x_lat = _gated_residual_matmul_ksplit(u, wfd, x_lat, mod_l, 5, seq, FFN_TN, "ffn_down")

        if need_ctx:
            g_ctx, _ = _project(h_ctx, w_g, bf16, 1024, sigmoid=True, name="proj_gates")
            b_ctx = _context_attention(pb_ctx, ctx_len, B_HEADS, B_DH, bk_cols, bv_cols)
            c_ctx_o = _context_attention(pc_ctx, ctx_len, C_HEADS, C_DH, ck_cols, cv_cols, sink=c_sink[l])
            m_ctx, _ = _merge(a_ctx, b_ctx, c_ctx_o, g_ctx, wbr)
            x_ctx = _gated_residual_matmul(m_ctx, wo, x_ctx, mod_c, 2, n_ctx, 1024, "out_proj")
            h2c = _norm_modulate(x_ctx, norm2_g[l], mod_c, 3, n_ctx)
            uc, _ = _swiglu_up(h2c, wfg, wfu)
            x_ctx = _gated_residual_matmul(uc, wfd, x_ctx, mod_c, 5, n_ctx, FFN_TN, "ffn_down")

    return _final_norm(x_lat, final_norm_g).reshape(batch, seq, d)
```
